```python
import math
import jax, jax.numpy as jnp
from jax import lax
import numpy as np

D_MODEL = 1024
BATCH = 16
SEQ = 2048
DEPTH = 1
DEC_BATCH = 32
DEC_SEQ = 4
PAST_LEN = 16384
PAGE_SIZE = 128

SSM_WIDTH = D_MODEL // 2
SSM_GROUP = 16
SSM_GROUPS = SSM_WIDTH // SSM_GROUP
SSM_STATE = 64
DT_MIN = 0.001
DT_MAX = 0.1
ATT_WIDTH = D_MODEL // 2
ATT_HEADS = 4
V_DIM = ATT_WIDTH // ATT_HEADS
HEAD_DIM = V_DIM // 2
QK_DIM = 2 * HEAD_DIM
IN_COLS = SSM_WIDTH + 3 * ATT_WIDTH
ROPE_THETA = 10000.0
Q_BLOCK = 128
NEG_INF = -1e30
N_EXPERTS = 32
TOP_K = 4
D_EXPERT = D_MODEL
SWIGLU_LIMIT = 7.0
SWIGLU_ALPHA = 1.702
MOE_BLOCK = 128
NORM_EPS = 1e-6

kernel_name = 'hybrid_s5_diffattn_moe_step'


def rms_norm(x, g):
    xf = x.astype(jnp.float32)
    y = xf * lax.rsqrt(jnp.mean(xf * xf, axis=-1, keepdims=True) + NORM_EPS)
    return (y * g.astype(jnp.float32)).astype(x.dtype)


def rotary(x, pos):
    half = HEAD_DIM // 2
    inv_freq = ROPE_THETA ** (-jnp.arange(half, dtype=jnp.float32) / half)
    ang = pos.astype(jnp.float32)[:, None] * inv_freq[None, :]
    cos = jnp.cos(ang)[None, :, None, None, :]
    sin = jnp.sin(ang)[None, :, None, None, :]
    xf = x.astype(jnp.float32)
    x1, x2 = xf[..., :half], xf[..., half:]
    return jnp.concatenate([x1 * cos - x2 * sin, x2 * cos + x1 * sin], axis=-1).astype(x.dtype)


def _complex_affine_combine(e1, e2):
    a1r, a1i, b1r, b1i = e1
    a2r, a2i, b2r, b2i = e2
    return (a2r * a1r - a2i * a1i,
            a2r * a1i + a2i * a1r,
            a2r * b1r - a2i * b1i + b2r,
            a2r * b1i + a2i * b1r + b2i)


def s5_scan(u, x0_re, x0_im, lam_re, lam_im, log_dt, b_re, b_im, c_re, c_im, d_skip):
    f32 = jnp.float32
    lam_re = lam_re.astype(f32)
    lam_im = lam_im.astype(f32)
    dt = jnp.exp(log_dt.astype(f32))[:, None]
    mag = jnp.exp(lam_re * dt)
    a_re = mag * jnp.cos(lam_im * dt)
    a_im = mag * jnp.sin(lam_im * dt)
    den = lam_re * lam_re + lam_im * lam_im
    f_re = ((a_re - 1.0) * lam_re + a_im * lam_im) / den
    f_im = (a_im * lam_re - (a_re - 1.0) * lam_im) / den
    br = b_re.astype(f32)
    bi = b_im.astype(f32)
    bb_re = f_re[..., None] * br - f_im[..., None] * bi
    bb_im = f_re[..., None] * bi + f_im[..., None] * br
    uf = u.astype(f32)
    z_re = jnp.einsum('btgh,gph->btgp', uf, bb_re)
    z_im = jnp.einsum('btgh,gph->btgp', uf, bb_im)
    A_re = jnp.broadcast_to(a_re, z_re.shape)
    A_im = jnp.broadcast_to(a_im, z_im.shape)
    P_re, P_im, S_re, S_im = lax.associative_scan(
        _complex_affine_combine, (A_re, A_im, z_re, z_im), axis=1)
    x0r = x0_re.astype(f32)[:, None]
    x0i = x0_im.astype(f32)[:, None]
    x_re = P_re * x0r - P_im * x0i + S_re
    x_im = P_re * x0i + P_im * x0r + S_im
    y = (jnp.einsum('btgp,ghp->btgh', x_re, c_re.astype(f32))
         - jnp.einsum('btgp,ghp->btgh', x_im, c_im.astype(f32))
         + d_skip.astype(f32) * uf)
    return y, x_re[:, -1], x_im[:, -1]


def diff_attend(q, q_pos, k_segs, v_segs, kpos_segs, lam):
    scale = HEAD_DIM ** -0.5
    scores = []
    for k, kp in zip(k_segs, kpos_segs):
        s = jnp.einsum('bqhcd,bkhcd->bhcqk', q, k, preferred_element_type=jnp.float32) * scale
        causal = kp[None, :] <= q_pos[:, None]
        scores.append(jnp.where(causal[None, None, None], s, NEG_INF))
    p = jax.nn.softmax(jnp.concatenate(scores, axis=-1), axis=-1)
    w = p[:, :, 0] - lam * p[:, :, 1]
    outs = []
    off = 0
    for v in v_segs:
        n = v.shape[1]
        outs.append(jnp.einsum('bhqk,bkhe->bqhe', w[..., off:off + n].astype(v.dtype), v,
                               preferred_element_type=jnp.float32))
        off += n
    return sum(outs)


def moe(h, w_router, b_router, w_gu, b_gu, w_down, b_down):
    n_tok = h.shape[0]
    logits = (h @ w_router + b_router).astype(jnp.float32)
    top_val, top_idx = lax.top_k(logits, TOP_K)
    gate = jax.nn.softmax(top_val, axis=-1)
    m = n_tok * TOP_K
    e_flat = top_idx.reshape(m).astype(jnp.int32)
    t_flat = jnp.repeat(jnp.arange(n_tok, dtype=jnp.int32), TOP_K)
    g_flat = gate.reshape(m)
    order = jnp.argsort(e_flat)
    e_sorted = e_flat[order]
    t_sorted = t_flat[order]
    g_sorted = g_flat[order]
    counts = jnp.bincount(e_flat, length=N_EXPERTS)
    padded = (counts + MOE_BLOCK - 1) // MOE_BLOCK * MOE_BLOCK
    start = jnp.cumsum(counts) - counts
    pend = jnp.cumsum(padded)
    pstart = pend - padded
    dest = pstart[e_sorted] + jnp.arange(m, dtype=jnp.int32) - start[e_sorted]
    n_rows = (m + MOE_BLOCK - 1) // MOE_BLOCK * MOE_BLOCK + N_EXPERTS * MOE_BLOCK
    n_blocks = n_rows // MOE_BLOCK
    row_tok = jnp.full((n_rows,), n_tok, jnp.int32).at[dest].set(t_sorted)
    row_gate = jnp.zeros((n_rows,), jnp.float32).at[dest].set(g_sorted)
    blk_start = jnp.arange(n_blocks, dtype=jnp.int32) * MOE_BLOCK
    blk_expert = jnp.minimum(jnp.searchsorted(pend, blk_start, side='right'), N_EXPERTS - 1)
    h_pad = jnp.concatenate([h, jnp.zeros((1, h.shape[1]), h.dtype)], axis=0)
    xs = h_pad[row_tok].reshape(n_blocks, MOE_BLOCK, h.shape[1])

    def expert_block(args):
        xb, e = args
        gu = xb @ w_gu[e] + b_gu[e]
        g_lin = jnp.minimum(gu[:, 0::2], SWIGLU_LIMIT)
        up = jnp.clip(gu[:, 1::2], -SWIGLU_LIMIT, SWIGLU_LIMIT)
        act = (up + 1.0) * (g_lin * jax.nn.sigmoid(SWIGLU_ALPHA * g_lin))
        return act @ w_down[e] + b_down[e]

    ys = lax.map(expert_block, (xs, blk_expert)).reshape(n_rows, h.shape[1])
    out = jax.ops.segment_sum(ys.astype(jnp.float32) * row_gate[:, None], row_tok,
                              num_segments=n_tok + 1)[:n_tok]
    return out.astype(h.dtype)


def trunk_layer(x, pos, ssm_x0_re, ssm_x0_im, past_k, past_v, past_pos, lp, lambda_init):
    bsz, t_len, _ = x.shape
    h = rms_norm(x, lp['norm_mix_g'])
    proj = h @ lp['w_in']
    u = proj[..., :SSM_WIDTH]
    q = proj[..., SSM_WIDTH:SSM_WIDTH + ATT_WIDTH].reshape(bsz, t_len, ATT_HEADS, 2, HEAD_DIM)
    k = proj[..., SSM_WIDTH + ATT_WIDTH:SSM_WIDTH + 2 * ATT_WIDTH].reshape(
        bsz, t_len, ATT_HEADS, 2, HEAD_DIM)
    v = proj[..., SSM_WIDTH + 2 * ATT_WIDTH:].reshape(bsz, t_len, ATT_HEADS, V_DIM)

    y_ssm, st_re, st_im = s5_scan(
        u.reshape(bsz, t_len, SSM_GROUPS, SSM_GROUP), ssm_x0_re, ssm_x0_im,
        lp['ssm_lambda_re'], lp['ssm_lambda_im'], lp['ssm_log_dt'],
        lp['ssm_b_re'], lp['ssm_b_im'], lp['ssm_c_re'], lp['ssm_c_im'], lp['ssm_d'])
    y_ssm = jax.nn.gelu(y_ssm.reshape(bsz, t_len, SSM_WIDTH)).astype(x.dtype)
    y_ssm = y_ssm * jax.nn.sigmoid(y_ssm @ lp['ssm_w_glu'] + lp['ssm_b_glu'])

    q = rotary(q, pos)
    k = rotary(k, pos)
    f32 = jnp.float32
    lam = (jnp.exp(jnp.sum(lp['lambda_q1'].astype(f32) * lp['lambda_k1'].astype(f32)))
           - jnp.exp(jnp.sum(lp['lambda_q2'].astype(f32) * lp['lambda_k2'].astype(f32)))
           + lambda_init)
    if past_k is None:
        n_blk = t_len // Q_BLOCK
        q_blocks = jnp.moveaxis(q.reshape(bsz, n_blk, Q_BLOCK, ATT_HEADS, 2, HEAD_DIM), 1, 0)
        pos_blocks = pos.reshape(n_blk, Q_BLOCK)
        o = lax.map(lambda qp: diff_attend(qp[0], qp[1], [k], [v], [pos], lam),
                    (q_blocks, pos_blocks))
        o = jnp.moveaxis(o, 0, 1).reshape(bsz, t_len, ATT_HEADS, V_DIM)
    else:
        o = diff_attend(q, pos, [past_k, k], [past_v, v], [past_pos, pos], lam)
    o = rms_norm(o.astype(x.dtype), lp['attn_subln_g']) * (1.0 - lambda_init)
    o = o.reshape(bsz, t_len, ATT_WIDTH)

    gates = jax.nn.sigmoid(h @ lp['w_gate'] + lp['b_gate']).reshape(bsz, t_len, 2, D_MODEL)
    merged = (gates[:, :, 0] * (y_ssm @ lp['w_branch_ssm'])
              + gates[:, :, 1] * (o @ lp['w_branch_attn']))
    x = x + merged @ lp['w_out']

    h2 = rms_norm(x, lp['norm_ffn_g']).reshape(bsz * t_len, D_MODEL)
    x = x + moe(h2, lp['w_router'], lp['b_router'], lp['w_gate_up'], lp['b_gate_up'],
                lp['w_down'], lp['b_down']).reshape(bsz, t_len, D_MODEL)
    new_k = k.reshape(bsz, t_len, ATT_HEADS, QK_DIM)
    return x, new_k, v, st_re.astype(x.dtype), st_im.astype(x.dtype)


def setup_inputs(seed: int = 0) -> dict:
    key = jax.random.key(seed)
    ks = jax.random.split(key, 40)
    f32 = jnp.float32
    n_pages = PAST_LEN // PAGE_SIZE
    n_used = DEC_BATCH * n_pages
    n_phys = n_used + max(1, n_used // 4)

    def normal(k, shape, scale):
        return jax.random.normal(k, shape, f32) * scale

    def gain(k, shape):
        return 1.0 + 0.05 * jax.random.normal(k, shape, f32)

    page_table = jax.random.permutation(ks[6], n_phys)[:n_used].reshape(
        DEC_BATCH, n_pages).astype(jnp.int32)
    lam_im_base = jnp.pi * jnp.arange(SSM_STATE, dtype=f32)
    return {
        'x_prompt': normal(ks[0], (BATCH, SEQ, D_MODEL), 1.0),
        'x_sample': normal(ks[1], (DEC_BATCH, DEC_SEQ, D_MODEL), 1.0),
        'cache_k': normal(ks[2], (DEPTH, n_phys, PAGE_SIZE, ATT_HEADS, QK_DIM), 1.0),
        'cache_v': normal(ks[3], (DEPTH, n_phys, PAGE_SIZE, ATT_HEADS, V_DIM), 1.0),
        'state_ssm_re': normal(ks[4], (DEPTH, DEC_BATCH, SSM_GROUPS, SSM_STATE), 0.3),
        'state_ssm_im': normal(ks[5], (DEPTH, DEC_BATCH, SSM_GROUPS, SSM_STATE), 0.3),
        'page_table': page_table,
        'norm_mix_g': gain(ks[7], (DEPTH, D_MODEL)),
        'w_in': normal(ks[8], (DEPTH, D_MODEL, IN_COLS), D_MODEL ** -0.5),
        'ssm_lambda_re': -0.5 + 0.01 * jax.random.normal(ks[9], (DEPTH, SSM_GROUPS, SSM_STATE), f32),
        'ssm_lambda_im': lam_im_base + 0.01 * jax.random.normal(ks[10], (DEPTH, SSM_GROUPS, SSM_STATE), f32),
        'ssm_log_dt': jax.random.uniform(ks[11], (DEPTH, SSM_GROUPS), dtype=f32,
                                         minval=math.log(DT_MIN), maxval=math.log(DT_MAX)),
        'ssm_b_re': normal(ks[12], (DEPTH, SSM_GROUPS, SSM_STATE, SSM_GROUP), (2 * SSM_GROUP) ** -0.5),
        'ssm_b_im': normal(ks[13], (DEPTH, SSM_GROUPS, SSM_STATE, SSM_GROUP), (2 * SSM_GROUP) ** -0.5),
        'ssm_c_re': normal(ks[14], (DEPTH, SSM_GROUPS, SSM_GROUP, SSM_STATE), SSM_STATE ** -0.5),
        'ssm_c_im': normal(ks[15], (DEPTH, SSM_GROUPS, SSM_GROUP, SSM_STATE), SSM_STATE ** -0.5),
        'ssm_d': normal(ks[16], (DEPTH, SSM_GROUPS, SSM_GROUP), 1.0),
        'ssm_w_glu': normal(ks[17], (DEPTH, SSM_WIDTH, SSM_WIDTH), SSM_WIDTH ** -0.5),
        'ssm_b_glu': normal(ks[18], (DEPTH, SSM_WIDTH), 0.01),
        'lambda_q1': normal(ks[19], (DEPTH, HEAD_DIM), 0.1),
        'lambda_k1': normal(ks[20], (DEPTH, HEAD_DIM), 0.1),
        'lambda_q2': normal(ks[21], (DEPTH, HEAD_DIM), 0.1),
        'lambda_k2': normal(ks[22], (DEPTH, HEAD_DIM), 0.1),
        'attn_subln_g': gain(ks[23], (DEPTH, V_DIM)),
        'w_branch_ssm': normal(ks[24], (DEPTH, SSM_WIDTH, D_MODEL), SSM_WIDTH ** -0.5),
        'w_branch_attn': normal(ks[25], (DEPTH, ATT_WIDTH, D_MODEL), ATT_WIDTH ** -0.5),
        'w_gate': normal(ks[26], (DEPTH, D_MODEL, 2 * D_MODEL), D_MODEL ** -0.5),
        'b_gate': normal(ks[27], (DEPTH, 2 * D_MODEL), 0.01),
        'w_out': normal(ks[28], (DEPTH, D_MODEL, D_MODEL), D_MODEL ** -0.5),
        'norm_ffn_g': gain(ks[29], (DEPTH, D_MODEL)),
        'w_router': normal(ks[30], (DEPTH, D_MODEL, N_EXPERTS), D_MODEL ** -0.5),
        'b_router': normal(ks[31], (DEPTH, N_EXPERTS), 0.01),
        'w_gate_up': normal(ks[32], (DEPTH, N_EXPERTS, D_MODEL, 2 * D_EXPERT), D_MODEL ** -0.5),
        'b_gate_up': normal(ks[33], (DEPTH, N_EXPERTS, 2 * D_EXPERT), 0.01),
        'w_down': normal(ks[34], (DEPTH, N_EXPERTS, D_EXPERT, D_MODEL), D_EXPERT ** -0.5),
        'b_down': normal(ks[35], (DEPTH, N_EXPERTS, D_MODEL), 0.01),
        'norm_final_g': gain(ks[36], (D_MODEL,)),
    }


def reference(x_prompt, x_sample, cache_k, cache_v, state_ssm_re, state_ssm_im, page_table,
              norm_mix_g, w_in, ssm_lambda_re, ssm_lambda_im, ssm_log_dt, ssm_b_re, ssm_b_im,
              ssm_c_re, ssm_c_im, ssm_d, ssm_w_glu, ssm_b_glu, lambda_q1, lambda_k1, lambda_q2,
              lambda_k2, attn_subln_g, w_branch_ssm, w_branch_attn, w_gate, b_gate, w_out,
              norm_ffn_g, w_router, b_router, w_gate_up, b_gate_up, w_down, b_down, norm_final_g):
    n_pages = PAST_LEN // PAGE_SIZE
    past_len = n_pages * PAGE_SIZE
    dec_b = x_sample.shape[0]
    pos_prompt = jnp.arange(x_prompt.shape[1], dtype=jnp.int32)
    pos_sample = past_len + jnp.arange(x_sample.shape[1], dtype=jnp.int32)
    past_pos = jnp.arange(past_len, dtype=jnp.int32)
    zero_state = jnp.zeros((x_prompt.shape[0], SSM_GROUPS, SSM_STATE), jnp.float32)

    xp = x_prompt
    xs = x_sample
    kp_l, vp_l, srp_l, sip_l = [], [], [], []
    ks_l, vs_l, srs_l, sis_l = [], [], [], []
    for l in range(DEPTH):
        lp = {
            'norm_mix_g': norm_mix_g[l], 'w_in': w_in[l],
            'ssm_lambda_re': ssm_lambda_re[l], 'ssm_lambda_im': ssm_lambda_im[l],
            'ssm_log_dt': ssm_log_dt[l], 'ssm_b_re': ssm_b_re[l], 'ssm_b_im': ssm_b_im[l],
            'ssm_c_re': ssm_c_re[l], 'ssm_c_im': ssm_c_im[l], 'ssm_d': ssm_d[l],
            'ssm_w_glu': ssm_w_glu[l], 'ssm_b_glu': ssm_b_glu[l],
            'lambda_q1': lambda_q1[l], 'lambda_k1': lambda_k1[l],
            'lambda_q2': lambda_q2[l], 'lambda_k2': lambda_k2[l],
            'attn_subln_g': attn_subln_g[l],
            'w_branch_ssm': w_branch_ssm[l], 'w_branch_attn': w_branch_attn[l],
            'w_gate': w_gate[l], 'b_gate': b_gate[l], 'w_out': w_out[l],
            'norm_ffn_g': norm_ffn_g[l], 'w_router': w_router[l], 'b_router': b_router[l],
            'w_gate_up': w_gate_up[l], 'b_gate_up': b_gate_up[l],
            'w_down': w_down[l], 'b_down': b_down[l],
        }
        lambda_init = 0.8 - 0.6 * math.exp(-0.3 * l)
        xp, kp, vp, srp, sip = trunk_layer(xp, pos_prompt, zero_state, zero_state,
                                           None, None, None, lp, lambda_init)
        past_k = cache_k[l][page_table].reshape(dec_b, past_len, ATT_HEADS, 2, HEAD_DIM)
        past_v = cache_v[l][page_table].reshape(dec_b, past_len, ATT_HEADS, V_DIM)
        xs, ks_, vs_, srs, sis = trunk_layer(xs, pos_sample, state_ssm_re[l], state_ssm_im[l],
                                             past_k, past_v, past_pos, lp, lambda_init)
        kp_l.append(kp); vp_l.append(vp); srp_l.append(srp); sip_l.append(sip)
        ks_l.append(ks_); vs_l.append(vs_); srs_l.append(srs); sis_l.append(sis)

    y_prompt = rms_norm(xp, norm_final_g)
    y_sample = rms_norm(xs, norm_final_g)
    k_prompt = jnp.stack(kp_l)
    v_prompt = jnp.stack(vp_l)
    ssm_re_prompt = jnp.stack(srp_l)
    ssm_im_prompt = jnp.stack(sip_l)
    k_sample = jnp.stack(ks_l)
    v_sample = jnp.stack(vs_l)
    ssm_re_sample = jnp.stack(srs_l)
    ssm_im_sample = jnp.stack(sis_l)
    return (y_prompt, y_sample, k_prompt, v_prompt, ssm_re_prompt, ssm_im_prompt,
            k_sample, v_sample, ssm_re_sample, ssm_im_sample)
```

```python
import functools
import math

import jax
import jax.numpy as jnp
from jax import lax
from jax.experimental import pallas as pl
from jax.experimental.pallas import tpu as pltpu

F32 = jnp.float32
BF16 = jnp.bfloat16

NORM_EPS = 1e-6
ROPE_THETA = 10000.0
NEG_INF = -1e30
TOP_K = 4
SWIGLU_LIMIT = 7.0
SWIGLU_ALPHA = 1.702
LANES = 128
MOE_BLOCK_ROWS = 256
VMEM_LIMIT = 56 * 1024 * 1024


def _cparams(sem):
    return pltpu.CompilerParams(dimension_semantics=sem, vmem_limit_bytes=VMEM_LIMIT)


def _dot(a, b):
    return jnp.dot(a, b, preferred_element_type=F32)


def _dot_nt(a, b):
    return lax.dot_general(a, b, (((1,), (1,)), ((), ())), preferred_element_type=F32)


def _rms(x, g):
    return x * lax.rsqrt(jnp.mean(x * x, axis=-1, keepdims=True) + NORM_EPS) * g


def _proj_kernel(x_ref, g_ref, win_ref, wgate_ref, bgate_ref, cos_ref, sin_ref,
                 u_ref, q_ref, k_ref, v_ref, kb_ref, vb_ref, gates_ref, *, half, scale):
    hb = _rms(x_ref[...], g_ref[...]).astype(BF16)
    proj = _dot(hb, win_ref[...])
    w = u_ref.shape[-1]
    u_ref[...] = proj[:, :w]
    cos = cos_ref[...]
    sin = sin_ref[...]
    lane = lax.broadcasted_iota(jnp.int32, cos.shape, 1)
    lo = (lane % (2 * half)) < half

    def rot(xh):
        fwd = pltpu.roll(xh, LANES - half, axis=1)
        bwd = pltpu.roll(xh, half, axis=1)
        return xh * cos + jnp.where(lo, fwd, bwd) * sin

    for hh in range(w // LANES):
        sl = slice(hh * LANES, (hh + 1) * LANES)
        qh = rot(proj[:, w + hh * LANES:w + (hh + 1) * LANES])
        q_ref[:, sl] = (qh * scale).astype(BF16)
        kh = rot(proj[:, 2 * w + hh * LANES:2 * w + (hh + 1) * LANES])
        k_ref[:, sl] = kh
        kb_ref[:, sl] = kh.astype(BF16)
    v = proj[:, 3 * w:]
    v_ref[...] = v
    vb_ref[...] = v.astype(BF16)
    gl = _dot(hb, wgate_ref[...]) + bgate_ref[...]
    gates_ref[...] = jax.nn.sigmoid(gl).astype(BF16)


def _proj(x2d, n_batch, t_len, tm, g, win_b, wgate_b, bgate, cos_t, sin_t, half, scale):
    n, d = x2d.shape
    w = win_b.shape[1] // 4
    nt = t_len // tm
    row = lambda b, t: (b * nt + t, 0)
    const = lambda b, t: (0, 0)
    outs = pl.pallas_call(
        functools.partial(_proj_kernel, half=half, scale=scale),
        grid=(n_batch, nt),
        in_specs=[pl.BlockSpec((tm, d), row),
                  pl.BlockSpec((1, d), const),
                  pl.BlockSpec(win_b.shape, const),
                  pl.BlockSpec(wgate_b.shape, const),
                  pl.BlockSpec((1, wgate_b.shape[1]), const),
                  pl.BlockSpec((tm, LANES), lambda b, t: (t, 0)),
                  pl.BlockSpec((tm, LANES), lambda b, t: (t, 0))],
        out_specs=[pl.BlockSpec((tm, w), lambda b, t: (t, b)),
                   pl.BlockSpec((tm, w), row),
                   pl.BlockSpec((tm, w), row),
                   pl.BlockSpec((tm, w), row),
                   pl.BlockSpec((tm, w), row),
                   pl.BlockSpec((tm, w), row),
                   pl.BlockSpec((tm, 2 * d), row)],
        out_shape=[jax.ShapeDtypeStruct((t_len, n_batch * w), F32),
                   jax.ShapeDtypeStruct((n, w), BF16),
                   jax.ShapeDtypeStruct((n, w), F32),
                   jax.ShapeDtypeStruct((n, w), F32),
                   jax.ShapeDtypeStruct((n, w), BF16),
                   jax.ShapeDtypeStruct((n, w), BF16),
                   jax.ShapeDtypeStruct((n, 2 * d), BF16)],
        compiler_params=_cparams(("parallel", "parallel")),
        name="proj",
    )(x2d, g, win_b, wgate_b, bgate, cos_t, sin_t)
    return outs


def _s5_kernel(u_ref, x0_ref, lre_ref, lim_ref, ldt_ref, bre_ref, bim_ref, cre_ref, cim_ref,
               d_ref, wglu_ref, bglu_ref, y_ref, st_ref,
               zx_ref, wz_ref, are_ref, aim_ref, state_ref, *, nb, tc, ns, sw):
    step = pl.program_id(0)

    @pl.when(step == 0)
    def _init():
        lre = lre_ref[...]
        lim = lim_ref[...]
        dt = jnp.exp(ldt_ref[...])
        mag = jnp.exp(lre * dt)
        are = mag * jnp.cos(lim * dt)
        aim = mag * jnp.sin(lim * dt)
        den = lre * lre + lim * lim
        fre = ((are - 1.0) * lre + aim * lim) / den
        fim = (aim * lre - (are - 1.0) * lim) / den
        are_ref[...] = are
        aim_ref[...] = aim
        for i in range(ns):
            fr = fre[:, i * sw:(i + 1) * sw]
            fi = fim[:, i * sw:(i + 1) * sw]
            br = bre_ref[i]
            bi = bim_ref[i]
            wz_ref[i, :, :sw] = (fr * br - fi * bi).astype(BF16)
            wz_ref[i, :, sw:] = (fr * bi + fi * br).astype(BF16)
        state_ref[...] = x0_ref[...]

    ub = u_ref[...]
    ubb = ub.astype(BF16)
    for i in range(ns):
        zx_ref[:, 2 * sw * i:2 * sw * (i + 1)] = _dot(ubb[:, LANES * i:LANES * (i + 1)], wz_ref[i])

    for i in range(ns):
        c0 = 2 * sw * i
        arb = jnp.broadcast_to(are_ref[:, i * sw:(i + 1) * sw], (nb, sw))
        aib = jnp.broadcast_to(aim_ref[:, i * sw:(i + 1) * sw], (nb, sw))

        def body(t, carry, c0=c0, arb=arb, aib=aib):
            xr, xi = carry
            r0 = pl.multiple_of(t * nb, nb)
            zr = zx_ref[pl.ds(r0, nb), c0:c0 + sw]
            zi = zx_ref[pl.ds(r0, nb), c0 + sw:c0 + 2 * sw]
            nxr = arb * xr - aib * xi + zr
            nxi = arb * xi + aib * xr + zi
            zx_ref[pl.ds(r0, nb), c0:c0 + sw] = nxr
            zx_ref[pl.ds(r0, nb), c0 + sw:c0 + 2 * sw] = nxi
            return nxr, nxi

        xr, xi = lax.fori_loop(0, tc, body,
                               (state_ref[:, c0:c0 + sw], state_ref[:, c0 + sw:c0 + 2 * sw]))
        state_ref[:, c0:c0 + sw] = xr
        state_ref[:, c0 + sw:c0 + 2 * sw] = xi

    ys = []
    for i in range(ns):
        c0 = 2 * sw * i
        xr = zx_ref[:, c0:c0 + sw].astype(BF16)
        xi = zx_ref[:, c0 + sw:c0 + 2 * sw].astype(BF16)
        ys.append(_dot(xr, cre_ref[i]) - _dot(xi, cim_ref[i]))
    y = jnp.concatenate(ys, axis=1) + d_ref[...] * ub
    y = jax.nn.gelu(y)
    gl = _dot(y.astype(BF16), wglu_ref[...]) + bglu_ref[...]
    y_ref[...] = (y * jax.nn.sigmoid(gl)).astype(BF16)

    @pl.when(step == pl.num_programs(0) - 1)
    def _fin():
        st_ref[...] = state_ref[...]


def _s5_layout(lp):
    g, p, h = lp['ssm_b_re'].shape
    gps = LANES // h
    ns = g // gps
    eye = jnp.eye(gps, dtype=jnp.bool_)

    def bd_b(b):
        bb = b.reshape(ns, gps, p, h).transpose(0, 1, 3, 2)
        out = jnp.where(eye[None, :, None, :, None], bb[:, :, :, None, :], 0.0)
        return out.reshape(ns, gps * h, gps * p)

    def bd_c(c):
        cc = c.reshape(ns, gps, h, p).transpose(0, 1, 3, 2)
        out = jnp.where(eye[None, :, None, :, None], cc[:, :, :, None, :], 0.0)
        return out.reshape(ns, gps * p, gps * h)

    return dict(
        lre=lp['ssm_lambda_re'].reshape(1, g * p),
        lim=lp['ssm_lambda_im'].reshape(1, g * p),
        ldt=jnp.repeat(lp['ssm_log_dt'], p).reshape(1, g * p),
        bre=bd_b(lp['ssm_b_re']), bim=bd_b(lp['ssm_b_im']),
        cre=bd_c(lp['ssm_c_re']).astype(BF16), cim=bd_c(lp['ssm_c_im']).astype(BF16),
        d=lp['ssm_d'].reshape(1, g * h),
        wglu=lp['ssm_w_glu'].astype(BF16), bglu=lp['ssm_b_glu'].reshape(1, -1),
        ns=ns, sw=gps * p)


def _state_to_cols(re, im, ns):
    b = re.shape[0]
    return jnp.stack([re.reshape(b, ns, -1), im.reshape(b, ns, -1)], axis=2).reshape(b, -1)


def _cols_to_state(st, ns, g, p):
    b = st.shape[0]
    s = st.reshape(b, ns, 2, -1)
    return s[:, :, 0].reshape(b, g, p), s[:, :, 1].reshape(b, g, p)


def _s5(u_tb, x0, sp, nb, t_len, tc):
    rows = nb * tc
    w = u_tb.shape[1]
    ns, sw = sp['ns'], sp['sw']
    ncol = ns * 2 * sw
    full = lambda a: pl.BlockSpec(a.shape, lambda i: (0,) * a.ndim)
    args = (u_tb, x0, sp['lre'], sp['lim'], sp['ldt'], sp['bre'], sp['bim'], sp['cre'], sp['cim'],
            sp['d'], sp['wglu'], sp['bglu'])
    return pl.pallas_call(
        functools.partial(_s5_kernel, nb=nb, tc=tc, ns=ns, sw=sw),
        grid=(t_len // tc,),
        in_specs=[pl.BlockSpec((rows, w), lambda i: (i, 0))] + [full(a) for a in args[1:]],
        out_specs=[pl.BlockSpec((rows, w), lambda i: (i, 0)),
                   pl.BlockSpec((nb, ncol), lambda i: (0, 0))],
        out_shape=[jax.ShapeDtypeStruct((t_len * nb, w), BF16),
                   jax.ShapeDtypeStruct((nb, ncol), F32)],
        scratch_shapes=[pltpu.VMEM((rows, ncol), F32),
                        pltpu.VMEM((ns, LANES, 2 * sw), BF16),
                        pltpu.VMEM((1, ns * sw), F32),
                        pltpu.VMEM((1, ns * sw), F32),
                        pltpu.VMEM((nb, ncol), F32)],
        compiler_params=_cparams(("arbitrary",)),
        name="s5",
    )(*args)


def _diff_lambda(lq1, lk1, lq2, lk2, lambda_init):
    return (jnp.exp(jnp.sum(lq1[...] * lk1[...], keepdims=True))
            - jnp.exp(jnp.sum(lq2[...] * lk2[...], keepdims=True)) + lambda_init)


def _attn_kernel(q_ref, k_ref, v_ref, lq1, lk1, lq2, lk2, g_ref, o_ref,
                 m_ref, l_ref, acc_ref, *, tq, hd, lambda_init):
    t_len = q_ref.shape[0]
    lam = _diff_lambda(lq1, lk1, lq2, lk2, lambda_init)
    lo = lax.broadcasted_iota(jnp.int32, (tq, LANES), 1) < hd
    r = lax.broadcasted_iota(jnp.int32, (2 * tq, tq), 0)
    c = lax.broadcasted_iota(jnp.int32, (2 * tq, tq), 1)
    causal = c <= jnp.where(r >= tq, r - tq, r)

    def block(kj, masked, qs):
        k0 = pl.multiple_of(kj * tq, tq)
        s = _dot_nt(qs, k_ref[pl.ds(k0, tq), :])
        if masked:
            s = jnp.where(causal, s, NEG_INF)
        m_prev = m_ref[...]
        m_new = jnp.maximum(m_prev, jnp.max(s, axis=1, keepdims=True))
        alpha = jnp.exp(m_prev - m_new)
        p = jnp.exp(s - m_new)
        l_ref[...] = alpha * l_ref[...] + jnp.sum(p, axis=1, keepdims=True)
        acc_ref[...] = alpha * acc_ref[...] + _dot(p.astype(BF16), v_ref[pl.ds(k0, tq), :])
        m_ref[...] = m_new

    for qi in range(t_len // tq):
        q = q_ref[qi * tq:(qi + 1) * tq, :]
        zero = jnp.zeros_like(q)
        qs = jnp.concatenate([jnp.where(lo, q, zero), jnp.where(lo, zero, q)], axis=0)
        m_ref[...] = jnp.full(m_ref.shape, NEG_INF, F32)
        l_ref[...] = jnp.zeros(l_ref.shape, F32)
        acc_ref[...] = jnp.zeros(acc_ref.shape, F32)
        if qi > 0:
            def body(j, carry, qs=qs):
                block(j, False, qs)
                return carry
            lax.fori_loop(0, qi, body, 0)
        block(qi, True, qs)
        o1 = acc_ref[:tq, :] / l_ref[:tq, :]
        o2 = acc_ref[tq:, :] / l_ref[tq:, :]
        o = o1 - lam * o2
        o_ref[qi * tq:(qi + 1) * tq, :] = (_rms(o, g_ref[...]) * (1.0 - lambda_init)).astype(BF16)


def _attn(qb, kb, vb, lam_params, g, n_batch, t_len, tq, hd, lambda_init):
    n, w = qb.shape
    nh = w // LANES
    blk = pl.BlockSpec((t_len, LANES), lambda b, h: (b, h))
    small = lambda a: pl.BlockSpec(a.shape, lambda b, h: (0, 0))
    return pl.pallas_call(
        functools.partial(_attn_kernel, tq=tq, hd=hd, lambda_init=lambda_init),
        grid=(n_batch, nh),
        in_specs=[blk, blk, blk] + [small(a) for a in lam_params] + [small(g)],
        out_specs=blk,
        out_shape=jax.ShapeDtypeStruct((n, w), BF16),
        scratch_shapes=[pltpu.VMEM((2 * tq, 1), F32),
                        pltpu.VMEM((2 * tq, 1), F32),
                        pltpu.VMEM((2 * tq, LANES), F32)],
        compiler_params=_cparams(("parallel", "parallel")),
        name="attn_prompt",
    )(qb, kb, vb, *lam_params, g)


def _sattn_kernel(pt_ref, q_ref, kown_ref, vown_ref, *rest, npg, page, dq, nh, lambda_init):
    k_refs = rest[:npg]
    v_refs = rest[npg:2 * npg]
    lq1, lk1, lq2, lk2, g_ref, o_ref, kb_ref, vb_ref, m_ref, l_ref, acc_ref = rest[2 * npg:]
    j = pl.program_id(1)
    q = q_ref[0]

    @pl.when(j == 0)
    def _init():
        m_ref[...] = jnp.full(m_ref.shape, NEG_INF, F32)
        l_ref[...] = jnp.zeros(l_ref.shape, F32)
        acc_ref[...] = jnp.zeros(acc_ref.shape, F32)

    def update(s, v):
        m_prev = m_ref[...]
        m_new = jnp.maximum(m_prev, jnp.max(s, axis=1, keepdims=True))
        alpha = jnp.exp(m_prev - m_new)
        p = jnp.exp(s - m_new)
        l_ref[...] = alpha * l_ref[...] + jnp.sum(p, axis=1, keepdims=True)
        acc_ref[...] = alpha * acc_ref[...] + _dot(p.astype(BF16), v)
        m_ref[...] = m_new

    for p_ in range(npg):
        kb_ref[p_ * page:(p_ + 1) * page, :] = k_refs[p_][0].astype(BF16)
        vb_ref[p_ * page:(p_ + 1) * page, :] = v_refs[p_][0].astype(BF16)
    update(_dot_nt(q, kb_ref[...]), vb_ref[...])

    @pl.when(j == pl.num_programs(1) - 1)
    def _fin():
        nq = q.shape[0]
        s = _dot_nt(q, kown_ref[0].astype(BF16))
        r = lax.broadcasted_iota(jnp.int32, (nq, dq), 0)
        c = lax.broadcasted_iota(jnp.int32, (nq, dq), 1)
        s = jnp.where(c <= r % dq, s, NEG_INF)
        update(s, vown_ref[0].astype(BF16))
        lam = _diff_lambda(lq1, lk1, lq2, lk2, lambda_init)
        o_all = acc_ref[...] / l_ref[...]
        half = nh * dq
        for h in range(nh):
            cs = slice(h * LANES, (h + 1) * LANES)
            o = o_all[h * dq:(h + 1) * dq, cs] - lam * o_all[half + h * dq:half + (h + 1) * dq, cs]
            o_ref[0, :, cs] = (_rms(o, g_ref[...]) * (1.0 - lambda_init)).astype(BF16)


def _sattn(page_table, qbd, k_own, v_own, cache_k, cache_v, layer, lam_params, g, npg, dq, nh,
           lambda_init):
    nb, nq, w = qbd.shape
    page = cache_k.shape[2]
    n_pages = page_table.shape[1]
    nsteps = n_pages // npg

    def page_spec(p_):
        return pl.BlockSpec((None, 1, page, w),
                            lambda b, j, pt: (layer, pt[b, j * npg + p_], 0, 0))

    per_b = lambda a: pl.BlockSpec((1,) + a.shape[1:], lambda b, j, pt: (b, 0, 0))
    small = lambda a: pl.BlockSpec(a.shape, lambda b, j, pt: (0, 0))
    grid_spec = pltpu.PrefetchScalarGridSpec(
        num_scalar_prefetch=1,
        grid=(nb, nsteps),
        in_specs=([per_b(qbd), per_b(k_own), per_b(v_own)]
                  + [page_spec(p_) for p_ in range(npg)]
                  + [page_spec(p_) for p_ in range(npg)]
                  + [small(a) for a in lam_params] + [small(g)]),
        out_specs=pl.BlockSpec((1, dq, w), lambda b, j, pt: (b, 0, 0)),
        scratch_shapes=[pltpu.VMEM((npg * page, w), BF16),
                        pltpu.VMEM((npg * page, w), BF16),
                        pltpu.VMEM((nq, 1), F32),
                        pltpu.VMEM((nq, 1), F32),
                        pltpu.VMEM((nq, w), F32)])
    return pl.pallas_call(
        functools.partial(_sattn_kernel, npg=npg, page=page, dq=dq, nh=nh,
                          lambda_init=lambda_init),
        grid_spec=grid_spec,
        out_shape=jax.ShapeDtypeStruct((nb, dq, w), BF16),
        compiler_params=_cparams(("parallel", "arbitrary")),
        name="attn_sample",
    )(page_table, qbd, k_own, v_own, *([cache_k] * npg), *([cache_v] * npg), *lam_params, g)


def _merge_kernel(x_ref, gates_ref, y_ref, o_ref, wbs_ref, wba_ref, wout_ref, gffn_ref,
                  wr_ref, br_ref, x1_ref, h2_ref, logit_ref):
    d = x_ref.shape[1]
    a = _dot(y_ref[...], wbs_ref[...])
    b = _dot(o_ref[...], wba_ref[...])
    merged = gates_ref[:, :d].astype(F32) * a + gates_ref[:, d:].astype(F32) * b
    x1 = x_ref[...] + _dot(merged.astype(BF16), wout_ref[...])
    x1_ref[...] = x1
    h2b = _rms(x1, gffn_ref[...]).astype(BF16)
    h2_ref[...] = h2b
    logit_ref[...] = _dot(h2b, wr_ref[...]) + br_ref[...]


def _merge(x2d, gates, y_tb, o, n_batch, t_len, tm, wbs, wba, wout, gffn, wr, br):
    n, d = x2d.shape
    w = o.shape[1]
    nt = t_len // tm
    row = lambda b, t: (b * nt + t, 0)
    const = lambda b, t: (0, 0)
    ne = wr.shape[1]
    return pl.pallas_call(
        _merge_kernel,
        grid=(n_batch, nt),
        in_specs=[pl.BlockSpec((tm, d), row),
                  pl.BlockSpec((tm, 2 * d), row),
                  pl.BlockSpec((tm, w), lambda b, t: (t, b)),
                  pl.BlockSpec((tm, w), row),
                  pl.BlockSpec(wbs.shape, const),
                  pl.BlockSpec(wba.shape, const),
                  pl.BlockSpec(wout.shape, const),
                  pl.BlockSpec((1, d), const),
                  pl.BlockSpec(wr.shape, const),
                  pl.BlockSpec((1, ne), const)],
        out_specs=[pl.BlockSpec((tm, d), row),
                   pl.BlockSpec((tm, d), row),
                   pl.BlockSpec((tm, ne), row)],
        out_shape=[jax.ShapeDtypeStruct((n, d), F32),
                   jax.ShapeDtypeStruct((n, d), BF16),
                   jax.ShapeDtypeStruct((n, ne), F32)],
        compiler_params=_cparams(("parallel", "parallel")),
        name="merge",
    )(x2d, gates, y_tb, o, wbs, wba, wout, gffn, wr, br)


def _moe_kernel(be_ref, nu_ref, x_ref, wg_ref, wu_ref, bg_ref, bu_ref, wd_ref, bd_ref, o_ref):
    i = pl.program_id(0)

    @pl.when(i < nu_ref[0])
    def _compute():
        x = x_ref[...]
        g_lin = jnp.minimum(_dot(x, wg_ref[0]) + bg_ref[0], SWIGLU_LIMIT)
        up = jnp.clip(_dot(x, wu_ref[0]) + bu_ref[0], -SWIGLU_LIMIT, SWIGLU_LIMIT)
        act = (up + 1.0) * (g_lin * jax.nn.sigmoid(SWIGLU_ALPHA * g_lin))
        o_ref[...] = _dot(act.astype(BF16), wd_ref[0]) + bd_ref[0]

    @pl.when(i >= nu_ref[0])
    def _skip():
        o_ref[...] = jnp.zeros(o_ref.shape, o_ref.dtype)


def _moe_gemm(xs, blk_expert, n_used, wg, wu, bg, bu, wd, bd):
    n_rows, d = xs.shape
    bm = MOE_BLOCK_ROWS
    de = wg.shape[2]
    wspec = lambda a: pl.BlockSpec((1,) + a.shape[1:], lambda i, be, nu: (be[i], 0, 0))
    grid_spec = pltpu.PrefetchScalarGridSpec(
        num_scalar_prefetch=2,
        grid=(n_rows // bm,),
        in_specs=[pl.BlockSpec((bm, d), lambda i, be, nu: (i, 0)),
                  wspec(wg), wspec(wu), wspec(bg), wspec(bu), wspec(wd), wspec(bd)],
        out_specs=pl.BlockSpec((bm, d), lambda i, be, nu: (i, 0)))
    return pl.pallas_call(
        _moe_kernel,
        grid_spec=grid_spec,
        out_shape=jax.ShapeDtypeStruct((n_rows, d), F32),
        compiler_params=_cparams(("arbitrary",)),
        name="moe_gemm",
    )(blk_expert, n_used, xs, wg, wu, bg, bu, wd, bd)


def _combine_kernel(x_ref, yg_ref, gate_ref, g_ref, o_ref, *, final_norm):
    d = x_ref.shape[1]
    acc = x_ref[...]
    gate = gate_ref[...]
    for k in range(TOP_K):
        acc = acc + gate[:, k:k + 1] * yg_ref[:, k * d:(k + 1) * d]
    o_ref[...] = _rms(acc, g_ref[...]) if final_norm else acc


def _combine(x1, yg, gate, g, tm, final_norm):
    n, d = x1.shape
    return pl.pallas_call(
        functools.partial(_combine_kernel, final_norm=final_norm),
        grid=(n // tm,),
        in_specs=[pl.BlockSpec((tm, d), lambda i: (i, 0)),
                  pl.BlockSpec((tm, TOP_K * d), lambda i: (i, 0)),
                  pl.BlockSpec((tm, TOP_K), lambda i: (i, 0)),
                  pl.BlockSpec((1, d), lambda i: (0, 0))],
        out_specs=pl.BlockSpec((tm, d), lambda i: (i, 0)),
        out_shape=jax.ShapeDtypeStruct((n, d), F32),
        compiler_params=_cparams(("parallel",)),
        name="combine",
    )(x1, yg, gate, g)


def _moe(x1, h2b, logits, mp, g_final, final_norm):
    n, d = x1.shape
    ne = mp['ne']
    bm = MOE_BLOCK_ROWS
    top_val, top_idx = lax.top_k(logits[:, :ne], TOP_K)
    gate = jax.nn.softmax(top_val, axis=-1)
    m = n * TOP_K
    e_flat = top_idx.reshape(m).astype(jnp.int32)
    onehot = (e_flat[:, None] == jnp.arange(ne, dtype=jnp.int32)[None, :]).astype(jnp.int32)
    csum = jnp.cumsum(onehot, axis=0)
    rank = jnp.sum((csum - 1) * onehot, axis=1)
    counts = csum[-1]
    padded = (counts + bm - 1) // bm * bm
    pend = jnp.cumsum(padded)
    pstart = pend - padded
    dest = pstart[e_flat] + rank
    n_rows = (m + bm - 1) // bm * bm + ne * bm
    t_flat = jnp.repeat(jnp.arange(n, dtype=jnp.int32), TOP_K)
    row_tok = jnp.zeros((n_rows,), jnp.int32).at[dest].set(t_flat)
    blk_start = jnp.arange(n_rows // bm, dtype=jnp.int32) * bm
    blk_expert = jnp.minimum(jnp.searchsorted(pend, blk_start, side='right'),
                             ne - 1).astype(jnp.int32)
    n_used = (pend[-1] // bm).astype(jnp.int32).reshape(1)
    xs = h2b[row_tok]
    ys = _moe_gemm(xs, blk_expert, n_used, mp['wg'], mp['wu'], mp['bg'], mp['bu'],
                   mp['wd'], mp['bd'])
    yg = ys[dest].reshape(n, TOP_K * d)
    tm = min(256, n)
    return _combine(x1, yg, gate, g_final, tm, final_norm)


def _rope_tables(pos, hd):
    half = hd // 2
    inv_freq = ROPE_THETA ** (-jnp.arange(half, dtype=F32) / half)
    ang = pos.astype(F32)[:, None] * inv_freq[None, :]
    cos = jnp.cos(ang)
    sin = jnp.sin(ang)
    reps = LANES // hd
    cos_t = jnp.tile(jnp.concatenate([cos, cos], axis=1), (1, reps))
    sin_t = jnp.tile(jnp.concatenate([-sin, sin], axis=1), (1, reps))
    return cos_t, sin_t


def _layer_params(lp):
    d = lp['w_in'].shape[0]
    ne = lp['w_router'].shape[1]
    wr = jnp.zeros((d, LANES), F32).at[:, :ne].set(lp['w_router']).astype(BF16)
    br = jnp.zeros((1, LANES), F32).at[0, :ne].set(lp['b_router'])
    return dict(
        g_mix=lp['norm_mix_g'].reshape(1, d),
        win=lp['w_in'].astype(BF16),
        wgate=lp['w_gate'].astype(BF16),
        bgate=lp['b_gate'].reshape(1, -1),
        s5=_s5_layout(lp),
        lam=[lp[k].reshape(1, -1) for k in ('lambda_q1', 'lambda_k1', 'lambda_q2', 'lambda_k2')],
        g_sub=lp['attn_subln_g'].reshape(1, -1),
        wbs=lp['w_branch_ssm'].astype(BF16),
        wba=lp['w_branch_attn'].astype(BF16),
        wout=lp['w_out'].astype(BF16),
        g_ffn=lp['norm_ffn_g'].reshape(1, d),
        wr=wr, br=br,
        moe=dict(ne=ne,
                 wg=lp['w_gate_up'][:, :, 0::2].astype(BF16),
                 wu=lp['w_gate_up'][:, :, 1::2].astype(BF16),
                 bg=lp['b_gate_up'][:, None, 0::2],
                 bu=lp['b_gate_up'][:, None, 1::2],
                 wd=lp['w_down'].astype(BF16),
                 bd=lp['b_down'][:, None, :]))


def _block_diag_queries(qb, bsz, dq, nh, hd):
    qt = qb.reshape(bsz, dq, nh, 2, hd).transpose(0, 3, 2, 1, 4)
    eh = jnp.eye(nh, dtype=jnp.bool_)[None, None, :, None, :, None, None]
    ec = jnp.eye(2, dtype=jnp.bool_)[None, :, None, None, None, :, None]
    out = jnp.where(eh & ec, qt[:, :, :, :, None, None, :], jnp.zeros((), qb.dtype))
    return out.reshape(bsz, 2 * nh * dq, nh * 2 * hd)


def _layer(x, pos, x0_re, x0_im, past, P, lambda_init, g_final, final_norm):
    bsz, t_len, d = x.shape
    n = bsz * t_len
    x2d = x.reshape(n, d)
    sp = P['s5']
    g, p = x0_re.shape[1], x0_re.shape[2]
    w = P['win'].shape[1] // 4
    nh = w // LANES
    hd = LANES // 2
    cos_t, sin_t = _rope_tables(pos, hd)
    x0 = _state_to_cols(x0_re.astype(F32), x0_im.astype(F32), sp['ns'])
    if past is None:
        n_batch, rows_t, tm = bsz, t_len, min(256, t_len)
    else:
        n_batch, rows_t, tm = 1, n, n
        cos_t = jnp.tile(cos_t, (bsz, 1))
        sin_t = jnp.tile(sin_t, (bsz, 1))
    u, qb, k, v, kb, vb, gates = _proj(x2d, n_batch, rows_t, tm, P['g_mix'], P['win'], P['wgate'],
                                       P['bgate'], cos_t, sin_t, hd // 2, hd ** -0.5)
    if past is None:
        u_tb = u.reshape(t_len * bsz, w)
        y_tb, st = _s5(u_tb, x0, sp, bsz, t_len, min(64, t_len))
        y_in = y_tb.reshape(t_len, bsz * w)
        o = _attn(qb, kb, vb, P['lam'], P['g_sub'], bsz, t_len, min(256, t_len), hd, lambda_init)
    else:
        cache_k, cache_v, page_table, layer = past
        u_tb = u.reshape(bsz, t_len, w).transpose(1, 0, 2).reshape(t_len * bsz, w)
        y_tb, st = _s5(u_tb, x0, sp, bsz, t_len, t_len)
        y_in = y_tb.reshape(t_len, bsz, w).transpose(1, 0, 2).reshape(n, w)
        npg = math.gcd(8, page_table.shape[1])
        cshape = cache_k.shape[:3] + (w,)
        o = _sattn(page_table, _block_diag_queries(qb, bsz, t_len, nh, hd),
                   k.reshape(bsz, t_len, w), v.reshape(bsz, t_len, w),
                   cache_k.reshape(cshape), cache_v.reshape(cshape), layer,
                   P['lam'], P['g_sub'], npg, t_len, nh, lambda_init).reshape(n, w)
    x1, h2b, logits = _merge(x2d, gates, y_in, o, n_batch, rows_t, tm,
                             P['wbs'], P['wba'], P['wout'], P['g_ffn'], P['wr'], P['br'])
    out = _moe(x1, h2b, logits, P['moe'], g_final, final_norm)
    st_re, st_im = _cols_to_state(st, sp['ns'], g, p)
    return (out.reshape(bsz, t_len, d), k.reshape(bsz, t_len, nh, LANES),
            v.reshape(bsz, t_len, nh, LANES), st_re.astype(x.dtype), st_im.astype(x.dtype))


def kernel(x_prompt, x_sample, cache_k, cache_v, state_ssm_re, state_ssm_im, page_table, norm_mix_g, w_in, ssm_lambda_re, ssm_lambda_im, ssm_log_dt, ssm_b_re, ssm_b_im, ssm_c_re, ssm_c_im, ssm_d, ssm_w_glu, ssm_b_glu, lambda_q1, lambda_k1, lambda_q2, lambda_k2, attn_subln_g, w_branch_ssm, w_branch_attn, w_gate, b_gate, w_out, norm_ffn_g, w_router, b_router, w_gate_up, b_gate_up, w_down, b_down, norm_final_g):
    depth = w_in.shape[0]
    past_len = page_table.shape[1] * cache_k.shape[2]
    pos_prompt = jnp.arange(x_prompt.shape[1], dtype=jnp.int32)
    pos_sample = past_len + jnp.arange(x_sample.shape[1], dtype=jnp.int32)
    g, p = state_ssm_re.shape[2], state_ssm_re.shape[3]
    zero_state = jnp.zeros((x_prompt.shape[0], g, p), F32)
    g_final = norm_final_g.reshape(1, -1)
    names = ('norm_mix_g', 'w_in', 'ssm_lambda_re', 'ssm_lambda_im', 'ssm_log_dt', 'ssm_b_re',
             'ssm_b_im', 'ssm_c_re', 'ssm_c_im', 'ssm_d', 'ssm_w_glu', 'ssm_b_glu', 'lambda_q1',
             'lambda_k1', 'lambda_q2', 'lambda_k2', 'attn_subln_g', 'w_branch_ssm',
             'w_branch_attn', 'w_gate', 'b_gate', 'w_out', 'norm_ffn_g', 'w_router', 'b_router',
             'w_gate_up', 'b_gate_up', 'w_down', 'b_down')
    stacked = (norm_mix_g, w_in, ssm_lambda_re, ssm_lambda_im, ssm_log_dt, ssm_b_re, ssm_b_im,
               ssm_c_re, ssm_c_im, ssm_d, ssm_w_glu, ssm_b_glu, lambda_q1, lambda_k1, lambda_q2,
               lambda_k2, attn_subln_g, w_branch_ssm, w_branch_attn, w_gate, b_gate, w_out,
               norm_ffn_g, w_router, b_router, w_gate_up, b_gate_up, w_down, b_down)
    xp, xs = x_prompt, x_sample
    outs_p, outs_s = [], []
    for l in range(depth):
        P = _layer_params({nm: a[l] for nm, a in zip(names, stacked)})
        lambda_init = 0.8 - 0.6 * math.exp(-0.3 * l)
        last = l == depth - 1
        xp, kp, vp, srp, sip = _layer(xp, pos_prompt, zero_state, zero_state, None, P,
                                      lambda_init, g_final, last)
        xs, ks, vs, srs, sis = _layer(xs, pos_sample, state_ssm_re[l], state_ssm_im[l],
                                      (cache_k, cache_v, page_table, l), P,
                                      lambda_init, g_final, last)
        outs_p.append((kp, vp, srp, sip))
        outs_s.append((ks, vs, srs, sis))
    stack = lambda outs, i: jnp.stack([o[i] for o in outs])
    return (xp, xs, stack(outs_p, 0), stack(outs_p, 1), stack(outs_p, 2), stack(outs_p, 3),
            stack(outs_s, 0), stack(outs_s, 1), stack(outs_s, 2), stack(outs_s, 3))
```

```python
import functools
import math

import jax
import jax.numpy as jnp
from jax import lax
from jax.experimental import pallas as pl
from jax.experimental.pallas import tpu as pltpu

F32 = jnp.float32
BF16 = jnp.bfloat16

NORM_EPS = 1e-6
ROPE_THETA = 10000.0
NEG_INF = -1e30
TOP_K = 4
SWIGLU_LIMIT = 7.0
SWIGLU_ALPHA = 1.702
LANES = 128
MOE_BLOCK_ROWS = 256
PAIR_COLS = 256
ATTN_KEY_CHUNK = 1024
VMEM_LIMIT = 56 * 1024 * 1024


def _cparams(sem):
    return pltpu.CompilerParams(dimension_semantics=sem, vmem_limit_bytes=VMEM_LIMIT)


def _dot(a, b):
    return jnp.dot(a, b, preferred_element_type=F32)


def _dot_nt(a, b):
    return lax.dot_general(a, b, (((1,), (1,)), ((), ())), preferred_element_type=F32)


def _rms(x, g):
    return x * lax.rsqrt(jnp.mean(x * x, axis=-1, keepdims=True) + NORM_EPS) * g


def _proj_kernel(x_ref, g_ref, win_ref, wgate_ref, bgate_ref, cos_ref, sin_ref,
                 u_ref, q_ref, k_ref, v_ref, kb_ref, vb_ref, gates_ref, *, half, scale):
    hb = _rms(x_ref[...], g_ref[...]).astype(BF16)
    proj = _dot(hb, win_ref[...])
    w = u_ref.shape[-1]
    u_ref[...] = proj[:, :w]
    cos = cos_ref[...]
    sin = sin_ref[...]
    lane = lax.broadcasted_iota(jnp.int32, cos.shape, 1)
    lo = (lane % (2 * half)) < half

    def rot(xh):
        fwd = pltpu.roll(xh, LANES - half, axis=1)
        bwd = pltpu.roll(xh, half, axis=1)
        return xh * cos + jnp.where(lo, fwd, bwd) * sin

    for hh in range(w // LANES):
        sl = slice(hh * LANES, (hh + 1) * LANES)
        qh = rot(proj[:, w + hh * LANES:w + (hh + 1) * LANES])
        q_ref[:, sl] = (qh * scale).astype(BF16)
        kh = rot(proj[:, 2 * w + hh * LANES:2 * w + (hh + 1) * LANES])
        k_ref[:, sl] = kh
        kb_ref[:, sl] = kh.astype(BF16)
    v = proj[:, 3 * w:]
    v_ref[...] = v
    vb_ref[...] = v.astype(BF16)
    gl = _dot(hb, wgate_ref[...]) + bgate_ref[...]
    gates_ref[...] = jax.nn.sigmoid(gl).astype(BF16)


def _proj(x2d, n_batch, t_len, tm, g, win_b, wgate_b, bgate, cos_t, sin_t, half, scale):
    n, d = x2d.shape
    w = win_b.shape[1] // 4
    nt = t_len // tm
    row = lambda b, t: (b * nt + t, 0)
    const = lambda b, t: (0, 0)
    outs = pl.pallas_call(
        functools.partial(_proj_kernel, half=half, scale=scale),
        grid=(n_batch, nt),
        in_specs=[pl.BlockSpec((tm, d), row),
                  pl.BlockSpec((1, d), const),
                  pl.BlockSpec(win_b.shape, const),
                  pl.BlockSpec(wgate_b.shape, const),
                  pl.BlockSpec((1, wgate_b.shape[1]), const),
                  pl.BlockSpec((tm, LANES), lambda b, t: (t, 0)),
                  pl.BlockSpec((tm, LANES), lambda b, t: (t, 0))],
        out_specs=[pl.BlockSpec((tm, w), lambda b, t: (t, b)),
                   pl.BlockSpec((tm, w), row),
                   pl.BlockSpec((tm, w), row),
                   pl.BlockSpec((tm, w), row),
                   pl.BlockSpec((tm, w), row),
                   pl.BlockSpec((tm, w), row),
                   pl.BlockSpec((tm, 2 * d), row)],
        out_shape=[jax.ShapeDtypeStruct((t_len, n_batch * w), F32),
                   jax.ShapeDtypeStruct((n, w), BF16),
                   jax.ShapeDtypeStruct((n, w), F32),
                   jax.ShapeDtypeStruct((n, w), F32),
                   jax.ShapeDtypeStruct((n, w), BF16),
                   jax.ShapeDtypeStruct((n, w), BF16),
                   jax.ShapeDtypeStruct((n, 2 * d), BF16)],
        compiler_params=_cparams(("parallel", "parallel")),
        name="proj",
    )(x2d, g, win_b, wgate_b, bgate, cos_t, sin_t)
    return outs


def _s5_kernel(u_ref, x0_ref, lre_ref, lim_ref, ldt_ref, bre_ref, bim_ref, cre_ref, cim_ref,
               d_ref, wglu_ref, bglu_ref, y_ref, st_ref,
               zx_ref, wz_ref, are_ref, aim_ref, state_ref, *, nb, tc, ns, sw):
    step = pl.program_id(0)

    @pl.when(step == 0)
    def _init():
        lre = lre_ref[...]
        lim = lim_ref[...]
        dt = jnp.exp(ldt_ref[...])
        mag = jnp.exp(lre * dt)
        are = mag * jnp.cos(lim * dt)
        aim = mag * jnp.sin(lim * dt)
        den = lre * lre + lim * lim
        fre = ((are - 1.0) * lre + aim * lim) / den
        fim = (aim * lre - (are - 1.0) * lim) / den
        are_ref[...] = are
        aim_ref[...] = aim
        for i in range(ns):
            fr = fre[:, i * sw:(i + 1) * sw]
            fi = fim[:, i * sw:(i + 1) * sw]
            br = bre_ref[i]
            bi = bim_ref[i]
            wz_ref[i, :, :sw] = (fr * br - fi * bi).astype(BF16)
            wz_ref[i, :, sw:] = (fr * bi + fi * br).astype(BF16)
        state_ref[...] = x0_ref[...]

    ub = u_ref[...]
    ubb = ub.astype(BF16)
    for i in range(ns):
        zx_ref[:, 2 * sw * i:2 * sw * (i + 1)] = _dot(ubb[:, LANES * i:LANES * (i + 1)], wz_ref[i])

    for i in range(ns):
        c0 = 2 * sw * i
        arb = jnp.broadcast_to(are_ref[:, i * sw:(i + 1) * sw], (nb, sw))
        aib = jnp.broadcast_to(aim_ref[:, i * sw:(i + 1) * sw], (nb, sw))

        def body(t, carry, c0=c0, arb=arb, aib=aib):
            xr, xi = carry
            r0 = pl.multiple_of(t * nb, nb)
            zr = zx_ref[pl.ds(r0, nb), c0:c0 + sw]
            zi = zx_ref[pl.ds(r0, nb), c0 + sw:c0 + 2 * sw]
            nxr = arb * xr - aib * xi + zr
            nxi = arb * xi + aib * xr + zi
            zx_ref[pl.ds(r0, nb), c0:c0 + sw] = nxr
            zx_ref[pl.ds(r0, nb), c0 + sw:c0 + 2 * sw] = nxi
            return nxr, nxi

        xr, xi = lax.fori_loop(0, tc, body,
                               (state_ref[:, c0:c0 + sw], state_ref[:, c0 + sw:c0 + 2 * sw]))
        state_ref[:, c0:c0 + sw] = xr
        state_ref[:, c0 + sw:c0 + 2 * sw] = xi

    ys = []
    for i in range(ns):
        c0 = 2 * sw * i
        xr = zx_ref[:, c0:c0 + sw].astype(BF16)
        xi = zx_ref[:, c0 + sw:c0 + 2 * sw].astype(BF16)
        ys.append(_dot(xr, cre_ref[i]) - _dot(xi, cim_ref[i]))
    y = jnp.concatenate(ys, axis=1) + d_ref[...] * ub
    y = jax.nn.gelu(y)
    gl = _dot(y.astype(BF16), wglu_ref[...]) + bglu_ref[...]
    y_ref[...] = (y * jax.nn.sigmoid(gl)).astype(BF16)

    @pl.when(step == pl.num_programs(0) - 1)
    def _fin():
        st_ref[...] = state_ref[...]


def _s5_layout(lp):
    g, p, h = lp['ssm_b_re'].shape
    gps = LANES // h
    ns = g // gps
    eye = jnp.eye(gps, dtype=jnp.bool_)

    def bd_b(b):
        bb = b.reshape(ns, gps, p, h).transpose(0, 1, 3, 2)
        out = jnp.where(eye[None, :, None, :, None], bb[:, :, :, None, :], 0.0)
        return out.reshape(ns, gps * h, gps * p)

    def bd_c(c):
        cc = c.reshape(ns, gps, h, p).transpose(0, 1, 3, 2)
        out = jnp.where(eye[None, :, None, :, None], cc[:, :, :, None, :], 0.0)
        return out.reshape(ns, gps * p, gps * h)

    return dict(
        lre=lp['ssm_lambda_re'].reshape(1, g * p),
        lim=lp['ssm_lambda_im'].reshape(1, g * p),
        ldt=jnp.repeat(lp['ssm_log_dt'], p).reshape(1, g * p),
        bre=bd_b(lp['ssm_b_re']), bim=bd_b(lp['ssm_b_im']),
        cre=bd_c(lp['ssm_c_re']).astype(BF16), cim=bd_c(lp['ssm_c_im']).astype(BF16),
        d=lp['ssm_d'].reshape(1, g * h),
        wglu=lp['ssm_w_glu'].astype(BF16), bglu=lp['ssm_b_glu'].reshape(1, -1),
        ns=ns, sw=gps * p)


def _state_to_cols(re, im, ns):
    b = re.shape[0]
    return jnp.stack([re.reshape(b, ns, -1), im.reshape(b, ns, -1)], axis=2).reshape(b, -1)


def _cols_to_state(st, ns, g, p):
    b = st.shape[0]
    s = st.reshape(b, ns, 2, -1)
    return s[:, :, 0].reshape(b, g, p), s[:, :, 1].reshape(b, g, p)


def _s5(u_tb, x0, sp, nb, t_len, tc):
    rows = nb * tc
    w = u_tb.shape[1]
    ns, sw = sp['ns'], sp['sw']
    ncol = ns * 2 * sw
    full = lambda a: pl.BlockSpec(a.shape, lambda i: (0,) * a.ndim)
    args = (u_tb, x0, sp['lre'], sp['lim'], sp['ldt'], sp['bre'], sp['bim'], sp['cre'], sp['cim'],
            sp['d'], sp['wglu'], sp['bglu'])
    return pl.pallas_call(
        functools.partial(_s5_kernel, nb=nb, tc=tc, ns=ns, sw=sw),
        grid=(t_len // tc,),
        in_specs=[pl.BlockSpec((rows, w), lambda i: (i, 0))] + [full(a) for a in args[1:]],
        out_specs=[pl.BlockSpec((rows, w), lambda i: (i, 0)),
                   pl.BlockSpec((nb, ncol), lambda i: (0, 0))],
        out_shape=[jax.ShapeDtypeStruct((t_len * nb, w), BF16),
                   jax.ShapeDtypeStruct((nb, ncol), F32)],
        scratch_shapes=[pltpu.VMEM((rows, ncol), F32),
                        pltpu.VMEM((ns, LANES, 2 * sw), BF16),
                        pltpu.VMEM((1, ns * sw), F32),
                        pltpu.VMEM((1, ns * sw), F32),
                        pltpu.VMEM((nb, ncol), F32)],
        compiler_params=_cparams(("arbitrary",)),
        name="s5",
    )(*args)


def _diff_lambda(lq1, lk1, lq2, lk2, lambda_init):
    return (jnp.exp(jnp.sum(lq1[...] * lk1[...], keepdims=True))
            - jnp.exp(jnp.sum(lq2[...] * lk2[...], keepdims=True)) + lambda_init)


def _attn_kernel(q_ref, k_ref, v_ref, lq1, lk1, lq2, lk2, g_ref, o_ref, *, tq, hd, lambda_init):
    t_len = q_ref.shape[0]
    lam = _diff_lambda(lq1, lk1, lq2, lk2, lambda_init)
    lo = lax.broadcasted_iota(jnp.int32, (tq, LANES), 1) < hd
    max_blocks = max(ATTN_KEY_CHUNK // tq, 1)

    for qi in range(t_len // tq):
        q = q_ref[qi * tq:(qi + 1) * tq, :]
        zero = jnp.zeros_like(q)
        qs = jnp.concatenate([jnp.where(lo, q, zero), jnp.where(lo, zero, q)], axis=0)
        m = jnp.full((1, 2 * tq), NEG_INF, F32)
        l = jnp.zeros((1, 2 * tq), F32)
        acc = jnp.zeros((LANES, 2 * tq), F32)
        n_blocks = qi + 1
        n_chunks = -(-n_blocks // max_blocks)
        k0 = 0
        for ci in range(n_chunks):
            ksz = (n_blocks // n_chunks + (1 if ci < n_blocks % n_chunks else 0)) * tq
            s = _dot_nt(k_ref[k0:k0 + ksz, :], qs)
            if ci == n_chunks - 1:
                kpos = k0 + lax.broadcasted_iota(jnp.int32, (ksz, 2 * tq), 0)
                qc = lax.broadcasted_iota(jnp.int32, (ksz, 2 * tq), 1)
                s = jnp.where(kpos <= qi * tq + jnp.where(qc >= tq, qc - tq, qc), s, NEG_INF)
            m_new = jnp.maximum(m, jnp.max(s, axis=0, keepdims=True))
            alpha = jnp.exp(m - m_new)
            p = jnp.exp(s - m_new)
            l = alpha * l + jnp.sum(p, axis=0, keepdims=True)
            pv = lax.dot_general(v_ref[k0:k0 + ksz, :], p.astype(BF16), (((0,), (0,)), ((), ())),
                                 preferred_element_type=F32)
            acc = alpha * acc + pv
            m = m_new
            k0 += ksz
        o_t = (acc[:, :tq] / l[:, :tq]) - lam * (acc[:, tq:] / l[:, tq:])
        ms = jnp.mean(o_t * o_t, axis=0, keepdims=True)
        o_t = o_t * lax.rsqrt(ms + NORM_EPS) * g_ref[...] * (1.0 - lambda_init)
        o_ref[qi * tq:(qi + 1) * tq, :] = o_t.T.astype(BF16)


def _attn(qb, kb, vb, lam_params, g, n_batch, t_len, tq, hd, lambda_init):
    n, w = qb.shape
    nh = w // LANES
    blk = pl.BlockSpec((t_len, LANES), lambda b, h: (b, h))
    small = lambda a: pl.BlockSpec(a.shape, lambda b, h: (0, 0))
    g_cols = jnp.broadcast_to(g.reshape(LANES, 1), (LANES, tq))
    return pl.pallas_call(
        functools.partial(_attn_kernel, tq=tq, hd=hd, lambda_init=lambda_init),
        grid=(n_batch, nh),
        in_specs=[blk, blk, blk] + [small(a) for a in lam_params] + [small(g_cols)],
        out_specs=blk,
        out_shape=jax.ShapeDtypeStruct((n, w), BF16),
        compiler_params=_cparams(("parallel", "parallel")),
        name="attn_prompt",
    )(qb, kb, vb, *lam_params, g_cols)


def _sattn_kernel(pt_ref, q_ref, kown_ref, vown_ref, *rest, npg, prow, dq, nh, lambda_init):
    k_refs = rest[:npg]
    v_refs = rest[npg:2 * npg]
    lq1, lk1, lq2, lk2, g_ref, o_ref, kb_ref, vb_ref, m_ref, l_ref, acc_ref = rest[2 * npg:]
    j = pl.program_id(1)
    q = q_ref[0]
    nq = q.shape[0]

    def own_head(ncols):
        r = lax.broadcasted_iota(jnp.int32, (nq, ncols), 0)
        c = lax.broadcasted_iota(jnp.int32, (nq, ncols), 1)
        return r, c, (c % nh) == ((r // dq) % nh)

    @pl.when(j == 0)
    def _init():
        m_ref[...] = jnp.full(m_ref.shape, NEG_INF, F32)
        l_ref[...] = jnp.zeros(l_ref.shape, F32)
        acc_ref[...] = jnp.zeros(acc_ref.shape, F32)

    def update(s, v):
        m_prev = m_ref[...]
        m_new = jnp.maximum(m_prev, jnp.max(s, axis=1, keepdims=True))
        alpha = jnp.exp(m_prev - m_new)
        p = jnp.exp(s - m_new)
        l_ref[...] = alpha * l_ref[...] + jnp.sum(p, axis=1, keepdims=True)
        acc_ref[...] = alpha * acc_ref[...] + _dot(p.astype(BF16), v)
        m_ref[...] = m_new

    for p_ in range(npg):
        kb_ref[p_ * prow:(p_ + 1) * prow, :] = k_refs[p_][0].astype(BF16)
        vb_ref[p_ * prow:(p_ + 1) * prow, :] = v_refs[p_][0].astype(BF16)
    _, _, valid = own_head(npg * prow)
    update(jnp.where(valid, _dot_nt(q, kb_ref[...]), NEG_INF), vb_ref[...])

    @pl.when(j == pl.num_programs(1) - 1)
    def _fin():
        r, c, valid = own_head(dq * nh)
        valid = jnp.logical_and(valid, (c // nh) <= (r % dq))
        s = _dot_nt(q, kown_ref[0].astype(BF16))
        update(jnp.where(valid, s, NEG_INF), vown_ref[0].astype(BF16))
        lam = _diff_lambda(lq1, lk1, lq2, lk2, lambda_init)
        o_all = acc_ref[...] / l_ref[...]
        o = o_all[:nq // 2] - lam * o_all[nq // 2:]
        o_ref[0] = (_rms(o, g_ref[...]) * (1.0 - lambda_init)).astype(BF16)


def _sattn(page_table, q_rows, k_own, v_own, cache_k, cache_v, layer, lam_params, g, npg, dq, nh,
           lambda_init):
    nb, nq, _ = q_rows.shape
    prow = cache_k.shape[2]
    n_pages = page_table.shape[1]
    nsteps = n_pages // npg

    def page_spec(p_):
        return pl.BlockSpec((None, 1, prow, LANES),
                            lambda b, j, pt: (layer, pt[b, j * npg + p_], 0, 0))

    per_b = lambda a: pl.BlockSpec((1,) + a.shape[1:], lambda b, j, pt: (b, 0, 0))
    small = lambda a: pl.BlockSpec(a.shape, lambda b, j, pt: (0, 0))
    grid_spec = pltpu.PrefetchScalarGridSpec(
        num_scalar_prefetch=1,
        grid=(nb, nsteps),
        in_specs=([per_b(q_rows), per_b(k_own), per_b(v_own)]
                  + [page_spec(p_) for p_ in range(npg)]
                  + [page_spec(p_) for p_ in range(npg)]
                  + [small(a) for a in lam_params] + [small(g)]),
        out_specs=pl.BlockSpec((1, nq // 2, LANES), lambda b, j, pt: (b, 0, 0)),
        scratch_shapes=[pltpu.VMEM((npg * prow, LANES), BF16),
                        pltpu.VMEM((npg * prow, LANES), BF16),
                        pltpu.VMEM((nq, 1), F32),
                        pltpu.VMEM((nq, 1), F32),
                        pltpu.VMEM((nq, LANES), F32)])
    return pl.pallas_call(
        functools.partial(_sattn_kernel, npg=npg, prow=prow, dq=dq, nh=nh,
                          lambda_init=lambda_init),
        grid_spec=grid_spec,
        out_shape=jax.ShapeDtypeStruct((nb, nq // 2, LANES), BF16),
        compiler_params=_cparams(("parallel", "arbitrary")),
        name="attn_sample",
    )(page_table, q_rows, k_own, v_own, *([cache_k] * npg), *([cache_v] * npg), *lam_params, g)


def _merge_kernel(x_ref, gates_ref, y_ref, o_ref, wbs_ref, wba_ref, wout_ref, gffn_ref,
                  wr_ref, br_ref, x1_ref, h2_ref, logit_ref):
    d = x_ref.shape[1]
    a = _dot(y_ref[...], wbs_ref[...])
    b = _dot(o_ref[...], wba_ref[...])
    merged = gates_ref[:, :d].astype(F32) * a + gates_ref[:, d:].astype(F32) * b
    x1 = x_ref[...] + _dot(merged.astype(BF16), wout_ref[...])
    x1_ref[...] = x1
    h2b = _rms(x1, gffn_ref[...]).astype(BF16)
    h2_ref[...] = h2b
    logit_ref[...] = _dot(h2b, wr_ref[...]) + br_ref[...]


def _merge(x2d, gates, y_tb, o, n_batch, t_len, tm, wbs, wba, wout, gffn, wr, br):
    n, d = x2d.shape
    w = o.shape[1]
    nt = t_len // tm
    row = lambda b, t: (b * nt + t, 0)
    const = lambda b, t: (0, 0)
    ne = wr.shape[1]
    return pl.pallas_call(
        _merge_kernel,
        grid=(n_batch, nt),
        in_specs=[pl.BlockSpec((tm, d), row),
                  pl.BlockSpec((tm, 2 * d), row),
                  pl.BlockSpec((tm, w), lambda b, t: (t, b)),
                  pl.BlockSpec((tm, w), row),
                  pl.BlockSpec(wbs.shape, const),
                  pl.BlockSpec(wba.shape, const),
                  pl.BlockSpec(wout.shape, const),
                  pl.BlockSpec((1, d), const),
                  pl.BlockSpec(wr.shape, const),
                  pl.BlockSpec((1, ne), const)],
        out_specs=[pl.BlockSpec((tm, d), row),
                   pl.BlockSpec((tm, d), row),
                   pl.BlockSpec((tm, ne), row)],
        out_shape=[jax.ShapeDtypeStruct((n, d), F32),
                   jax.ShapeDtypeStruct((n, d), BF16),
                   jax.ShapeDtypeStruct((n, ne), F32)],
        compiler_params=_cparams(("parallel", "parallel")),
        name="merge",
    )(x2d, gates, y_tb, o, wbs, wba, wout, gffn, wr, br)


def _moe_kernel(be_ref, nu_ref, x_ref, wgu_ref, bgu_ref, wd_ref, bd_ref, o_ref, wgu_s, wd_s):
    i = pl.program_id(0)
    active = i < nu_ref[0]
    fresh = jnp.logical_or(i == 0, be_ref[i] != be_ref[jnp.maximum(i - 1, 0)])
    pc = PAIR_COLS
    hc = pc // 2
    n_chunk = wgu_ref.shape[2] // pc

    @pl.when(jnp.logical_and(active, fresh))
    def _stage_weights():
        r = lax.broadcasted_iota(jnp.int32, (pc, pc), 0)
        c = lax.broadcasted_iota(jnp.int32, (pc, pc), 1)
        perm = (r == jnp.where(c < hc, 2 * c, 2 * (c - hc) + 1)).astype(BF16)
        for j in range(n_chunk):
            wj = wgu_ref[0, :, j * pc:(j + 1) * pc].astype(BF16)
            wgu_s[:, j * pc:(j + 1) * pc] = _dot(wj, perm).astype(BF16)
        wd_s[...] = wd_ref[0].astype(BF16)

    @pl.when(active)
    def _compute():
        gu = _dot(x_ref[...], wgu_s[...]) + bgu_ref[0]
        acts = []
        for j in range(n_chunk):
            g_lin = jnp.minimum(gu[:, j * pc:j * pc + hc], SWIGLU_LIMIT)
            up = jnp.clip(gu[:, j * pc + hc:(j + 1) * pc], -SWIGLU_LIMIT, SWIGLU_LIMIT)
            acts.append(((up + 1.0) * (g_lin * jax.nn.sigmoid(SWIGLU_ALPHA * g_lin))).astype(BF16))
        o_ref[...] = _dot(jnp.concatenate(acts, axis=1), wd_s[...]) + bd_ref[0]

    @pl.when(jnp.logical_not(active))
    def _skip():
        o_ref[...] = jnp.zeros(o_ref.shape, o_ref.dtype)


def _moe_gemm(xs, blk_expert, n_used, wgu, bgu, wd, bd):
    n_rows, d = xs.shape
    bm = MOE_BLOCK_ROWS
    wspec = lambda a: pl.BlockSpec((1,) + a.shape[1:], lambda i, be, nu: (be[i], 0, 0))
    grid_spec = pltpu.PrefetchScalarGridSpec(
        num_scalar_prefetch=2,
        grid=(n_rows // bm,),
        in_specs=[pl.BlockSpec((bm, d), lambda i, be, nu: (i, 0)),
                  wspec(wgu), wspec(bgu), wspec(wd), wspec(bd)],
        out_specs=pl.BlockSpec((bm, d), lambda i, be, nu: (i, 0)),
        scratch_shapes=[pltpu.VMEM(wgu.shape[1:], BF16),
                        pltpu.VMEM(wd.shape[1:], BF16)])
    return pl.pallas_call(
        _moe_kernel,
        grid_spec=grid_spec,
        out_shape=jax.ShapeDtypeStruct((n_rows, d), F32),
        compiler_params=_cparams(("arbitrary",)),
        name="moe_gemm",
    )(blk_expert, n_used, xs, wgu, bgu, wd, bd)


def _combine_kernel(x_ref, *rest, final_norm):
    yg_refs = rest[:TOP_K]
    gate_ref, g_ref, o_ref = rest[TOP_K:]
    acc = x_ref[...]
    gate = gate_ref[...]
    for k in range(TOP_K):
        acc = acc + gate[:, k:k + 1] * yg_refs[k][...]
    o_ref[...] = _rms(acc, g_ref[...]) if final_norm else acc


def _combine(x1, ygs, gate, g, tm, final_norm):
    n, d = x1.shape
    row = pl.BlockSpec((tm, d), lambda i: (i, 0))
    return pl.pallas_call(
        functools.partial(_combine_kernel, final_norm=final_norm),
        grid=(n // tm,),
        in_specs=[row] + [row] * TOP_K + [pl.BlockSpec((tm, TOP_K), lambda i: (i, 0)),
                                          pl.BlockSpec((1, d), lambda i: (0, 0))],
        out_specs=row,
        out_shape=jax.ShapeDtypeStruct((n, d), F32),
        compiler_params=_cparams(("parallel",)),
        name="combine",
    )(x1, *ygs, gate, g)


def _moe(x1, h2b, logits, mp, g_final, final_norm):
    n, d = x1.shape
    ne = mp['ne']
    bm = MOE_BLOCK_ROWS
    top_val, top_idx = lax.top_k(logits[:, :ne], TOP_K)
    gate = jax.nn.softmax(top_val, axis=-1)
    m = n * TOP_K
    e_flat = top_idx.reshape(m).astype(jnp.int32)
    order = jnp.argsort(e_flat).astype(jnp.int32)
    slot = jnp.argsort(order).astype(jnp.int32)
    counts = jnp.sum((e_flat[:, None] == jnp.arange(ne, dtype=jnp.int32)[None, :])
                     .astype(jnp.int32), axis=0)
    start = jnp.cumsum(counts) - counts
    padded = (counts + bm - 1) // bm * bm
    pend = jnp.cumsum(padded)
    pstart = pend - padded
    dest = (pstart - start)[e_flat] + slot
    n_rows = (m + bm - 1) // bm * bm + ne * bm
    blk_start = jnp.arange(n_rows // bm, dtype=jnp.int32) * bm
    blk_expert = jnp.minimum(jnp.searchsorted(pend, blk_start, side='right'),
                             ne - 1).astype(jnp.int32)
    n_used = (pend[-1] // bm).astype(jnp.int32).reshape(1)
    row = jnp.arange(n_rows, dtype=jnp.int32)
    row_e = jnp.repeat(blk_expert, bm)
    src = jnp.clip(row - (pstart - start)[row_e], 0, m - 1)
    row_tok = order[src] // TOP_K
    xs = h2b[row_tok]
    ys = _moe_gemm(xs, blk_expert, n_used, mp['wgu'], mp['bgu'], mp['wd'], mp['bd'])
    dest_k = dest.reshape(n, TOP_K)
    ygs = [ys[dest_k[:, k]] for k in range(TOP_K)]
    tm = min(256, n)
    return _combine(x1, ygs, gate, g_final, tm, final_norm)


def _rope_tables(pos, hd):
    half = hd // 2
    inv_freq = ROPE_THETA ** (-jnp.arange(half, dtype=F32) / half)
    ang = pos.astype(F32)[:, None] * inv_freq[None, :]
    cos = jnp.cos(ang)
    sin = jnp.sin(ang)
    reps = LANES // hd
    cos_t = jnp.tile(jnp.concatenate([cos, cos], axis=1), (1, reps))
    sin_t = jnp.tile(jnp.concatenate([-sin, sin], axis=1), (1, reps))
    return cos_t, sin_t


def _layer_params(lp):
    d = lp['w_in'].shape[0]
    ne = lp['w_router'].shape[1]
    wr = jnp.zeros((d, LANES), F32).at[:, :ne].set(lp['w_router']).astype(BF16)
    br = jnp.zeros((1, LANES), F32).at[0, :ne].set(lp['b_router'])
    return dict(
        g_mix=lp['norm_mix_g'].reshape(1, d),
        win=lp['w_in'].astype(BF16),
        wgate=lp['w_gate'].astype(BF16),
        bgate=lp['b_gate'].reshape(1, -1),
        s5=_s5_layout(lp),
        lam=[lp[k].reshape(1, -1) for k in ('lambda_q1', 'lambda_k1', 'lambda_q2', 'lambda_k2')],
        g_sub=lp['attn_subln_g'].reshape(1, -1),
        wbs=lp['w_branch_ssm'].astype(BF16),
        wba=lp['w_branch_attn'].astype(BF16),
        wout=lp['w_out'].astype(BF16),
        g_ffn=lp['norm_ffn_g'].reshape(1, d),
        wr=wr, br=br,
        moe=dict(ne=ne,
                 wgu=lp['w_gate_up'],
                 bgu=lp['b_gate_up'].reshape(ne, -1, PAIR_COLS // 2, 2).transpose(0, 1, 3, 2)
                 .reshape(ne, 1, -1),
                 wd=lp['w_down'],
                 bd=lp['b_down'][:, None, :]))


def _query_rows(qb, bsz, dq, nh, hd):
    qt = qb.reshape(bsz, dq, nh, 2, hd).transpose(0, 3, 2, 1, 4)
    ec = jnp.eye(2, dtype=jnp.bool_)[None, :, None, None, :, None]
    out = jnp.where(ec, qt[:, :, :, :, None, :], jnp.zeros((), qb.dtype))
    return out.reshape(bsz, 2 * nh * dq, 2 * hd)


def _layer(x, pos, x0_re, x0_im, past, P, lambda_init, g_final, final_norm):
    bsz, t_len, d = x.shape
    n = bsz * t_len
    x2d = x.reshape(n, d)
    sp = P['s5']
    g, p = x0_re.shape[1], x0_re.shape[2]
    w = P['win'].shape[1] // 4
    nh = w // LANES
    hd = LANES // 2
    cos_t, sin_t = _rope_tables(pos, hd)
    x0 = _state_to_cols(x0_re.astype(F32), x0_im.astype(F32), sp['ns'])
    if past is None:
        n_batch, rows_t, tm = bsz, t_len, min(256, t_len)
    else:
        n_batch, rows_t, tm = 1, n, n
        cos_t = jnp.tile(cos_t, (bsz, 1))
        sin_t = jnp.tile(sin_t, (bsz, 1))
    u, qb, k, v, kb, vb, gates = _proj(x2d, n_batch, rows_t, tm, P['g_mix'], P['win'], P['wgate'],
                                       P['bgate'], cos_t, sin_t, hd // 2, hd ** -0.5)
    if past is None:
        u_tb = u.reshape(t_len * bsz, w)
        y_tb, st = _s5(u_tb, x0, sp, bsz, t_len, min(64, t_len))
        y_in = y_tb.reshape(t_len, bsz * w)
        o = _attn(qb, kb, vb, P['lam'], P['g_sub'], bsz, t_len, min(256, t_len), hd, lambda_init)
    else:
        cache_k, cache_v, page_table, layer = past
        u_tb = u.reshape(bsz, t_len, w).transpose(1, 0, 2).reshape(t_len * bsz, w)
        y_tb, st = _s5(u_tb, x0, sp, bsz, t_len, t_len)
        y_in = y_tb.reshape(t_len, bsz, w).transpose(1, 0, 2).reshape(n, w)
        npg = math.gcd(8, page_table.shape[1])
        cshape = cache_k.shape[:2] + (cache_k.shape[2] * nh, LANES)
        o = _sattn(page_table, _query_rows(qb, bsz, t_len, nh, hd),
                   k.reshape(bsz, t_len * nh, LANES), v.reshape(bsz, t_len * nh, LANES),
                   cache_k.reshape(cshape), cache_v.reshape(cshape), layer,
                   P['lam'], P['g_sub'], npg, t_len, nh, lambda_init)
        o = o.reshape(bsz, nh, t_len, LANES).transpose(0, 2, 1, 3).reshape(n, w)
    x1, h2b, logits = _merge(x2d, gates, y_in, o, n_batch, rows_t, tm,
                             P['wbs'], P['wba'], P['wout'], P['g_ffn'], P['wr'], P['br'])
    out = _moe(x1, h2b, logits, P['moe'], g_final, final_norm)
    st_re, st_im = _cols_to_state(st, sp['ns'], g, p)
    return (out.reshape(bsz, t_len, d), k.reshape(bsz, t_len, nh, LANES),
            v.reshape(bsz, t_len, nh, LANES), st_re.astype(x.dtype), st_im.astype(x.dtype))


def kernel(x_prompt, x_sample, cache_k, cache_v, state_ssm_re, state_ssm_im, page_table, norm_mix_g, w_in, ssm_lambda_re, ssm_lambda_im, ssm_log_dt, ssm_b_re, ssm_b_im, ssm_c_re, ssm_c_im, ssm_d, ssm_w_glu, ssm_b_glu, lambda_q1, lambda_k1, lambda_q2, lambda_k2, attn_subln_g, w_branch_ssm, w_branch_attn, w_gate, b_gate, w_out, norm_ffn_g, w_router, b_router, w_gate_up, b_gate_up, w_down, b_down, norm_final_g):
    depth = w_in.shape[0]
    past_len = page_table.shape[1] * cache_k.shape[2]
    pos_prompt = jnp.arange(x_prompt.shape[1], dtype=jnp.int32)
    pos_sample = past_len + jnp.arange(x_sample.shape[1], dtype=jnp.int32)
    g, p = state_ssm_re.shape[2], state_ssm_re.shape[3]
    zero_state = jnp.zeros((x_prompt.shape[0], g, p), F32)
    g_final = norm_final_g.reshape(1, -1)
    names = ('norm_mix_g', 'w_in', 'ssm_lambda_re', 'ssm_lambda_im', 'ssm_log_dt', 'ssm_b_re',
             'ssm_b_im', 'ssm_c_re', 'ssm_c_im', 'ssm_d', 'ssm_w_glu', 'ssm_b_glu', 'lambda_q1',
             'lambda_k1', 'lambda_q2', 'lambda_k2', 'attn_subln_g', 'w_branch_ssm',
             'w_branch_attn', 'w_gate', 'b_gate', 'w_out', 'norm_ffn_g', 'w_router', 'b_router',
             'w_gate_up', 'b_gate_up', 'w_down', 'b_down')
    stacked = (norm_mix_g, w_in, ssm_lambda_re, ssm_lambda_im, ssm_log_dt, ssm_b_re, ssm_b_im,
               ssm_c_re, ssm_c_im, ssm_d, ssm_w_glu, ssm_b_glu, lambda_q1, lambda_k1, lambda_q2,
               lambda_k2, attn_subln_g, w_branch_ssm, w_branch_attn, w_gate, b_gate, w_out,
               norm_ffn_g, w_router, b_router, w_gate_up, b_gate_up, w_down, b_down)
    xp, xs = x_prompt, x_sample
    outs_p, outs_s = [], []
    for l in range(depth):
        P = _layer_params({nm: a[l] for nm, a in zip(names, stacked)})
        lambda_init = 0.8 - 0.6 * math.exp(-0.3 * l)
        last = l == depth - 1
        xp, kp, vp, srp, sip = _layer(xp, pos_prompt, zero_state, zero_state, None, P,
                                      lambda_init, g_final, last)
        xs, ks, vs, srs, sis = _layer(xs, pos_sample, state_ssm_re[l], state_ssm_im[l],
                                      (cache_k, cache_v, page_table, l), P,
                                      lambda_init, g_final, last)
        outs_p.append((kp, vp, srp, sip))
        outs_s.append((ks, vs, srs, sis))
    stack = lambda outs, i: jnp.stack([o[i] for o in outs])
    return (xp, xs, stack(outs_p, 0), stack(outs_p, 1), stack(outs_p, 2), stack(outs_p, 3),
            stack(outs_s, 0), stack(outs_s, 1), stack(outs_s, 2), stack(outs_s, 3))
```

```python
import functools
import math

import jax
import jax.numpy as jnp
from jax import lax
from jax.experimental import pallas as pl
from jax.experimental.pallas import tpu as pltpu

F32 = jnp.float32
BF16 = jnp.bfloat16

NORM_EPS = 1e-6
ROPE_THETA = 10000.0
NEG_INF = -1e30
TOP_K = 4
SWIGLU_LIMIT = 7.0
SWIGLU_ALPHA = 1.702
LANES = 128
MOE_BLOCK_ROWS = 256
PAIR_COLS = 256
ATTN_KEY_CHUNK = 1024
VMEM_LIMIT = 56 * 1024 * 1024


def _cparams(sem):
    return pltpu.CompilerParams(dimension_semantics=sem, vmem_limit_bytes=VMEM_LIMIT)


def _dot(a, b):
    return jnp.dot(a, b, preferred_element_type=F32)


def _dot_nt(a, b):
    return lax.dot_general(a, b, (((1,), (1,)), ((), ())), preferred_element_type=F32)


def _rms(x, g):
    return x * lax.rsqrt(jnp.mean(x * x, axis=-1, keepdims=True) + NORM_EPS) * g


def _proj_kernel(x_ref, g_ref, win_ref, wgate_ref, bgate_ref, cos_ref, sin_ref,
                 u_ref, q_ref, k_ref, v_ref, kb_ref, vb_ref, gates_ref, *, half, scale):
    hb = _rms(x_ref[...], g_ref[...]).astype(BF16)
    proj = _dot(hb, win_ref[...])
    w = u_ref.shape[-1]
    u_ref[...] = proj[:, :w]
    cos = cos_ref[...]
    sin = sin_ref[...]
    lane = lax.broadcasted_iota(jnp.int32, cos.shape, 1)
    lo = (lane % (2 * half)) < half

    def rot(xh):
        fwd = pltpu.roll(xh, LANES - half, axis=1)
        bwd = pltpu.roll(xh, half, axis=1)
        return xh * cos + jnp.where(lo, fwd, bwd) * sin

    nh = w // LANES
    tm = x_ref.shape[0]
    for hh in range(nh):
        sl = slice(hh * LANES, (hh + 1) * LANES)
        qh = rot(proj[:, w + hh * LANES:w + (hh + 1) * LANES])
        q_ref[:, sl] = (qh * scale).astype(BF16)
        kh = rot(proj[:, 2 * w + hh * LANES:2 * w + (hh + 1) * LANES])
        k_ref[pl.ds(hh, tm, stride=nh), :] = kh
        kb_ref[:, sl] = kh.astype(BF16)
        v_ref[pl.ds(hh, tm, stride=nh), :] = proj[:, 3 * w + hh * LANES:3 * w + (hh + 1) * LANES]
    vb_ref[...] = proj[:, 3 * w:].astype(BF16)
    gl = _dot(hb, wgate_ref[...]) + bgate_ref[...]
    gates_ref[...] = jax.nn.sigmoid(gl).astype(BF16)


def _proj(x2d, n_batch, t_len, tm, g, win_b, wgate_b, bgate, cos_t, sin_t, half, scale):
    n, d = x2d.shape
    w = win_b.shape[1] // 4
    nh = w // LANES
    nt = t_len // tm
    row = lambda b, t: (b * nt + t, 0)
    const = lambda b, t: (0, 0)
    outs = pl.pallas_call(
        functools.partial(_proj_kernel, half=half, scale=scale),
        grid=(n_batch, nt),
        in_specs=[pl.BlockSpec((tm, d), row),
                  pl.BlockSpec((1, d), const),
                  pl.BlockSpec(win_b.shape, const),
                  pl.BlockSpec(wgate_b.shape, const),
                  pl.BlockSpec((1, wgate_b.shape[1]), const),
                  pl.BlockSpec((tm, LANES), lambda b, t: (t, 0)),
                  pl.BlockSpec((tm, LANES), lambda b, t: (t, 0))],
        out_specs=[pl.BlockSpec((tm, w), lambda b, t: (t, b)),
                   pl.BlockSpec((tm, w), row),
                   pl.BlockSpec((tm * nh, LANES), row),
                   pl.BlockSpec((tm * nh, LANES), row),
                   pl.BlockSpec((tm, w), row),
                   pl.BlockSpec((tm, w), row),
                   pl.BlockSpec((tm, 2 * d), row)],
        out_shape=[jax.ShapeDtypeStruct((t_len, n_batch * w), F32),
                   jax.ShapeDtypeStruct((n, w), BF16),
                   jax.ShapeDtypeStruct((n * nh, LANES), F32),
                   jax.ShapeDtypeStruct((n * nh, LANES), F32),
                   jax.ShapeDtypeStruct((n, w), BF16),
                   jax.ShapeDtypeStruct((n, w), BF16),
                   jax.ShapeDtypeStruct((n, 2 * d), BF16)],
        compiler_params=_cparams(("parallel", "parallel")),
        name="proj",
    )(x2d, g, win_b, wgate_b, bgate, cos_t, sin_t)
    return outs


def _s5_kernel(u_ref, x0_ref, lre_ref, lim_ref, ldt_ref, bre_ref, bim_ref, cre_ref, cim_ref,
               d_ref, wglu_ref, bglu_ref, y_ref, st_ref,
               zx_ref, wz_ref, are_ref, aim_ref, state_ref, *, nb, tc, ns, sw):
    step = pl.program_id(0)

    @pl.when(step == 0)
    def _init():
        lre = lre_ref[...]
        lim = lim_ref[...]
        dt = jnp.exp(ldt_ref[...])
        mag = jnp.exp(lre * dt)
        are = mag * jnp.cos(lim * dt)
        aim = mag * jnp.sin(lim * dt)
        den = lre * lre + lim * lim
        fre = ((are - 1.0) * lre + aim * lim) / den
        fim = (aim * lre - (are - 1.0) * lim) / den
        are_ref[...] = are
        aim_ref[...] = aim
        for i in range(ns):
            fr = fre[:, i * sw:(i + 1) * sw]
            fi = fim[:, i * sw:(i + 1) * sw]
            br = bre_ref[i]
            bi = bim_ref[i]
            wz_ref[i, :, :sw] = (fr * br - fi * bi).astype(BF16)
            wz_ref[i, :, sw:] = (fr * bi + fi * br).astype(BF16)
        state_ref[...] = x0_ref[...]

    ub = u_ref[...]
    ubb = ub.astype(BF16)
    for i in range(ns):
        zx_ref[:, 2 * sw * i:2 * sw * (i + 1)] = _dot(ubb[:, LANES * i:LANES * (i + 1)], wz_ref[i])

    for i in range(ns):
        c0 = 2 * sw * i
        arb = jnp.broadcast_to(are_ref[:, i * sw:(i + 1) * sw], (nb, sw))
        aib = jnp.broadcast_to(aim_ref[:, i * sw:(i + 1) * sw], (nb, sw))

        def body(t, carry, c0=c0, arb=arb, aib=aib):
            xr, xi = carry
            r0 = pl.multiple_of(t * nb, nb)
            zr = zx_ref[pl.ds(r0, nb), c0:c0 + sw]
            zi = zx_ref[pl.ds(r0, nb), c0 + sw:c0 + 2 * sw]
            nxr = arb * xr - aib * xi + zr
            nxi = arb * xi + aib * xr + zi
            zx_ref[pl.ds(r0, nb), c0:c0 + sw] = nxr
            zx_ref[pl.ds(r0, nb), c0 + sw:c0 + 2 * sw] = nxi
            return nxr, nxi

        xr, xi = lax.fori_loop(0, tc, body,
                               (state_ref[:, c0:c0 + sw], state_ref[:, c0 + sw:c0 + 2 * sw]))
        state_ref[:, c0:c0 + sw] = xr
        state_ref[:, c0 + sw:c0 + 2 * sw] = xi

    ys = []
    for i in range(ns):
        c0 = 2 * sw * i
        xr = zx_ref[:, c0:c0 + sw].astype(BF16)
        xi = zx_ref[:, c0 + sw:c0 + 2 * sw].astype(BF16)
        ys.append(_dot(xr, cre_ref[i]) - _dot(xi, cim_ref[i]))
    y = jnp.concatenate(ys, axis=1) + d_ref[...] * ub
    y = jax.nn.gelu(y)
    gl = _dot(y.astype(BF16), wglu_ref[...]) + bglu_ref[...]
    y_ref[...] = (y * jax.nn.sigmoid(gl)).astype(BF16)

    @pl.when(step == pl.num_programs(0) - 1)
    def _fin():
        st_ref[...] = state_ref[...]


def _s5_layout(lp):
    g, p, h = lp['ssm_b_re'].shape
    gps = LANES // h
    ns = g // gps
    eye = jnp.eye(gps, dtype=jnp.bool_)

    def bd_b(b):
        bb = b.reshape(ns, gps, p, h).transpose(0, 1, 3, 2)
        out = jnp.where(eye[None, :, None, :, None], bb[:, :, :, None, :], 0.0)
        return out.reshape(ns, gps * h, gps * p)

    def bd_c(c):
        cc = c.reshape(ns, gps, h, p).transpose(0, 1, 3, 2)
        out = jnp.where(eye[None, :, None, :, None], cc[:, :, :, None, :], 0.0)
        return out.reshape(ns, gps * p, gps * h)

    return dict(
        lre=lp['ssm_lambda_re'].reshape(1, g * p),
        lim=lp['ssm_lambda_im'].reshape(1, g * p),
        ldt=jnp.repeat(lp['ssm_log_dt'], p).reshape(1, g * p),
        bre=bd_b(lp['ssm_b_re']), bim=bd_b(lp['ssm_b_im']),
        cre=bd_c(lp['ssm_c_re']).astype(BF16), cim=bd_c(lp['ssm_c_im']).astype(BF16),
        d=lp['ssm_d'].reshape(1, g * h),
        wglu=lp['ssm_w_glu'].astype(BF16), bglu=lp['ssm_b_glu'].reshape(1, -1),
        ns=ns, sw=gps * p)


def _state_to_cols(re, im, ns):
    b = re.shape[0]
    return jnp.stack([re.reshape(b, ns, -1), im.reshape(b, ns, -1)], axis=2).reshape(b, -1)


def _cols_to_state(st, ns, g, p):
    b = st.shape[0]
    s = st.reshape(b, ns, 2, -1)
    return s[:, :, 0].reshape(b, g, p), s[:, :, 1].reshape(b, g, p)


def _s5(u_tb, x0, sp, nb, t_len, tc):
    rows = nb * tc
    w = u_tb.shape[1]
    ns, sw = sp['ns'], sp['sw']
    ncol = ns * 2 * sw
    full = lambda a: pl.BlockSpec(a.shape, lambda i: (0,) * a.ndim)
    args = (u_tb, x0, sp['lre'], sp['lim'], sp['ldt'], sp['bre'], sp['bim'], sp['cre'], sp['cim'],
            sp['d'], sp['wglu'], sp['bglu'])
    return pl.pallas_call(
        functools.partial(_s5_kernel, nb=nb, tc=tc, ns=ns, sw=sw),
        grid=(t_len // tc,),
        in_specs=[pl.BlockSpec((rows, w), lambda i: (i, 0))] + [full(a) for a in args[1:]],
        out_specs=[pl.BlockSpec((rows, w), lambda i: (i, 0)),
                   pl.BlockSpec((nb, ncol), lambda i: (0, 0))],
        out_shape=[jax.ShapeDtypeStruct((t_len * nb, w), BF16),
                   jax.ShapeDtypeStruct((nb, ncol), F32)],
        scratch_shapes=[pltpu.VMEM((rows, ncol), F32),
                        pltpu.VMEM((ns, LANES, 2 * sw), BF16),
                        pltpu.VMEM((1, ns * sw), F32),
                        pltpu.VMEM((1, ns * sw), F32),
                        pltpu.VMEM((nb, ncol), F32)],
        compiler_params=_cparams(("arbitrary",)),
        name="s5",
    )(*args)


def _diff_lambda(lq1, lk1, lq2, lk2, lambda_init):
    return (jnp.exp(jnp.sum(lq1[...] * lk1[...], keepdims=True))
            - jnp.exp(jnp.sum(lq2[...] * lk2[...], keepdims=True)) + lambda_init)


def _attn_kernel(q_ref, k_ref, v_ref, lq1, lk1, lq2, lk2, g_ref, o_ref, *, tq, hd, lambda_init):
    t_len = q_ref.shape[0]
    lam = _diff_lambda(lq1, lk1, lq2, lk2, lambda_init)
    lo = lax.broadcasted_iota(jnp.int32, (tq, LANES), 1) < hd
    max_blocks = max(ATTN_KEY_CHUNK // tq, 1)

    for qi in range(t_len // tq):
        q = q_ref[qi * tq:(qi + 1) * tq, :]
        zero = jnp.zeros_like(q)
        qs = jnp.concatenate([jnp.where(lo, q, zero), jnp.where(lo, zero, q)], axis=0)
        m = jnp.full((1, 2 * tq), NEG_INF, F32)
        l = jnp.zeros((1, 2 * tq), F32)
        acc = jnp.zeros((LANES, 2 * tq), F32)
        n_blocks = qi + 1
        n_chunks = -(-n_blocks // max_blocks)
        k0 = 0
        for ci in range(n_chunks):
            ksz = (n_blocks // n_chunks + (1 if ci < n_blocks % n_chunks else 0)) * tq
            s = _dot_nt(k_ref[k0:k0 + ksz, :], qs)
            if ci == n_chunks - 1:
                kpos = k0 + lax.broadcasted_iota(jnp.int32, (ksz, 2 * tq), 0)
                qc = lax.broadcasted_iota(jnp.int32, (ksz, 2 * tq), 1)
                s = jnp.where(kpos <= qi * tq + jnp.where(qc >= tq, qc - tq, qc), s, NEG_INF)
            m_new = jnp.maximum(m, jnp.max(s, axis=0, keepdims=True))
            alpha = jnp.exp(m - m_new)
            p = jnp.exp(s - m_new)
            l = alpha * l + jnp.sum(p, axis=0, keepdims=True)
            pv = lax.dot_general(v_ref[k0:k0 + ksz, :], p.astype(BF16), (((0,), (0,)), ((), ())),
                                 preferred_element_type=F32)
            acc = alpha * acc + pv
            m = m_new
            k0 += ksz
        o_t = (acc[:, :tq] / l[:, :tq]) - lam * (acc[:, tq:] / l[:, tq:])
        ms = jnp.mean(o_t * o_t, axis=0, keepdims=True)
        o_t = o_t * lax.rsqrt(ms + NORM_EPS) * g_ref[...] * (1.0 - lambda_init)
        o_ref[qi * tq:(qi + 1) * tq, :] = o_t.T.astype(BF16)


def _attn(qb, kb, vb, lam_params, g, n_batch, t_len, tq, hd, lambda_init):
    n, w = qb.shape
    nh = w // LANES
    blk = pl.BlockSpec((t_len, LANES), lambda b, h: (b, h))
    small = lambda a: pl.BlockSpec(a.shape, lambda b, h: (0, 0))
    g_cols = jnp.broadcast_to(g.reshape(LANES, 1), (LANES, tq))
    return pl.pallas_call(
        functools.partial(_attn_kernel, tq=tq, hd=hd, lambda_init=lambda_init),
        grid=(n_batch, nh),
        in_specs=[blk, blk, blk] + [small(a) for a in lam_params] + [small(g_cols)],
        out_specs=blk,
        out_shape=jax.ShapeDtypeStruct((n, w), BF16),
        compiler_params=_cparams(("parallel", "parallel")),
        name="attn_prompt",
    )(qb, kb, vb, *lam_params, g_cols)


def _sattn_kernel(pt_ref, q_ref, kown_ref, vown_ref, *rest, npg, prow, dq, nh, lambda_init):
    k_refs = rest[:npg]
    v_refs = rest[npg:2 * npg]
    lq1, lk1, lq2, lk2, g_ref, o_ref, kb_ref, vb_ref, m_ref, l_ref, acc_ref = rest[2 * npg:]
    j = pl.program_id(1)
    q = q_ref[0]
    nq = q.shape[0]

    def own_head(ncols):
        r = lax.broadcasted_iota(jnp.int32, (nq, ncols), 0)
        c = lax.broadcasted_iota(jnp.int32, (nq, ncols), 1)
        return r, c, (c % nh) == ((r // dq) % nh)

    @pl.when(j == 0)
    def _init():
        m_ref[...] = jnp.full(m_ref.shape, NEG_INF, F32)
        l_ref[...] = jnp.zeros(l_ref.shape, F32)
        acc_ref[...] = jnp.zeros(acc_ref.shape, F32)

    def update(c, s, v):
        m_prev = m_ref[c]
        m_new = jnp.maximum(m_prev, jnp.max(s, axis=1, keepdims=True))
        alpha = jnp.exp(m_prev - m_new)
        p = jnp.exp(s - m_new)
        l_ref[c] = alpha * l_ref[c] + jnp.sum(p, axis=1, keepdims=True)
        acc_ref[c] = alpha * acc_ref[c] + _dot(p.astype(BF16), v)
        m_ref[c] = m_new

    _, _, valid = own_head(prow)
    for p_ in range(npg):
        kb_ref[p_] = k_refs[p_][0].astype(BF16)
        vb_ref[p_] = v_refs[p_][0].astype(BF16)
        update(p_, jnp.where(valid, _dot_nt(q, kb_ref[p_]), NEG_INF), vb_ref[p_])

    @pl.when(j == pl.num_programs(1) - 1)
    def _fin():
        r, c, valid_own = own_head(dq * nh)
        valid_own = jnp.logical_and(valid_own, (c // nh) <= (r % dq))
        s = _dot_nt(q, kown_ref[0].astype(BF16))
        update(0, jnp.where(valid_own, s, NEG_INF), vown_ref[0].astype(BF16))
        m_all = functools.reduce(jnp.maximum, [m_ref[c_] for c_ in range(npg)])
        scales = [jnp.exp(m_ref[c_] - m_all) for c_ in range(npg)]
        l_all = sum(scales[c_] * l_ref[c_] for c_ in range(npg))
        acc_all = sum(scales[c_] * acc_ref[c_] for c_ in range(npg))
        lam = _diff_lambda(lq1, lk1, lq2, lk2, lambda_init)
        o_all = acc_all / l_all
        o = o_all[:nq // 2] - lam * o_all[nq // 2:]
        o_ref[0] = (_rms(o, g_ref[...]) * (1.0 - lambda_init)).astype(BF16)


def _sattn(page_table, q_rows, k_own, v_own, cache_k, cache_v, layer, lam_params, g, npg, dq, nh,
           lambda_init):
    nb, nq, _ = q_rows.shape
    prow = cache_k.shape[2]
    n_pages = page_table.shape[1]
    nsteps = n_pages // npg

    def page_spec(p_):
        return pl.BlockSpec((None, 1, prow, LANES),
                            lambda b, j, pt: (layer, pt[b, j * npg + p_], 0, 0))

    per_b = lambda a: pl.BlockSpec((1,) + a.shape[1:], lambda b, j, pt: (b, 0, 0))
    small = lambda a: pl.BlockSpec(a.shape, lambda b, j, pt: (0, 0))
    grid_spec = pltpu.PrefetchScalarGridSpec(
        num_scalar_prefetch=1,
        grid=(nb, nsteps),
        in_specs=([per_b(q_rows), per_b(k_own), per_b(v_own)]
                  + [page_spec(p_) for p_ in range(npg)]
                  + [page_spec(p_) for p_ in range(npg)]
                  + [small(a) for a in lam_params] + [small(g)]),
        out_specs=pl.BlockSpec((1, nq // 2, LANES), lambda b, j, pt: (b, 0, 0)),
        scratch_shapes=[pltpu.VMEM((npg, prow, LANES), BF16),
                        pltpu.VMEM((npg, prow, LANES), BF16),
                        pltpu.VMEM((npg, nq, 1), F32),
                        pltpu.VMEM((npg, nq, 1), F32),
                        pltpu.VMEM((npg, nq, LANES), F32)])
    return pl.pallas_call(
        functools.partial(_sattn_kernel, npg=npg, prow=prow, dq=dq, nh=nh,
                          lambda_init=lambda_init),
        grid_spec=grid_spec,
        out_shape=jax.ShapeDtypeStruct((nb, nq // 2, LANES), BF16),
        compiler_params=_cparams(("parallel", "arbitrary")),
        name="attn_sample",
    )(page_table, q_rows, k_own, v_own, *([cache_k] * npg), *([cache_v] * npg), *lam_params, g)


def _merge_kernel(x_ref, gates_ref, y_ref, o_ref, wbs_ref, wba_ref, wout_ref, gffn_ref,
                  wr_ref, br_ref, x1_ref, h2_ref, ri_ref, rf_ref, cnt_ref, run_ref, *, ne):
    first = jnp.logical_and(pl.program_id(0) == 0, pl.program_id(1) == 0)

    @pl.when(first)
    def _init():
        run_ref[...] = jnp.zeros(run_ref.shape, F32)

    tm, d = x_ref.shape
    a = _dot(y_ref[...], wbs_ref[...])
    b = _dot(o_ref[...], wba_ref[...])
    merged = gates_ref[:, :d].astype(F32) * a + gates_ref[:, d:].astype(F32) * b
    x1 = x_ref[...] + _dot(merged.astype(BF16), wout_ref[...])
    x1_ref[...] = x1
    h2b = _rms(x1, gffn_ref[...]).astype(BF16)
    bits = lax.bitcast_convert_type(h2b.astype(F32), jnp.uint32)
    h2_ref[...] = bits[:, d // 2:] | (bits[:, :d // 2] >> 16)

    logits = _dot(h2b, wr_ref[...]) + br_ref[...]
    lane = lax.broadcasted_iota(jnp.int32, logits.shape, 1)
    lane_f = lane.astype(F32)
    cur = jnp.where(lane < ne, logits, -jnp.inf)
    vals, idxs, hots = [], [], []
    for _ in range(TOP_K):
        mk = jnp.max(cur, axis=1, keepdims=True)
        ik = jnp.min(jnp.where(cur == mk, lane_f, float(LANES)), axis=1, keepdims=True)
        hot = lane_f == ik
        cur = jnp.where(hot, -jnp.inf, cur)
        vals.append(mk)
        idxs.append(ik.astype(jnp.int32))
        hots.append(hot)
    exps = [jnp.exp(v - vals[0]) for v in vals]
    denom = sum(exps[1:], exps[0])
    chosen = functools.reduce(jnp.logical_or, hots)
    cnt = jnp.where(chosen, 1.0, 0.0)
    r = lax.broadcasted_iota(jnp.int32, (tm, tm), 0)
    c = lax.broadcasted_iota(jnp.int32, (tm, tm), 1)
    before = _dot((c < r).astype(BF16), cnt.astype(BF16)) + run_ref[...]
    ri = jnp.zeros(logits.shape, jnp.int32)
    rf = jnp.zeros(logits.shape, F32)
    for k in range(TOP_K):
        rank = jnp.sum(jnp.where(hots[k], before, 0.0), axis=1, keepdims=True).astype(jnp.int32)
        ri = jnp.where(lane == k, idxs[k], ri)
        ri = jnp.where(lane == TOP_K + k, rank, ri)
        rf = jnp.where(lane == k, exps[k] / denom, rf)
    ri_ref[...] = ri
    rf_ref[...] = rf
    run_ref[...] = run_ref[...] + jnp.sum(cnt, axis=0, keepdims=True)
    cnt_ref[...] = run_ref[...]


def _merge(x2d, gates, y_tb, o, n_batch, t_len, tm, wbs, wba, wout, gffn, wr, br, ne):
    n, d = x2d.shape
    w = o.shape[1]
    nt = t_len // tm
    row = lambda b, t: (b * nt + t, 0)
    const = lambda b, t: (0, 0)
    return pl.pallas_call(
        functools.partial(_merge_kernel, ne=ne),
        grid=(n_batch, nt),
        in_specs=[pl.BlockSpec((tm, d), row),
                  pl.BlockSpec((tm, 2 * d), row),
                  pl.BlockSpec((tm, w), lambda b, t: (t, b)),
                  pl.BlockSpec((tm, w), row),
                  pl.BlockSpec(wbs.shape, const),
                  pl.BlockSpec(wba.shape, const),
                  pl.BlockSpec(wout.shape, const),
                  pl.BlockSpec((1, d), const),
                  pl.BlockSpec(wr.shape, const),
                  pl.BlockSpec((1, LANES), const)],
        out_specs=[pl.BlockSpec((tm, d), row),
                   pl.BlockSpec((tm, d // 2), row),
                   pl.BlockSpec((tm, LANES), row),
                   pl.BlockSpec((tm, LANES), row),
                   pl.BlockSpec((1, LANES), const)],
        out_shape=[jax.ShapeDtypeStruct((n, d), F32),
                   jax.ShapeDtypeStruct((n, d // 2), jnp.uint32),
                   jax.ShapeDtypeStruct((n, LANES), jnp.int32),
                   jax.ShapeDtypeStruct((n, LANES), F32),
                   jax.ShapeDtypeStruct((1, LANES), F32)],
        scratch_shapes=[pltpu.VMEM((1, LANES), F32)],
        compiler_params=_cparams(("arbitrary", "arbitrary")),
        name="merge",
    )(x2d, gates, y_tb, o, wbs, wba, wout, gffn, wr, br)


def _moe_kernel(be_ref, nu_ref, x_ref, wgu_ref, bgu_ref, wd_ref, bd_ref, o_ref, wgu_s, wd_s):
    i = pl.program_id(0)
    active = i < nu_ref[0]
    fresh = jnp.logical_or(i == 0, be_ref[i] != be_ref[jnp.maximum(i - 1, 0)])
    pc = PAIR_COLS
    hc = pc // 2
    n_chunk = wgu_ref.shape[2] // pc

    @pl.when(jnp.logical_and(active, fresh))
    def _stage_weights():
        r = lax.broadcasted_iota(jnp.int32, (pc, pc), 0)
        c = lax.broadcasted_iota(jnp.int32, (pc, pc), 1)
        perm = (r == jnp.where(c < hc, 2 * c, 2 * (c - hc) + 1)).astype(BF16)
        for j in range(n_chunk):
            wj = wgu_ref[0, :, j * pc:(j + 1) * pc].astype(BF16)
            wgu_s[:, j * pc:(j + 1) * pc] = _dot(wj, perm).astype(BF16)
        wd_s[...] = wd_ref[0].astype(BF16)

    @pl.when(active)
    def _compute():
        words = x_ref[...]
        x = jnp.concatenate(
            [lax.bitcast_convert_type(words << 16, F32),
             lax.bitcast_convert_type(words & jnp.uint32(0xFFFF0000), F32)], axis=1).astype(BF16)
        gu = _dot(x, wgu_s[...]) + bgu_ref[0]
        acts = []
        for j in range(n_chunk):
            g_lin = jnp.minimum(gu[:, j * pc:j * pc + hc], SWIGLU_LIMIT)
            up = jnp.clip(gu[:, j * pc + hc:(j + 1) * pc], -SWIGLU_LIMIT, SWIGLU_LIMIT)
            acts.append(((up + 1.0) * (g_lin * jax.nn.sigmoid(SWIGLU_ALPHA * g_lin))).astype(BF16))
        o_ref[...] = _dot(jnp.concatenate(acts, axis=1), wd_s[...]) + bd_ref[0]

    @pl.when(jnp.logical_not(active))
    def _skip():
        o_ref[...] = jnp.zeros(o_ref.shape, o_ref.dtype)


def _moe_gemm(xs, blk_expert, n_used, wgu, bgu, wd, bd):
    n_rows = xs.shape[0]
    d = wd.shape[2]
    bm = MOE_BLOCK_ROWS
    wspec = lambda a: pl.BlockSpec((1,) + a.shape[1:], lambda i, be, nu: (be[i], 0, 0))
    grid_spec = pltpu.PrefetchScalarGridSpec(
        num_scalar_prefetch=2,
        grid=(n_rows // bm,),
        in_specs=[pl.BlockSpec((bm, xs.shape[1]), lambda i, be, nu: (i, 0)),
                  wspec(wgu), wspec(bgu), wspec(wd), wspec(bd)],
        out_specs=pl.BlockSpec((bm, d), lambda i, be, nu: (i, 0)),
        scratch_shapes=[pltpu.VMEM(wgu.shape[1:], BF16),
                        pltpu.VMEM(wd.shape[1:], BF16)])
    return pl.pallas_call(
        _moe_kernel,
        grid_spec=grid_spec,
        out_shape=jax.ShapeDtypeStruct((n_rows, d), F32),
        compiler_params=_cparams(("arbitrary",)),
        name="moe_gemm",
    )(blk_expert, n_used, xs, wgu, bgu, wd, bd)


def _combine_kernel(x_ref, *rest, final_norm):
    yg_refs = rest[:TOP_K]
    gate_ref, g_ref, o_ref = rest[TOP_K:]
    acc = x_ref[...]
    gate = gate_ref[...]
    for k in range(TOP_K):
        acc = acc + gate[:, k:k + 1] * yg_refs[k][...]
    o_ref[...] = _rms(acc, g_ref[...]) if final_norm else acc


def _combine(x1, ygs, gate, g, tm, final_norm):
    n, d = x1.shape
    row = pl.BlockSpec((tm, d), lambda i: (i, 0))
    return pl.pallas_call(
        functools.partial(_combine_kernel, final_norm=final_norm),
        grid=(n // tm,),
        in_specs=[row] + [row] * TOP_K + [pl.BlockSpec((tm, TOP_K), lambda i: (i, 0)),
                                          pl.BlockSpec((1, d), lambda i: (0, 0))],
        out_specs=row,
        out_shape=jax.ShapeDtypeStruct((n, d), F32),
        compiler_params=_cparams(("parallel",)),
        name="combine",
    )(x1, *ygs, gate, g)


def _moe(x1, h2p, route_i, route_f, count_row, mp, g_final, final_norm):
    n, d = x1.shape
    ne = mp['ne']
    bm = MOE_BLOCK_ROWS
    m = n * TOP_K
    e_idx = route_i[:, :TOP_K]
    rank = route_i[:, TOP_K:2 * TOP_K]
    gate = route_f[:, :TOP_K]
    counts = count_row[0, :ne].astype(jnp.int32)
    start = jnp.cumsum(counts) - counts
    padded = (counts + bm - 1) // bm * bm
    pend = jnp.cumsum(padded)
    pstart = pend - padded
    dest = pstart[e_idx] + rank
    n_rows = (m + bm - 1) // bm * bm + ne * bm
    blk_start = jnp.arange(n_rows // bm, dtype=jnp.int32) * bm
    blk_expert = jnp.minimum(
        jnp.sum((blk_start[:, None] >= pend[None, :]).astype(jnp.int32), axis=1), ne - 1)
    n_used = (pend[-1] // bm).astype(jnp.int32).reshape(1)
    tok = jnp.broadcast_to(jnp.arange(n, dtype=jnp.int32)[:, None], (n, TOP_K))
    _, tok_sorted = lax.sort_key_val(dest.reshape(m), tok.reshape(m))
    blk_shift = (pstart - start)[blk_expert]
    src = jnp.arange(n_rows, dtype=jnp.int32) - jnp.repeat(blk_shift, bm)
    row_tok = tok_sorted[jnp.clip(src, 0, m - 1)]
    xs = h2p[row_tok]
    ys = _moe_gemm(xs, blk_expert, n_used, mp['wgu'], mp['bgu'], mp['wd'], mp['bd'])
    ygs = [ys[dest[:, k]] for k in range(TOP_K)]
    tm = min(256, n)
    return _combine(x1, ygs, gate, g_final, tm, final_norm)


def _rope_tables(pos, hd):
    half = hd // 2
    inv_freq = ROPE_THETA ** (-jnp.arange(half, dtype=F32) / half)
    ang = pos.astype(F32)[:, None] * inv_freq[None, :]
    cos = jnp.cos(ang)
    sin = jnp.sin(ang)
    reps = LANES // hd
    cos_t = jnp.tile(jnp.concatenate([cos, cos], axis=1), (1, reps))
    sin_t = jnp.tile(jnp.concatenate([-sin, sin], axis=1), (1, reps))
    return cos_t, sin_t


def _layer_params(lp):
    d = lp['w_in'].shape[0]
    ne = lp['w_router'].shape[1]
    wr = jnp.zeros((d, LANES), F32).at[:, :ne].set(lp['w_router']).astype(BF16)
    br = jnp.zeros((1, LANES), F32).at[0, :ne].set(lp['b_router'])
    return dict(
        g_mix=lp['norm_mix_g'].reshape(1, d),
        win=lp['w_in'].astype(BF16),
        wgate=lp['w_gate'].astype(BF16),
        bgate=lp['b_gate'].reshape(1, -1),
        s5=_s5_layout(lp),
        lam=[lp[k].reshape(1, -1) for k in ('lambda_q1', 'lambda_k1', 'lambda_q2', 'lambda_k2')],
        g_sub=lp['attn_subln_g'].reshape(1, -1),
        wbs=lp['w_branch_ssm'].astype(BF16),
        wba=lp['w_branch_attn'].astype(BF16),
        wout=lp['w_out'].astype(BF16),
        g_ffn=lp['norm_ffn_g'].reshape(1, d),
        wr=wr, br=br,
        moe=dict(ne=ne,
                 wgu=lp['w_gate_up'],
                 bgu=lp['b_gate_up'].reshape(ne, -1, PAIR_COLS // 2, 2).transpose(0, 1, 3, 2)
                 .reshape(ne, 1, -1),
                 wd=lp['w_down'],
                 bd=lp['b_down'][:, None, :]))


def _query_rows(qb, bsz, dq, nh, hd):
    qt = qb.reshape(bsz, dq, nh, 2, hd).transpose(0, 3, 2, 1, 4)
    ec = jnp.eye(2, dtype=jnp.bool_)[None, :, None, None, :, None]
    out = jnp.where(ec, qt[:, :, :, :, None, :], jnp.zeros((), qb.dtype))
    return out.reshape(bsz, 2 * nh * dq, 2 * hd)


def _layer(x, pos, x0_re, x0_im, past, P, lambda_init, g_final, final_norm):
    bsz, t_len, d = x.shape
    n = bsz * t_len
    x2d = x.reshape(n, d)
    sp = P['s5']
    g, p = x0_re.shape[1], x0_re.shape[2]
    w = P['win'].shape[1] // 4
    nh = w // LANES
    hd = LANES // 2
    cos_t, sin_t = _rope_tables(pos, hd)
    x0 = _state_to_cols(x0_re.astype(F32), x0_im.astype(F32), sp['ns'])
    if past is None:
        n_batch, rows_t, tm = bsz, t_len, min(256, t_len)
    else:
        n_batch, rows_t, tm = 1, n, n
        cos_t = jnp.tile(cos_t, (bsz, 1))
        sin_t = jnp.tile(sin_t, (bsz, 1))
    u, qb, k, v, kb, vb, gates = _proj(x2d, n_batch, rows_t, tm, P['g_mix'], P['win'], P['wgate'],
                                       P['bgate'], cos_t, sin_t, hd // 2, hd ** -0.5)
    if past is None:
        u_tb = u.reshape(t_len * bsz, w)
        y_tb, st = _s5(u_tb, x0, sp, bsz, t_len, min(64, t_len))
        y_in = y_tb.reshape(t_len, bsz * w)
        o = _attn(qb, kb, vb, P['lam'], P['g_sub'], bsz, t_len, min(256, t_len), hd, lambda_init)
    else:
        cache_k, cache_v, page_table, layer = past
        u_tb = u.reshape(bsz, t_len, w).transpose(1, 0, 2).reshape(t_len * bsz, w)
        y_tb, st = _s5(u_tb, x0, sp, bsz, t_len, t_len)
        y_in = y_tb.reshape(t_len, bsz, w).transpose(1, 0, 2).reshape(n, w)
        npg = math.gcd(8, page_table.shape[1])
        cshape = cache_k.shape[:2] + (cache_k.shape[2] * nh, LANES)
        o = _sattn(page_table, _query_rows(qb, bsz, t_len, nh, hd),
                   k.reshape(bsz, t_len * nh, LANES), v.reshape(bsz, t_len * nh, LANES),
                   cache_k.reshape(cshape), cache_v.reshape(cshape), layer,
                   P['lam'], P['g_sub'], npg, t_len, nh, lambda_init)
        o = o.reshape(bsz, nh, t_len, LANES).transpose(0, 2, 1, 3).reshape(n, w)
    x1, h2p, route_i, route_f, count_row = _merge(
        x2d, gates, y_in, o, n_batch, rows_t, tm, P['wbs'], P['wba'], P['wout'], P['g_ffn'],
        P['wr'], P['br'], P['moe']['ne'])
    out = _moe(x1, h2p, route_i, route_f, count_row, P['moe'], g_final, final_norm)
    st_re, st_im = _cols_to_state(st, sp['ns'], g, p)
    return (out.reshape(bsz, t_len, d), k.reshape(bsz, t_len, nh, LANES),
            v.reshape(bsz, t_len, nh, LANES), st_re.astype(x.dtype), st_im.astype(x.dtype))


def kernel(x_prompt, x_sample, cache_k, cache_v, state_ssm_re, state_ssm_im, page_table, norm_mix_g, w_in, ssm_lambda_re, ssm_lambda_im, ssm_log_dt, ssm_b_re, ssm_b_im, ssm_c_re, ssm_c_im, ssm_d, ssm_w_glu, ssm_b_glu, lambda_q1, lambda_k1, lambda_q2, lambda_k2, attn_subln_g, w_branch_ssm, w_branch_attn, w_gate, b_gate, w_out, norm_ffn_g, w_router, b_router, w_gate_up, b_gate_up, w_down, b_down, norm_final_g):
    depth = w_in.shape[0]
    past_len = page_table.shape[1] * cache_k.shape[2]
    pos_prompt = jnp.arange(x_prompt.shape[1], dtype=jnp.int32)
    pos_sample = past_len + jnp.arange(x_sample.shape[1], dtype=jnp.int32)
    g, p = state_ssm_re.shape[2], state_ssm_re.shape[3]
    zero_state = jnp.zeros((x_prompt.shape[0], g, p), F32)
    g_final = norm_final_g.reshape(1, -1)
    names = ('norm_mix_g', 'w_in', 'ssm_lambda_re', 'ssm_lambda_im', 'ssm_log_dt', 'ssm_b_re',
             'ssm_b_im', 'ssm_c_re', 'ssm_c_im', 'ssm_d', 'ssm_w_glu', 'ssm_b_glu', 'lambda_q1',
             'lambda_k1', 'lambda_q2', 'lambda_k2', 'attn_subln_g', 'w_branch_ssm',
             'w_branch_attn', 'w_gate', 'b_gate', 'w_out', 'norm_ffn_g', 'w_router', 'b_router',
             'w_gate_up', 'b_gate_up', 'w_down', 'b_down')
    stacked = (norm_mix_g, w_in, ssm_lambda_re, ssm_lambda_im, ssm_log_dt, ssm_b_re, ssm_b_im,
               ssm_c_re, ssm_c_im, ssm_d, ssm_w_glu, ssm_b_glu, lambda_q1, lambda_k1, lambda_q2,
               lambda_k2, attn_subln_g, w_branch_ssm, w_branch_attn, w_gate, b_gate, w_out,
               norm_ffn_g, w_router, b_router, w_gate_up, b_gate_up, w_down, b_down)
    xp, xs = x_prompt, x_sample
    outs_p, outs_s = [], []
    for l in range(depth):
        P = _layer_params({nm: a[l] for nm, a in zip(names, stacked)})
        lambda_init = 0.8 - 0.6 * math.exp(-0.3 * l)
        last = l == depth - 1
        xp, kp, vp, srp, sip = _layer(xp, pos_prompt, zero_state, zero_state, None, P,
                                      lambda_init, g_final, last)
        xs, ks, vs, srs, sis = _layer(xs, pos_sample, state_ssm_re[l], state_ssm_im[l],
                                      (cache_k, cache_v, page_table, l), P,
                                      lambda_init, g_final, last)
        outs_p.append((kp, vp, srp, sip))
        outs_s.append((ks, vs, srs, sis))
    stack = lambda outs, i: jnp.stack([o[i] for o in outs])
    return (xp, xs, stack(outs_p, 0), stack(outs_p, 1), stack(outs_p, 2), stack(outs_p, 3),
            stack(outs_s, 0), stack(outs_s, 1), stack(outs_s, 2), stack(outs_s, 3))
```

```python
import functools
import math

import jax
import jax.numpy as jnp
from jax import lax
from jax.experimental import pallas as pl
from jax.experimental.pallas import tpu as pltpu

F32 = jnp.float32
BF16 = jnp.bfloat16

NORM_EPS = 1e-6
ROPE_THETA = 10000.0
NEG_INF = -1e30
TOP_K = 4
SWIGLU_LIMIT = 7.0
SWIGLU_ALPHA = 1.702
LANES = 128
MOE_BLOCK_ROWS = 256
PAIR_COLS = 256
ATTN_KEY_CHUNK = 1024
VMEM_LIMIT = 56 * 1024 * 1024


def _cparams(sem):
    return pltpu.CompilerParams(dimension_semantics=sem, vmem_limit_bytes=VMEM_LIMIT)


def _dot(a, b):
    return jnp.dot(a, b, preferred_element_type=F32)


def _dot_nt(a, b):
    return lax.dot_general(a, b, (((1,), (1,)), ((), ())), preferred_element_type=F32)


def _rms(x, g):
    return x * lax.rsqrt(jnp.mean(x * x, axis=-1, keepdims=True) + NORM_EPS) * g


def _proj_kernel(x_ref, g_ref, win_ref, wgate_ref, bgate_ref, cos_ref, sin_ref,
                 u_ref, q_ref, k_ref, v_ref, kb_ref, vb_ref, gates_ref, *, half, scale):
    hb = _rms(x_ref[...], g_ref[...]).astype(BF16)
    proj = _dot(hb, win_ref[...])
    w = u_ref.shape[-1]
    u_ref[...] = proj[:, :w]
    cos = cos_ref[...]
    sin = sin_ref[...]
    lane = lax.broadcasted_iota(jnp.int32, cos.shape, 1)
    lo = (lane % (2 * half)) < half

    def rot(xh):
        fwd = pltpu.roll(xh, LANES - half, axis=1)
        bwd = pltpu.roll(xh, half, axis=1)
        return xh * cos + jnp.where(lo, fwd, bwd) * sin

    nh = w // LANES
    tm = x_ref.shape[0]
    for hh in range(nh):
        sl = slice(hh * LANES, (hh + 1) * LANES)
        qh = rot(proj[:, w + hh * LANES:w + (hh + 1) * LANES])
        q_ref[:, sl] = (qh * scale).astype(BF16)
        kh = rot(proj[:, 2 * w + hh * LANES:2 * w + (hh + 1) * LANES])
        k_ref[pl.ds(hh, tm, stride=nh), :] = kh
        kb_ref[:, sl] = kh.astype(BF16)
        v_ref[pl.ds(hh, tm, stride=nh), :] = proj[:, 3 * w + hh * LANES:3 * w + (hh + 1) * LANES]
    vb_ref[...] = proj[:, 3 * w:].astype(BF16)
    gl = _dot(hb, wgate_ref[...]) + bgate_ref[...]
    gates_ref[...] = jax.nn.sigmoid(gl).astype(BF16)


def _proj(x2d, n_batch, t_len, tm, g, win_b, wgate_b, bgate, cos_t, sin_t, half, scale):
    n, d = x2d.shape
    w = win_b.shape[1] // 4
    nh = w // LANES
    nt = t_len // tm
    row = lambda b, t: (b * nt + t, 0)
    const = lambda b, t: (0, 0)
    outs = pl.pallas_call(
        functools.partial(_proj_kernel, half=half, scale=scale),
        grid=(n_batch, nt),
        in_specs=[pl.BlockSpec((tm, d), row),
                  pl.BlockSpec((1, d), const),
                  pl.BlockSpec(win_b.shape, const),
                  pl.BlockSpec(wgate_b.shape, const),
                  pl.BlockSpec((1, wgate_b.shape[1]), const),
                  pl.BlockSpec((tm, LANES), lambda b, t: (t, 0)),
                  pl.BlockSpec((tm, LANES), lambda b, t: (t, 0))],
        out_specs=[pl.BlockSpec((tm, w), lambda b, t: (t, b)),
                   pl.BlockSpec((tm, w), row),
                   pl.BlockSpec((tm * nh, LANES), row),
                   pl.BlockSpec((tm * nh, LANES), row),
                   pl.BlockSpec((tm, w), row),
                   pl.BlockSpec((tm, w), row),
                   pl.BlockSpec((tm, 2 * d), row)],
        out_shape=[jax.ShapeDtypeStruct((t_len, n_batch * w), F32),
                   jax.ShapeDtypeStruct((n, w), BF16),
                   jax.ShapeDtypeStruct((n * nh, LANES), F32),
                   jax.ShapeDtypeStruct((n * nh, LANES), F32),
                   jax.ShapeDtypeStruct((n, w), BF16),
                   jax.ShapeDtypeStruct((n, w), BF16),
                   jax.ShapeDtypeStruct((n, 2 * d), BF16)],
        compiler_params=_cparams(("parallel", "parallel")),
        name="proj",
    )(x2d, g, win_b, wgate_b, bgate, cos_t, sin_t)
    return outs


def _s5_kernel(u_ref, x0_ref, lre_ref, lim_ref, ldt_ref, bre_ref, bim_ref, cre_ref, cim_ref,
               d_ref, wglu_ref, bglu_ref, y_ref, st_ref,
               zx_ref, wz_ref, are_ref, aim_ref, state_ref, *, nb, tc, ns, sw):
    step = pl.program_id(0)

    @pl.when(step == 0)
    def _init():
        lre = lre_ref[...]
        lim = lim_ref[...]
        dt = jnp.exp(ldt_ref[...])
        mag = jnp.exp(lre * dt)
        are = mag * jnp.cos(lim * dt)
        aim = mag * jnp.sin(lim * dt)
        den = lre * lre + lim * lim
        fre = ((are - 1.0) * lre + aim * lim) / den
        fim = (aim * lre - (are - 1.0) * lim) / den
        are_ref[...] = are
        aim_ref[...] = aim
        for i in range(ns):
            fr = fre[:, i * sw:(i + 1) * sw]
            fi = fim[:, i * sw:(i + 1) * sw]
            br = bre_ref[i]
            bi = bim_ref[i]
            wz_ref[i, :, :sw] = (fr * br - fi * bi).astype(BF16)
            wz_ref[i, :, sw:] = (fr * bi + fi * br).astype(BF16)
        state_ref[...] = x0_ref[...]

    ub = u_ref[...]
    ubb = ub.astype(BF16)
    for i in range(ns):
        zx_ref[:, 2 * sw * i:2 * sw * (i + 1)] = _dot(ubb[:, LANES * i:LANES * (i + 1)], wz_ref[i])

    for i in range(ns):
        c0 = 2 * sw * i
        arb = jnp.broadcast_to(are_ref[:, i * sw:(i + 1) * sw], (nb, sw))
        aib = jnp.broadcast_to(aim_ref[:, i * sw:(i + 1) * sw], (nb, sw))

        def body(t, carry, c0=c0, arb=arb, aib=aib):
            xr, xi = carry
            r0 = pl.multiple_of(t * nb, nb)
            zr = zx_ref[pl.ds(r0, nb), c0:c0 + sw]
            zi = zx_ref[pl.ds(r0, nb), c0 + sw:c0 + 2 * sw]
            nxr = arb * xr - aib * xi + zr
            nxi = arb * xi + aib * xr + zi
            zx_ref[pl.ds(r0, nb), c0:c0 + sw] = nxr
            zx_ref[pl.ds(r0, nb), c0 + sw:c0 + 2 * sw] = nxi
            return nxr, nxi

        xr, xi = lax.fori_loop(0, tc, body,
                               (state_ref[:, c0:c0 + sw], state_ref[:, c0 + sw:c0 + 2 * sw]))
        state_ref[:, c0:c0 + sw] = xr
        state_ref[:, c0 + sw:c0 + 2 * sw] = xi

    ys = []
    for i in range(ns):
        c0 = 2 * sw * i
        xr = zx_ref[:, c0:c0 + sw].astype(BF16)
        xi = zx_ref[:, c0 + sw:c0 + 2 * sw].astype(BF16)
        ys.append(_dot(xr, cre_ref[i]) - _dot(xi, cim_ref[i]))
    y = jnp.concatenate(ys, axis=1) + d_ref[...] * ub
    y = jax.nn.gelu(y)
    gl = _dot(y.astype(BF16), wglu_ref[...]) + bglu_ref[...]
    y_ref[...] = (y * jax.nn.sigmoid(gl)).astype(BF16)

    @pl.when(step == pl.num_programs(0) - 1)
    def _fin():
        st_ref[...] = state_ref[...]


def _s5_layout(lp):
    g, p, h = lp['ssm_b_re'].shape
    gps = LANES // h
    ns = g // gps
    eye = jnp.eye(gps, dtype=jnp.bool_)

    def bd_b(b):
        bb = b.reshape(ns, gps, p, h).transpose(0, 1, 3, 2)
        out = jnp.where(eye[None, :, None, :, None], bb[:, :, :, None, :], 0.0)
        return out.reshape(ns, gps * h, gps * p)

    def bd_c(c):
        cc = c.reshape(ns, gps, h, p).transpose(0, 1, 3, 2)
        out = jnp.where(eye[None, :, None, :, None], cc[:, :, :, None, :], 0.0)
        return out.reshape(ns, gps * p, gps * h)

    return dict(
        lre=lp['ssm_lambda_re'].reshape(1, g * p),
        lim=lp['ssm_lambda_im'].reshape(1, g * p),
        ldt=jnp.repeat(lp['ssm_log_dt'], p).reshape(1, g * p),
        bre=bd_b(lp['ssm_b_re']), bim=bd_b(lp['ssm_b_im']),
        cre=bd_c(lp['ssm_c_re']).astype(BF16), cim=bd_c(lp['ssm_c_im']).astype(BF16),
        d=lp['ssm_d'].reshape(1, g * h),
        wglu=lp['ssm_w_glu'].astype(BF16), bglu=lp['ssm_b_glu'].reshape(1, -1),
        ns=ns, sw=gps * p)


def _state_to_cols(re, im, ns):
    b = re.shape[0]
    return jnp.stack([re.reshape(b, ns, -1), im.reshape(b, ns, -1)], axis=2).reshape(b, -1)


def _cols_to_state(st, ns, g, p):
    b = st.shape[0]
    s = st.reshape(b, ns, 2, -1)
    return s[:, :, 0].reshape(b, g, p), s[:, :, 1].reshape(b, g, p)


def _s5(u_tb, x0, sp, nb, t_len, tc):
    rows = nb * tc
    w = u_tb.shape[1]
    ns, sw = sp['ns'], sp['sw']
    ncol = ns * 2 * sw
    full = lambda a: pl.BlockSpec(a.shape, lambda i: (0,) * a.ndim)
    args = (u_tb, x0, sp['lre'], sp['lim'], sp['ldt'], sp['bre'], sp['bim'], sp['cre'], sp['cim'],
            sp['d'], sp['wglu'], sp['bglu'])
    return pl.pallas_call(
        functools.partial(_s5_kernel, nb=nb, tc=tc, ns=ns, sw=sw),
        grid=(t_len // tc,),
        in_specs=[pl.BlockSpec((rows, w), lambda i: (i, 0))] + [full(a) for a in args[1:]],
        out_specs=[pl.BlockSpec((rows, w), lambda i: (i, 0)),
                   pl.BlockSpec((nb, ncol), lambda i: (0, 0))],
        out_shape=[jax.ShapeDtypeStruct((t_len * nb, w), BF16),
                   jax.ShapeDtypeStruct((nb, ncol), F32)],
        scratch_shapes=[pltpu.VMEM((rows, ncol), F32),
                        pltpu.VMEM((ns, LANES, 2 * sw), BF16),
                        pltpu.VMEM((1, ns * sw), F32),
                        pltpu.VMEM((1, ns * sw), F32),
                        pltpu.VMEM((nb, ncol), F32)],
        compiler_params=_cparams(("arbitrary",)),
        name="s5",
    )(*args)


def _diff_lambda(lq1, lk1, lq2, lk2, lambda_init):
    return (jnp.exp(jnp.sum(lq1[...] * lk1[...], keepdims=True))
            - jnp.exp(jnp.sum(lq2[...] * lk2[...], keepdims=True)) + lambda_init)


def _attn_kernel(q_ref, k_ref, v_ref, lq1, lk1, lq2, lk2, g_ref, o_ref, *, tq, hd, lambda_init):
    t_len = q_ref.shape[0]
    lam = _diff_lambda(lq1, lk1, lq2, lk2, lambda_init)
    lo = lax.broadcasted_iota(jnp.int32, (tq, LANES), 1) < hd
    max_blocks = max(ATTN_KEY_CHUNK // tq, 1)

    for qi in range(t_len // tq):
        q = q_ref[qi * tq:(qi + 1) * tq, :]
        zero = jnp.zeros_like(q)
        qs = jnp.concatenate([jnp.where(lo, q, zero), jnp.where(lo, zero, q)], axis=0)
        m = jnp.full((1, 2 * tq), NEG_INF, F32)
        l = jnp.zeros((1, 2 * tq), F32)
        acc = jnp.zeros((LANES, 2 * tq), F32)
        n_blocks = qi + 1
        n_chunks = -(-n_blocks // max_blocks)
        k0 = 0
        for ci in range(n_chunks):
            ksz = (n_blocks // n_chunks + (1 if ci < n_blocks % n_chunks else 0)) * tq
            s = _dot_nt(k_ref[k0:k0 + ksz, :], qs)
            if ci == n_chunks - 1:
                kpos = k0 + lax.broadcasted_iota(jnp.int32, (ksz, 2 * tq), 0)
                qc = lax.broadcasted_iota(jnp.int32, (ksz, 2 * tq), 1)
                s = jnp.where(kpos <= qi * tq + jnp.where(qc >= tq, qc - tq, qc), s, NEG_INF)
            m_new = jnp.maximum(m, jnp.max(s, axis=0, keepdims=True))
            alpha = jnp.exp(m - m_new)
            p = jnp.exp(s - m_new)
            l = alpha * l + jnp.sum(p, axis=0, keepdims=True)
            pv = lax.dot_general(v_ref[k0:k0 + ksz, :], p.astype(BF16), (((0,), (0,)), ((), ())),
                                 preferred_element_type=F32)
            acc = alpha * acc + pv
            m = m_new
            k0 += ksz
        o_t = (acc[:, :tq] / l[:, :tq]) - lam * (acc[:, tq:] / l[:, tq:])
        ms = jnp.mean(o_t * o_t, axis=0, keepdims=True)
        o_t = o_t * lax.rsqrt(ms + NORM_EPS) * g_ref[...] * (1.0 - lambda_init)
        o_ref[qi * tq:(qi + 1) * tq, :] = o_t.T.astype(BF16)


def _attn(qb, kb, vb, lam_params, g, n_batch, t_len, tq, hd, lambda_init):
    n, w = qb.shape
    nh = w // LANES
    blk = pl.BlockSpec((t_len, LANES), lambda b, h: (b, h))
    small = lambda a: pl.BlockSpec(a.shape, lambda b, h: (0, 0))
    g_cols = jnp.broadcast_to(g.reshape(LANES, 1), (LANES, tq))
    return pl.pallas_call(
        functools.partial(_attn_kernel, tq=tq, hd=hd, lambda_init=lambda_init),
        grid=(n_batch, nh),
        in_specs=[blk, blk, blk] + [small(a) for a in lam_params] + [small(g_cols)],
        out_specs=blk,
        out_shape=jax.ShapeDtypeStruct((n, w), BF16),
        compiler_params=_cparams(("parallel", "parallel")),
        name="attn_prompt",
    )(qb, kb, vb, *lam_params, g_cols)


def _sattn_kernel(pt_ref, q_ref, kown_ref, vown_ref, *rest, npg, prow, dq, nh, lambda_init):
    k_refs = rest[:npg]
    v_refs = rest[npg:2 * npg]
    lq1, lk1, lq2, lk2, g_ref, o_ref, kb_ref, vb_ref, m_ref, l_ref, acc_ref = rest[2 * npg:]
    j = pl.program_id(1)
    nq = q_ref.shape[1]
    rq = nq // nh
    page = prow // nh
    q = [q_ref[0, h * rq:(h + 1) * rq, :].astype(BF16) for h in range(nh)]

    @pl.when(j == 0)
    def _init():
        m_ref[...] = jnp.full(m_ref.shape, NEG_INF, F32)
        l_ref[...] = jnp.zeros(l_ref.shape, F32)
        acc_ref[...] = jnp.zeros(acc_ref.shape, F32)

    def update(s, values):
        m_prev = m_ref[...]
        m_new = jnp.maximum(m_prev, jnp.max(s, axis=1, keepdims=True))
        alpha = jnp.exp(m_prev - m_new)
        p = jnp.exp(s - m_new)
        l_ref[...] = alpha * l_ref[...] + jnp.sum(p, axis=1, keepdims=True)
        pv = jnp.concatenate([_dot(p[h * rq:(h + 1) * rq].astype(BF16), values[h])
                              for h in range(nh)], axis=0)
        acc_ref[...] = alpha * acc_ref[...] + pv
        m_ref[...] = m_new

    for p_ in range(npg):
        for h in range(nh):
            rows = pl.ds(h, page, stride=nh)
            kb_ref[h, p_ * page:(p_ + 1) * page, :] = k_refs[p_][0, rows, :].astype(BF16)
            vb_ref[h, p_ * page:(p_ + 1) * page, :] = v_refs[p_][0, rows, :].astype(BF16)
    s = jnp.concatenate([_dot_nt(q[h], kb_ref[h]) for h in range(nh)], axis=0)
    update(s, [vb_ref[h] for h in range(nh)])

    @pl.when(j == pl.num_programs(1) - 1)
    def _fin():
        k_own = [kown_ref[0, pl.ds(h, dq, stride=nh), :].astype(BF16) for h in range(nh)]
        v_own = [vown_ref[0, pl.ds(h, dq, stride=nh), :].astype(BF16) for h in range(nh)]
        s_own = jnp.concatenate([_dot_nt(q[h], k_own[h]) for h in range(nh)], axis=0)
        r = lax.broadcasted_iota(jnp.int32, (nq, dq), 0)
        c = lax.broadcasted_iota(jnp.int32, (nq, dq), 1)
        update(jnp.where(c <= r % dq, s_own, NEG_INF), v_own)
        lam = _diff_lambda(lq1, lk1, lq2, lk2, lambda_init)
        o_all = acc_ref[...] / l_ref[...]
        for h in range(nh):
            o = o_all[h * rq:h * rq + dq] - lam * o_all[h * rq + dq:(h + 1) * rq]
            o_ref[0, h * dq:(h + 1) * dq, :] = (_rms(o, g_ref[...])
                                                * (1.0 - lambda_init)).astype(BF16)


def _sattn(page_table, q_rows, k_own, v_own, cache_k, cache_v, layer, lam_params, g, npg, dq, nh,
           lambda_init):
    nb, nq, _ = q_rows.shape
    prow = cache_k.shape[2]
    n_pages = page_table.shape[1]
    nsteps = n_pages // npg

    def page_spec(p_):
        return pl.BlockSpec((None, 1, prow, LANES),
                            lambda b, j, pt: (layer, pt[b, j * npg + p_], 0, 0))

    per_b = lambda a: pl.BlockSpec((1,) + a.shape[1:], lambda b, j, pt: (b, 0, 0))
    small = lambda a: pl.BlockSpec(a.shape, lambda b, j, pt: (0, 0))
    grid_spec = pltpu.PrefetchScalarGridSpec(
        num_scalar_prefetch=1,
        grid=(nb, nsteps),
        in_specs=([per_b(q_rows), per_b(k_own), per_b(v_own)]
                  + [page_spec(p_) for p_ in range(npg)]
                  + [page_spec(p_) for p_ in range(npg)]
                  + [small(a) for a in lam_params] + [small(g)]),
        out_specs=pl.BlockSpec((1, nq // 2, LANES), lambda b, j, pt: (b, 0, 0)),
        scratch_shapes=[pltpu.VMEM((nh, npg * prow // nh, LANES), BF16),
                        pltpu.VMEM((nh, npg * prow // nh, LANES), BF16),
                        pltpu.VMEM((nq, 1), F32),
                        pltpu.VMEM((nq, 1), F32),
                        pltpu.VMEM((nq, LANES), F32)])
    return pl.pallas_call(
        functools.partial(_sattn_kernel, npg=npg, prow=prow, dq=dq, nh=nh,
                          lambda_init=lambda_init),
        grid_spec=grid_spec,
        out_shape=jax.ShapeDtypeStruct((nb, nq // 2, LANES), BF16),
        compiler_params=_cparams(("parallel", "arbitrary")),
        name="attn_sample",
    )(page_table, q_rows, k_own, v_own, *([cache_k] * npg), *([cache_v] * npg), *lam_params, g)


def _merge_kernel(x_ref, gates_ref, y_ref, o_ref, wbs_ref, wba_ref, wout_ref, gffn_ref,
                  wr_ref, br_ref, x1_ref, h2_ref, ri_ref, rf_ref, cnt_ref, run_ref, *, ne):
    first = jnp.logical_and(pl.program_id(0) == 0, pl.program_id(1) == 0)

    @pl.when(first)
    def _init():
        run_ref[...] = jnp.zeros(run_ref.shape, F32)

    tm, d = x_ref.shape
    a = _dot(y_ref[...], wbs_ref[...])
    b = _dot(o_ref[...], wba_ref[...])
    merged = gates_ref[:, :d].astype(F32) * a + gates_ref[:, d:].astype(F32) * b
    x1 = x_ref[...] + _dot(merged.astype(BF16), wout_ref[...])
    x1_ref[...] = x1
    h2b = _rms(x1, gffn_ref[...]).astype(BF16)
    bits = lax.bitcast_convert_type(h2b.astype(F32), jnp.uint32)
    h2_ref[...] = bits[:, d // 2:] | (bits[:, :d // 2] >> 16)

    logits = _dot(h2b, wr_ref[...]) + br_ref[...]
    lane = lax.broadcasted_iota(jnp.int32, logits.shape, 1)
    lane_f = lane.astype(F32)
    cur = jnp.where(lane < ne, logits, -jnp.inf)
    vals, idxs, hots = [], [], []
    for _ in range(TOP_K):
        mk = jnp.max(cur, axis=1, keepdims=True)
        ik = jnp.min(jnp.where(cur == mk, lane_f, float(LANES)), axis=1, keepdims=True)
        hot = lane_f == ik
        cur = jnp.where(hot, -jnp.inf, cur)
        vals.append(mk)
        idxs.append(ik.astype(jnp.int32))
        hots.append(hot)
    exps = [jnp.exp(v - vals[0]) for v in vals]
    denom = sum(exps[1:], exps[0])
    chosen = functools.reduce(jnp.logical_or, hots)
    cnt = jnp.where(chosen, 1.0, 0.0)
    r = lax.broadcasted_iota(jnp.int32, (tm, tm), 0)
    c = lax.broadcasted_iota(jnp.int32, (tm, tm), 1)
    before = _dot((c < r).astype(BF16), cnt.astype(BF16)) + run_ref[...]
    ri = jnp.zeros(logits.shape, jnp.int32)
    rf = jnp.zeros(logits.shape, F32)
    for k in range(TOP_K):
        rank = jnp.sum(jnp.where(hots[k], before, 0.0), axis=1, keepdims=True).astype(jnp.int32)
        ri = jnp.where(lane == k, idxs[k], ri)
        ri = jnp.where(lane == TOP_K + k, rank, ri)
        rf = jnp.where(lane == k, exps[k] / denom, rf)
    ri_ref[...] = ri
    rf_ref[...] = rf
    run_ref[...] = run_ref[...] + jnp.sum(cnt, axis=0, keepdims=True)
    cnt_ref[...] = run_ref[...]


def _merge(x2d, gates, y_tb, o, n_batch, t_len, tm, wbs, wba, wout, gffn, wr, br, ne):
    n, d = x2d.shape
    w = o.shape[1]
    nt = t_len // tm
    row = lambda b, t: (b * nt + t, 0)
    const = lambda b, t: (0, 0)
    return pl.pallas_call(
        functools.partial(_merge_kernel, ne=ne),
        grid=(n_batch, nt),
        in_specs=[pl.BlockSpec((tm, d), row),
                  pl.BlockSpec((tm, 2 * d), row),
                  pl.BlockSpec((tm, w), lambda b, t: (t, b)),
                  pl.BlockSpec((tm, w), row),
                  pl.BlockSpec(wbs.shape, const),
                  pl.BlockSpec(wba.shape, const),
                  pl.BlockSpec(wout.shape, const),
                  pl.BlockSpec((1, d), const),
                  pl.BlockSpec(wr.shape, const),
                  pl.BlockSpec((1, LANES), const)],
        out_specs=[pl.BlockSpec((tm, d), row),
                   pl.BlockSpec((tm, d // 2), row),
                   pl.BlockSpec((tm, LANES), row),
                   pl.BlockSpec((tm, LANES), row),
                   pl.BlockSpec((1, LANES), const)],
        out_shape=[jax.ShapeDtypeStruct((n, d), F32),
                   jax.ShapeDtypeStruct((n, d // 2), jnp.uint32),
                   jax.ShapeDtypeStruct((n, LANES), jnp.int32),
                   jax.ShapeDtypeStruct((n, LANES), F32),
                   jax.ShapeDtypeStruct((1, LANES), F32)],
        scratch_shapes=[pltpu.VMEM((1, LANES), F32)],
        compiler_params=_cparams(("arbitrary", "arbitrary")),
        name="merge",
    )(x2d, gates, y_tb, o, wbs, wba, wout, gffn, wr, br)


def _moe_kernel(be_ref, nu_ref, x_ref, wgu_ref, bgu_ref, wd_ref, bd_ref, o_ref, wgu_s, wd_s):
    i = pl.program_id(0)
    active = i < nu_ref[0]
    fresh = jnp.logical_or(i == 0, be_ref[i] != be_ref[jnp.maximum(i - 1, 0)])
    pc = PAIR_COLS
    hc = pc // 2
    n_chunk = wgu_ref.shape[2] // pc

    @pl.when(jnp.logical_and(active, fresh))
    def _stage_weights():
        r = lax.broadcasted_iota(jnp.int32, (pc, pc), 0)
        c = lax.broadcasted_iota(jnp.int32, (pc, pc), 1)
        perm = (r == jnp.where(c < hc, 2 * c, 2 * (c - hc) + 1)).astype(BF16)
        for j in range(n_chunk):
            wj = wgu_ref[0, :, j * pc:(j + 1) * pc].astype(BF16)
            wgu_s[:, j * pc:(j + 1) * pc] = _dot(wj, perm).astype(BF16)
        wd_s[...] = wd_ref[0].astype(BF16)

    @pl.when(active)
    def _compute():
        words = x_ref[...]
        x = jnp.concatenate(
            [lax.bitcast_convert_type(words << 16, F32),
             lax.bitcast_convert_type(words & jnp.uint32(0xFFFF0000), F32)], axis=1).astype(BF16)
        gu = _dot(x, wgu_s[...]) + bgu_ref[0]
        acts = []
        for j in range(n_chunk):
            g_lin = jnp.minimum(gu[:, j * pc:j * pc + hc], SWIGLU_LIMIT)
            up = jnp.clip(gu[:, j * pc + hc:(j + 1) * pc], -SWIGLU_LIMIT, SWIGLU_LIMIT)
            acts.append(((up + 1.0) * (g_lin * jax.nn.sigmoid(SWIGLU_ALPHA * g_lin))).astype(BF16))
        o_ref[...] = _dot(jnp.concatenate(acts, axis=1), wd_s[...]) + bd_ref[0]

    @pl.when(jnp.logical_not(active))
    def _skip():
        o_ref[...] = jnp.zeros(o_ref.shape, o_ref.dtype)


def _moe_gemm(xs, blk_expert, n_used, wgu, bgu, wd, bd):
    n_rows = xs.shape[0]
    d = wd.shape[2]
    bm = MOE_BLOCK_ROWS
    wspec = lambda a: pl.BlockSpec((1,) + a.shape[1:], lambda i, be, nu: (be[i], 0, 0))
    grid_spec = pltpu.PrefetchScalarGridSpec(
        num_scalar_prefetch=2,
        grid=(n_rows // bm,),
        in_specs=[pl.BlockSpec((bm, xs.shape[1]), lambda i, be, nu: (i, 0)),
                  wspec(wgu), wspec(bgu), wspec(wd), wspec(bd)],
        out_specs=pl.BlockSpec((bm, d), lambda i, be, nu: (i, 0)),
        scratch_shapes=[pltpu.VMEM(wgu.shape[1:], BF16),
                        pltpu.VMEM(wd.shape[1:], BF16)])
    return pl.pallas_call(
        _moe_kernel,
        grid_spec=grid_spec,
        out_shape=jax.ShapeDtypeStruct((n_rows, d), F32),
        compiler_params=_cparams(("arbitrary",)),
        name="moe_gemm",
    )(blk_expert, n_used, xs, wgu, bgu, wd, bd)


def _combine_kernel(x_ref, *rest, final_norm):
    yg_refs = rest[:TOP_K]
    gate_ref, g_ref, o_ref = rest[TOP_K:]
    acc = x_ref[...]
    gate = gate_ref[...]
    for k in range(TOP_K):
        acc = acc + gate[:, k:k + 1] * yg_refs[k][...]
    o_ref[...] = _rms(acc, g_ref[...]) if final_norm else acc


def _combine(x1, ygs, gate, g, tm, final_norm):
    n, d = x1.shape
    row = pl.BlockSpec((tm, d), lambda i: (i, 0))
    return pl.pallas_call(
        functools.partial(_combine_kernel, final_norm=final_norm),
        grid=(n // tm,),
        in_specs=[row] + [row] * TOP_K + [pl.BlockSpec((tm, TOP_K), lambda i: (i, 0)),
                                          pl.BlockSpec((1, d), lambda i: (0, 0))],
        out_specs=row,
        out_shape=jax.ShapeDtypeStruct((n, d), F32),
        compiler_params=_cparams(("parallel",)),
        name="combine",
    )(x1, *ygs, gate, g)


def _moe_dispatch(h2p, route_i, count_row, ne):
    n = h2p.shape[0]
    bm = MOE_BLOCK_ROWS
    m = n * TOP_K
    e_idx = route_i[:, :TOP_K]
    rank = route_i[:, TOP_K:2 * TOP_K]
    counts = count_row[0, :ne].astype(jnp.int32)
    start = jnp.cumsum(counts) - counts
    padded = (counts + bm - 1) // bm * bm
    pend = jnp.cumsum(padded)
    pstart = pend - padded
    dest = pstart[e_idx] + rank
    n_rows = (m + bm - 1) // bm * bm + ne * bm
    blk_start = jnp.arange(n_rows // bm, dtype=jnp.int32) * bm
    blk_expert = jnp.minimum(
        jnp.sum((blk_start[:, None] >= pend[None, :]).astype(jnp.int32), axis=1), ne - 1)
    n_used = (pend[-1] // bm).astype(jnp.int32).reshape(1)
    tok = jnp.broadcast_to(jnp.arange(n, dtype=jnp.int32)[:, None], (n, TOP_K))
    _, tok_sorted = lax.sort_key_val(dest.reshape(m), tok.reshape(m))
    blk_shift = (pstart - start)[blk_expert]
    src = jnp.arange(n_rows, dtype=jnp.int32) - jnp.repeat(blk_shift, bm)
    row_tok = tok_sorted[jnp.clip(src, 0, m - 1)]
    return dict(xs=h2p[row_tok], blk_expert=blk_expert, n_used=n_used, dest=dest)


def _moe_apply(x1, disp, route_f, mp, g_final, final_norm):
    ys = _moe_gemm(disp['xs'], disp['blk_expert'], disp['n_used'],
                   mp['wgu'], mp['bgu'], mp['wd'], mp['bd'])
    ygs = [ys[disp['dest'][:, k]] for k in range(TOP_K)]
    tm = min(256, x1.shape[0])
    return _combine(x1, ygs, route_f[:, :TOP_K], g_final, tm, final_norm)


def _rope_tables(pos, hd):
    half = hd // 2
    inv_freq = ROPE_THETA ** (-jnp.arange(half, dtype=F32) / half)
    ang = pos.astype(F32)[:, None] * inv_freq[None, :]
    cos = jnp.cos(ang)
    sin = jnp.sin(ang)
    reps = LANES // hd
    cos_t = jnp.tile(jnp.concatenate([cos, cos], axis=1), (1, reps))
    sin_t = jnp.tile(jnp.concatenate([-sin, sin], axis=1), (1, reps))
    return cos_t, sin_t


def _layer_params(lp):
    d = lp['w_in'].shape[0]
    ne = lp['w_router'].shape[1]
    wr = jnp.zeros((d, LANES), F32).at[:, :ne].set(lp['w_router']).astype(BF16)
    br = jnp.zeros((1, LANES), F32).at[0, :ne].set(lp['b_router'])
    return dict(
        g_mix=lp['norm_mix_g'].reshape(1, d),
        win=lp['w_in'].astype(BF16),
        wgate=lp['w_gate'].astype(BF16),
        bgate=lp['b_gate'].reshape(1, -1),
        s5=_s5_layout(lp),
        lam=[lp[k].reshape(1, -1) for k in ('lambda_q1', 'lambda_k1', 'lambda_q2', 'lambda_k2')],
        g_sub=lp['attn_subln_g'].reshape(1, -1),
        wbs=lp['w_branch_ssm'].astype(BF16),
        wba=lp['w_branch_attn'].astype(BF16),
        wout=lp['w_out'].astype(BF16),
        g_ffn=lp['norm_ffn_g'].reshape(1, d),
        wr=wr, br=br,
        moe=dict(ne=ne,
                 wgu=lp['w_gate_up'],
                 bgu=lp['b_gate_up'].reshape(ne, -1, PAIR_COLS // 2, 2).transpose(0, 1, 3, 2)
                 .reshape(ne, 1, -1),
                 wd=lp['w_down'],
                 bd=lp['b_down'][:, None, :]))


def _query_rows(qb, bsz, dq, nh, hd):
    qt = qb.astype(F32).reshape(bsz, dq, nh, 2, hd).transpose(0, 2, 3, 1, 4)
    ec = jnp.eye(2, dtype=jnp.bool_)[None, None, :, None, :, None]
    out = jnp.where(ec, qt[:, :, :, :, None, :], 0.0)
    return out.reshape(bsz, nh * 2 * dq, 2 * hd)


def _layer(x, pos, x0_re, x0_im, past, P, lambda_init):
    bsz, t_len, d = x.shape
    n = bsz * t_len
    x2d = x.reshape(n, d)
    sp = P['s5']
    g, p = x0_re.shape[1], x0_re.shape[2]
    w = P['win'].shape[1] // 4
    nh = w // LANES
    hd = LANES // 2
    cos_t, sin_t = _rope_tables(pos, hd)
    x0 = _state_to_cols(x0_re.astype(F32), x0_im.astype(F32), sp['ns'])
    if past is None:
        n_batch, rows_t, tm = bsz, t_len, min(256, t_len)
    else:
        n_batch, rows_t, tm = 1, n, n
        cos_t = jnp.tile(cos_t, (bsz, 1))
        sin_t = jnp.tile(sin_t, (bsz, 1))
    u, qb, k, v, kb, vb, gates = _proj(x2d, n_batch, rows_t, tm, P['g_mix'], P['win'], P['wgate'],
                                       P['bgate'], cos_t, sin_t, hd // 2, hd ** -0.5)
    if past is None:
        u_tb = u.reshape(t_len * bsz, w)
        y_tb, st = _s5(u_tb, x0, sp, bsz, t_len, min(64, t_len))
        y_in = y_tb.reshape(t_len, bsz * w)
        o = _attn(qb, kb, vb, P['lam'], P['g_sub'], bsz, t_len, min(256, t_len), hd, lambda_init)
    else:
        cache_k, cache_v, page_table, layer = past
        u_tb = u.reshape(bsz, t_len, w).transpose(1, 0, 2).reshape(t_len * bsz, w)
        y_tb, st = _s5(u_tb, x0, sp, bsz, t_len, t_len)
        y_in = y_tb.reshape(t_len, bsz, w).transpose(1, 0, 2).reshape(n, w)
        npg = math.gcd(8, page_table.shape[1])
        cshape = cache_k.shape[:2] + (cache_k.shape[2] * nh, LANES)
        o = _sattn(page_table, _query_rows(qb, bsz, t_len, nh, hd),
                   k.reshape(bsz, t_len * nh, LANES), v.reshape(bsz, t_len * nh, LANES),
                   cache_k.reshape(cshape), cache_v.reshape(cshape), layer,
                   P['lam'], P['g_sub'], npg, t_len, nh, lambda_init)
        o = o.reshape(bsz, nh, t_len, LANES).transpose(0, 2, 1, 3).reshape(n, w)
    x1, h2p, route_i, route_f, count_row = _merge(
        x2d, gates, y_in, o, n_batch, rows_t, tm, P['wbs'], P['wba'], P['wout'], P['g_ffn'],
        P['wr'], P['br'], P['moe']['ne'])
    disp = _moe_dispatch(h2p, route_i, count_row, P['moe']['ne'])
    st_re, st_im = _cols_to_state(st, sp['ns'], g, p)
    side = (k.reshape(bsz, t_len, nh, LANES), v.reshape(bsz, t_len, nh, LANES),
            st_re.astype(x.dtype), st_im.astype(x.dtype))
    return (x1, disp, route_f), side


def kernel(x_prompt, x_sample, cache_k, cache_v, state_ssm_re, state_ssm_im, page_table, norm_mix_g, w_in, ssm_lambda_re, ssm_lambda_im, ssm_log_dt, ssm_b_re, ssm_b_im, ssm_c_re, ssm_c_im, ssm_d, ssm_w_glu, ssm_b_glu, lambda_q1, lambda_k1, lambda_q2, lambda_k2, attn_subln_g, w_branch_ssm, w_branch_attn, w_gate, b_gate, w_out, norm_ffn_g, w_router, b_router, w_gate_up, b_gate_up, w_down, b_down, norm_final_g):
    depth = w_in.shape[0]
    past_len = page_table.shape[1] * cache_k.shape[2]
    pos_prompt = jnp.arange(x_prompt.shape[1], dtype=jnp.int32)
    pos_sample = past_len + jnp.arange(x_sample.shape[1], dtype=jnp.int32)
    g, p = state_ssm_re.shape[2], state_ssm_re.shape[3]
    zero_state = jnp.zeros((x_prompt.shape[0], g, p), F32)
    g_final = norm_final_g.reshape(1, -1)
    names = ('norm_mix_g', 'w_in', 'ssm_lambda_re', 'ssm_lambda_im', 'ssm_log_dt', 'ssm_b_re',
             'ssm_b_im', 'ssm_c_re', 'ssm_c_im', 'ssm_d', 'ssm_w_glu', 'ssm_b_glu', 'lambda_q1',
             'lambda_k1', 'lambda_q2', 'lambda_k2', 'attn_subln_g', 'w_branch_ssm',
             'w_branch_attn', 'w_gate', 'b_gate', 'w_out', 'norm_ffn_g', 'w_router', 'b_router',
             'w_gate_up', 'b_gate_up', 'w_down', 'b_down')
    stacked = (norm_mix_g, w_in, ssm_lambda_re, ssm_lambda_im, ssm_log_dt, ssm_b_re, ssm_b_im,
               ssm_c_re, ssm_c_im, ssm_d, ssm_w_glu, ssm_b_glu, lambda_q1, lambda_k1, lambda_q2,
               lambda_k2, attn_subln_g, w_branch_ssm, w_branch_attn, w_gate, b_gate, w_out,
               norm_ffn_g, w_router, b_router, w_gate_up, b_gate_up, w_down, b_down)
    xp, xs = x_prompt, x_sample
    outs_p, outs_s = [], []
    for l in range(depth):
        P = _layer_params({nm: a[l] for nm, a in zip(names, stacked)})
        lambda_init = 0.8 - 0.6 * math.exp(-0.3 * l)
        last = l == depth - 1
        moe_p, side_p = _layer(xp, pos_prompt, zero_state, zero_state, None, P, lambda_init)
        moe_s, side_s = _layer(xs, pos_sample, state_ssm_re[l], state_ssm_im[l],
                               (cache_k, cache_v, page_table, l), P, lambda_init)
        xp = _moe_apply(*moe_p, P['moe'], g_final, last).reshape(xp.shape)
        xs = _moe_apply(*moe_s, P['moe'], g_final, last).reshape(xs.shape)
        outs_p.append(side_p)
        outs_s.append(side_s)
    stack = lambda outs, i: jnp.stack([o[i] for o in outs])
    return (xp, xs, stack(outs_p, 0), stack(outs_p, 1), stack(outs_p, 2), stack(outs_p, 3),
            stack(outs_s, 0), stack(outs_s, 1), stack(outs_s, 2), stack(outs_s, 3))
```

```python
import functools
import math

import jax
import jax.numpy as jnp
from jax import lax
from jax.experimental import pallas as pl
from jax.experimental.pallas import tpu as pltpu

F32 = jnp.float32
BF16 = jnp.bfloat16

NORM_EPS = 1e-6
ROPE_THETA = 10000.0
NEG_INF = -1e30
TOP_K = 4
SWIGLU_LIMIT = 7.0
SWIGLU_ALPHA = 1.702
LANES = 128
MOE_BLOCK_ROWS = 256
PAIR_COLS = 256
ATTN_KEY_CHUNK = 1024
VMEM_LIMIT = 56 * 1024 * 1024


def _cparams(sem):
    return pltpu.CompilerParams(dimension_semantics=sem, vmem_limit_bytes=VMEM_LIMIT)


def _dot(a, b):
    return jnp.dot(a, b, preferred_element_type=F32)


def _dot_nt(a, b):
    return lax.dot_general(a, b, (((1,), (1,)), ((), ())), preferred_element_type=F32)


def _rms(x, g):
    return x * lax.rsqrt(jnp.mean(x * x, axis=-1, keepdims=True) + NORM_EPS) * g


def _proj_kernel(x_ref, g_ref, win_ref, wgate_ref, bgate_ref, cos_ref, sin_ref,
                 u_ref, q_ref, k_ref, v_ref, kb_ref, vb_ref, gates_ref, *, half, scale):
    hb = _rms(x_ref[...], g_ref[...]).astype(BF16)
    proj = _dot(hb, win_ref[...])
    w = u_ref.shape[-1]
    u_ref[...] = proj[:, :w]
    cos = cos_ref[...]
    sin = sin_ref[...]
    lane = lax.broadcasted_iota(jnp.int32, cos.shape, 1)
    lo = (lane % (2 * half)) < half

    def rot(xh):
        fwd = pltpu.roll(xh, LANES - half, axis=1)
        bwd = pltpu.roll(xh, half, axis=1)
        return xh * cos + jnp.where(lo, fwd, bwd) * sin

    nh = w // LANES
    tm = x_ref.shape[0]
    for hh in range(nh):
        sl = slice(hh * LANES, (hh + 1) * LANES)
        qh = rot(proj[:, w + hh * LANES:w + (hh + 1) * LANES])
        q_ref[:, sl] = (qh * scale).astype(BF16)
        kh = rot(proj[:, 2 * w + hh * LANES:2 * w + (hh + 1) * LANES])
        k_ref[pl.ds(hh, tm, stride=nh), :] = kh
        kb_ref[:, sl] = kh.astype(BF16)
        v_ref[pl.ds(hh, tm, stride=nh), :] = proj[:, 3 * w + hh * LANES:3 * w + (hh + 1) * LANES]
    vb_ref[...] = proj[:, 3 * w:].astype(BF16)
    gl = _dot(hb, wgate_ref[...]) + bgate_ref[...]
    gates_ref[...] = jax.nn.sigmoid(gl).astype(BF16)


def _proj(x2d, n_batch, t_len, tm, g, win_b, wgate_b, bgate, cos_t, sin_t, half, scale):
    n, d = x2d.shape
    w = win_b.shape[1] // 4
    nh = w // LANES
    nt = t_len // tm
    row = lambda b, t: (b * nt + t, 0)
    const = lambda b, t: (0, 0)
    outs = pl.pallas_call(
        functools.partial(_proj_kernel, half=half, scale=scale),
        grid=(n_batch, nt),
        in_specs=[pl.BlockSpec((tm, d), row),
                  pl.BlockSpec((1, d), const),
                  pl.BlockSpec(win_b.shape, const),
                  pl.BlockSpec(wgate_b.shape, const),
                  pl.BlockSpec((1, wgate_b.shape[1]), const),
                  pl.BlockSpec((tm, LANES), lambda b, t: (t, 0)),
                  pl.BlockSpec((tm, LANES), lambda b, t: (t, 0))],
        out_specs=[pl.BlockSpec((tm, w), lambda b, t: (t, b)),
                   pl.BlockSpec((tm, w), row),
                   pl.BlockSpec((tm * nh, LANES), row),
                   pl.BlockSpec((tm * nh, LANES), row),
                   pl.BlockSpec((tm, w), row),
                   pl.BlockSpec((tm, w), row),
                   pl.BlockSpec((tm, 2 * d), row)],
        out_shape=[jax.ShapeDtypeStruct((t_len, n_batch * w), F32),
                   jax.ShapeDtypeStruct((n, w), BF16),
                   jax.ShapeDtypeStruct((n * nh, LANES), F32),
                   jax.ShapeDtypeStruct((n * nh, LANES), F32),
                   jax.ShapeDtypeStruct((n, w), BF16),
                   jax.ShapeDtypeStruct((n, w), BF16),
                   jax.ShapeDtypeStruct((n, 2 * d), BF16)],
        compiler_params=_cparams(("parallel", "parallel")),
        name="proj",
    )(x2d, g, win_b, wgate_b, bgate, cos_t, sin_t)
    return outs


def _s5_kernel(u_ref, x0_ref, lre_ref, lim_ref, ldt_ref, bre_ref, bim_ref, cre_ref, cim_ref,
               d_ref, wglu_ref, bglu_ref, y_ref, st_ref,
               zx_ref, wz_ref, are_ref, aim_ref, state_ref, *, nb, tc, ns, sw):
    step = pl.program_id(0)

    @pl.when(step == 0)
    def _init():
        lre = lre_ref[...]
        lim = lim_ref[...]
        dt = jnp.exp(ldt_ref[...])
        mag = jnp.exp(lre * dt)
        are = mag * jnp.cos(lim * dt)
        aim = mag * jnp.sin(lim * dt)
        den = lre * lre + lim * lim
        fre = ((are - 1.0) * lre + aim * lim) / den
        fim = (aim * lre - (are - 1.0) * lim) / den
        are_ref[...] = are
        aim_ref[...] = aim
        for i in range(ns):
            fr = fre[:, i * sw:(i + 1) * sw]
            fi = fim[:, i * sw:(i + 1) * sw]
            br = bre_ref[i]
            bi = bim_ref[i]
            wz_ref[i, :, :sw] = (fr * br - fi * bi).astype(BF16)
            wz_ref[i, :, sw:] = (fr * bi + fi * br).astype(BF16)
        state_ref[...] = x0_ref[...]

    ub = u_ref[...]
    ubb = ub.astype(BF16)
    for i in range(ns):
        zx_ref[:, 2 * sw * i:2 * sw * (i + 1)] = _dot(ubb[:, LANES * i:LANES * (i + 1)], wz_ref[i])

    for i in range(ns):
        c0 = 2 * sw * i
        arb = jnp.broadcast_to(are_ref[:, i * sw:(i + 1) * sw], (nb, sw))
        aib = jnp.broadcast_to(aim_ref[:, i * sw:(i + 1) * sw], (nb, sw))

        def body(t, carry, c0=c0, arb=arb, aib=aib):
            xr, xi = carry
            r0 = pl.multiple_of(t * nb, nb)
            zr = zx_ref[pl.ds(r0, nb), c0:c0 + sw]
            zi = zx_ref[pl.ds(r0, nb), c0 + sw:c0 + 2 * sw]
            nxr = arb * xr - aib * xi + zr
            nxi = arb * xi + aib * xr + zi
            zx_ref[pl.ds(r0, nb), c0:c0 + sw] = nxr
            zx_ref[pl.ds(r0, nb), c0 + sw:c0 + 2 * sw] = nxi
            return nxr, nxi

        xr, xi = lax.fori_loop(0, tc, body,
                               (state_ref[:, c0:c0 + sw], state_ref[:, c0 + sw:c0 + 2 * sw]))
        state_ref[:, c0:c0 + sw] = xr
        state_ref[:, c0 + sw:c0 + 2 * sw] = xi

    ys = []
    for i in range(ns):
        c0 = 2 * sw * i
        xr = zx_ref[:, c0:c0 + sw].astype(BF16)
        xi = zx_ref[:, c0 + sw:c0 + 2 * sw].astype(BF16)
        ys.append(_dot(xr, cre_ref[i]) - _dot(xi, cim_ref[i]))
    y = jnp.concatenate(ys, axis=1) + d_ref[...] * ub
    y = jax.nn.gelu(y)
    gl = _dot(y.astype(BF16), wglu_ref[...]) + bglu_ref[...]
    y_ref[...] = (y * jax.nn.sigmoid(gl)).astype(BF16)

    @pl.when(step == pl.num_programs(0) - 1)
    def _fin():
        st_ref[...] = state_ref[...]


def _s5_layout(lp):
    g, p, h = lp['ssm_b_re'].shape
    gps = LANES // h
    ns = g // gps
    eye = jnp.eye(gps, dtype=jnp.bool_)

    def bd_b(b):
        bb = b.reshape(ns, gps, p, h).transpose(0, 1, 3, 2)
        out = jnp.where(eye[None, :, None, :, None], bb[:, :, :, None, :], 0.0)
        return out.reshape(ns, gps * h, gps * p)

    def bd_c(c):
        cc = c.reshape(ns, gps, h, p).transpose(0, 1, 3, 2)
        out = jnp.where(eye[None, :, None, :, None], cc[:, :, :, None, :], 0.0)
        return out.reshape(ns, gps * p, gps * h)

    return dict(
        lre=lp['ssm_lambda_re'].reshape(1, g * p),
        lim=lp['ssm_lambda_im'].reshape(1, g * p),
        ldt=jnp.repeat(lp['ssm_log_dt'], p).reshape(1, g * p),
        bre=bd_b(lp['ssm_b_re']), bim=bd_b(lp['ssm_b_im']),
        cre=bd_c(lp['ssm_c_re']).astype(BF16), cim=bd_c(lp['ssm_c_im']).astype(BF16),
        d=lp['ssm_d'].reshape(1, g * h),
        wglu=lp['ssm_w_glu'].astype(BF16), bglu=lp['ssm_b_glu'].reshape(1, -1),
        ns=ns, sw=gps * p)


def _state_to_cols(re, im, ns):
    b = re.shape[0]
    return jnp.stack([re.reshape(b, ns, -1), im.reshape(b, ns, -1)], axis=2).reshape(b, -1)


def _cols_to_state(st, ns, g, p):
    b = st.shape[0]
    s = st.reshape(b, ns, 2, -1)
    return s[:, :, 0].reshape(b, g, p), s[:, :, 1].reshape(b, g, p)


def _s5(u_tb, x0, sp, nb, t_len, tc):
    rows = nb * tc
    w = u_tb.shape[1]
    ns, sw = sp['ns'], sp['sw']
    ncol = ns * 2 * sw
    full = lambda a: pl.BlockSpec(a.shape, lambda i: (0,) * a.ndim)
    args = (u_tb, x0, sp['lre'], sp['lim'], sp['ldt'], sp['bre'], sp['bim'], sp['cre'], sp['cim'],
            sp['d'], sp['wglu'], sp['bglu'])
    return pl.pallas_call(
        functools.partial(_s5_kernel, nb=nb, tc=tc, ns=ns, sw=sw),
        grid=(t_len // tc,),
        in_specs=[pl.BlockSpec((rows, w), lambda i: (i, 0))] + [full(a) for a in args[1:]],
        out_specs=[pl.BlockSpec((rows, w), lambda i: (i, 0)),
                   pl.BlockSpec((nb, ncol), lambda i: (0, 0))],
        out_shape=[jax.ShapeDtypeStruct((t_len * nb, w), BF16),
                   jax.ShapeDtypeStruct((nb, ncol), F32)],
        scratch_shapes=[pltpu.VMEM((rows, ncol), F32),
                        pltpu.VMEM((ns, LANES, 2 * sw), BF16),
                        pltpu.VMEM((1, ns * sw), F32),
                        pltpu.VMEM((1, ns * sw), F32),
                        pltpu.VMEM((nb, ncol), F32)],
        compiler_params=_cparams(("arbitrary",)),
        name="s5",
    )(*args)


def _diff_lambda(lq1, lk1, lq2, lk2, lambda_init):
    return (jnp.exp(jnp.sum(lq1[...] * lk1[...], keepdims=True))
            - jnp.exp(jnp.sum(lq2[...] * lk2[...], keepdims=True)) + lambda_init)


def _attn_kernel(q_ref, k_ref, v_ref, lq1, lk1, lq2, lk2, g_ref, o_ref, *, tq, hd, lambda_init):
    t_len = q_ref.shape[0]
    lam = _diff_lambda(lq1, lk1, lq2, lk2, lambda_init)
    lo = lax.broadcasted_iota(jnp.int32, (tq, LANES), 1) < hd
    max_blocks = max(ATTN_KEY_CHUNK // tq, 1)

    for qi in range(t_len // tq):
        q = q_ref[qi * tq:(qi + 1) * tq, :]
        zero = jnp.zeros_like(q)
        qs = jnp.concatenate([jnp.where(lo, q, zero), jnp.where(lo, zero, q)], axis=0)
        m = jnp.full((1, 2 * tq), NEG_INF, F32)
        l = jnp.zeros((1, 2 * tq), F32)
        acc = jnp.zeros((LANES, 2 * tq), F32)
        n_blocks = qi + 1
        n_chunks = -(-n_blocks // max_blocks)
        k0 = 0
        for ci in range(n_chunks):
            ksz = (n_blocks // n_chunks + (1 if ci < n_blocks % n_chunks else 0)) * tq
            s = _dot_nt(k_ref[k0:k0 + ksz, :], qs)
            if ci == n_chunks - 1:
                kpos = k0 + lax.broadcasted_iota(jnp.int32, (ksz, 2 * tq), 0)
                qc = lax.broadcasted_iota(jnp.int32, (ksz, 2 * tq), 1)
                s = jnp.where(kpos <= qi * tq + jnp.where(qc >= tq, qc - tq, qc), s, NEG_INF)
            m_new = jnp.maximum(m, jnp.max(s, axis=0, keepdims=True))
            alpha = jnp.exp(m - m_new)
            p = jnp.exp(s - m_new)
            l = alpha * l + jnp.sum(p, axis=0, keepdims=True)
            pv = lax.dot_general(v_ref[k0:k0 + ksz, :], p.astype(BF16), (((0,), (0,)), ((), ())),
                                 preferred_element_type=F32)
            acc = alpha * acc + pv
            m = m_new
            k0 += ksz
        o_t = (acc[:, :tq] / l[:, :tq]) - lam * (acc[:, tq:] / l[:, tq:])
        ms = jnp.mean(o_t * o_t, axis=0, keepdims=True)
        o_t = o_t * lax.rsqrt(ms + NORM_EPS) * g_ref[...] * (1.0 - lambda_init)
        o_ref[qi * tq:(qi + 1) * tq, :] = o_t.T.astype(BF16)


def _attn(qb, kb, vb, lam_params, g, n_batch, t_len, tq, hd, lambda_init):
    n, w = qb.shape
    nh = w // LANES
    blk = pl.BlockSpec((t_len, LANES), lambda b, h: (b, h))
    small = lambda a: pl.BlockSpec(a.shape, lambda b, h: (0, 0))
    g_cols = jnp.broadcast_to(g.reshape(LANES, 1), (LANES, tq))
    return pl.pallas_call(
        functools.partial(_attn_kernel, tq=tq, hd=hd, lambda_init=lambda_init),
        grid=(n_batch, nh),
        in_specs=[blk, blk, blk] + [small(a) for a in lam_params] + [small(g_cols)],
        out_specs=blk,
        out_shape=jax.ShapeDtypeStruct((n, w), BF16),
        compiler_params=_cparams(("parallel", "parallel")),
        name="attn_prompt",
    )(qb, kb, vb, *lam_params, g_cols)


def _sattn_kernel(pt_ref, q_ref, kown_ref, vown_ref, *rest, npg, prow, dq, nh, lambda_init):
    k_refs = rest[:npg]
    v_refs = rest[npg:2 * npg]
    lq1, lk1, lq2, lk2, g_ref, o_ref, kb_ref, vb_ref, m_ref, l_ref, acc_ref = rest[2 * npg:]
    j = pl.program_id(1)
    nq = q_ref.shape[1]
    rq = nq // nh
    page = prow // nh
    q = [q_ref[0, h * rq:(h + 1) * rq, :].astype(BF16) for h in range(nh)]

    @pl.when(j == 0)
    def _init():
        m_ref[...] = jnp.full(m_ref.shape, NEG_INF, F32)
        l_ref[...] = jnp.zeros(l_ref.shape, F32)
        acc_ref[...] = jnp.zeros(acc_ref.shape, F32)

    def update(s, values):
        m_prev = m_ref[...]
        m_new = jnp.maximum(m_prev, jnp.max(s, axis=1, keepdims=True))
        alpha = jnp.exp(m_prev - m_new)
        p = jnp.exp(s - m_new)
        l_ref[...] = alpha * l_ref[...] + jnp.sum(p, axis=1, keepdims=True)
        pv = jnp.concatenate([_dot(p[h * rq:(h + 1) * rq].astype(BF16), values[h])
                              for h in range(nh)], axis=0)
        acc_ref[...] = alpha * acc_ref[...] + pv
        m_ref[...] = m_new

    for p_ in range(npg):
        for h in range(nh):
            rows = pl.ds(h, page, stride=nh)
            kb_ref[h, p_ * page:(p_ + 1) * page, :] = k_refs[p_][0, rows, :].astype(BF16)
            vb_ref[h, p_ * page:(p_ + 1) * page, :] = v_refs[p_][0, rows, :].astype(BF16)
    s = jnp.concatenate([_dot_nt(q[h], kb_ref[h]) for h in range(nh)], axis=0)
    update(s, [vb_ref[h] for h in range(nh)])

    @pl.when(j == pl.num_programs(1) - 1)
    def _fin():
        k_own = [kown_ref[0, pl.ds(h, dq, stride=nh), :].astype(BF16) for h in range(nh)]
        v_own = [vown_ref[0, pl.ds(h, dq, stride=nh), :].astype(BF16) for h in range(nh)]
        s_own = jnp.concatenate([_dot_nt(q[h], k_own[h]) for h in range(nh)], axis=0)
        r = lax.broadcasted_iota(jnp.int32, (nq, dq), 0)
        c = lax.broadcasted_iota(jnp.int32, (nq, dq), 1)
        update(jnp.where(c <= r % dq, s_own, NEG_INF), v_own)
        lam = _diff_lambda(lq1, lk1, lq2, lk2, lambda_init)
        o_all = acc_ref[...] / l_ref[...]
        for h in range(nh):
            o = o_all[h * rq:h * rq + dq] - lam * o_all[h * rq + dq:(h + 1) * rq]
            o_ref[0, h * dq:(h + 1) * dq, :] = (_rms(o, g_ref[...])
                                                * (1.0 - lambda_init)).astype(BF16)


def _sattn(page_table, q_rows, k_own, v_own, cache_k, cache_v, layer, lam_params, g, npg, dq, nh,
           lambda_init):
    nb, nq, _ = q_rows.shape
    prow = cache_k.shape[2]
    n_pages = page_table.shape[1]
    nsteps = n_pages // npg

    def page_spec(p_):
        return pl.BlockSpec((None, 1, prow, LANES),
                            lambda b, j, pt: (layer, pt[b, j * npg + p_], 0, 0))

    per_b = lambda a: pl.BlockSpec((1,) + a.shape[1:], lambda b, j, pt: (b, 0, 0))
    small = lambda a: pl.BlockSpec(a.shape, lambda b, j, pt: (0, 0))
    grid_spec = pltpu.PrefetchScalarGridSpec(
        num_scalar_prefetch=1,
        grid=(nb, nsteps),
        in_specs=([per_b(q_rows), per_b(k_own), per_b(v_own)]
                  + [page_spec(p_) for p_ in range(npg)]
                  + [page_spec(p_) for p_ in range(npg)]
                  + [small(a) for a in lam_params] + [small(g)]),
        out_specs=pl.BlockSpec((1, nq // 2, LANES), lambda b, j, pt: (b, 0, 0)),
        scratch_shapes=[pltpu.VMEM((nh, npg * prow // nh, LANES), BF16),
                        pltpu.VMEM((nh, npg * prow // nh, LANES), BF16),
                        pltpu.VMEM((nq, 1), F32),
                        pltpu.VMEM((nq, 1), F32),
                        pltpu.VMEM((nq, LANES), F32)])
    return pl.pallas_call(
        functools.partial(_sattn_kernel, npg=npg, prow=prow, dq=dq, nh=nh,
                          lambda_init=lambda_init),
        grid_spec=grid_spec,
        out_shape=jax.ShapeDtypeStruct((nb, nq // 2, LANES), BF16),
        compiler_params=_cparams(("parallel", "arbitrary")),
        name="attn_sample",
    )(page_table, q_rows, k_own, v_own, *([cache_k] * npg), *([cache_v] * npg), *lam_params, g)


def _merge_kernel(x_ref, gates_ref, y_ref, o_ref, wbs_ref, wba_ref, wout_ref, gffn_ref,
                  wr_ref, br_ref, x1_ref, h2_ref, ri_ref, rf_ref, cnt_ref, run_ref, *, ne):
    first = jnp.logical_and(pl.program_id(0) == 0, pl.program_id(1) == 0)

    @pl.when(first)
    def _init():
        run_ref[...] = jnp.zeros(run_ref.shape, F32)

    tm, d = x_ref.shape
    a = _dot(y_ref[...], wbs_ref[...])
    b = _dot(o_ref[...], wba_ref[...])
    merged = gates_ref[:, :d].astype(F32) * a + gates_ref[:, d:].astype(F32) * b
    x1 = x_ref[...] + _dot(merged.astype(BF16), wout_ref[...])
    x1_ref[...] = x1
    h2b = _rms(x1, gffn_ref[...]).astype(BF16)
    bits = lax.bitcast_convert_type(h2b.astype(F32), jnp.uint32)
    h2_ref[...] = bits[:, d // 2:] | (bits[:, :d // 2] >> 16)

    logits = _dot(h2b, wr_ref[...]) + br_ref[...]
    lane = lax.broadcasted_iota(jnp.int32, logits.shape, 1)
    lane_f = lane.astype(F32)
    cur = jnp.where(lane < ne, logits, -jnp.inf)
    vals, idxs, hots = [], [], []
    for _ in range(TOP_K):
        mk = jnp.max(cur, axis=1, keepdims=True)
        ik = jnp.min(jnp.where(cur == mk, lane_f, float(LANES)), axis=1, keepdims=True)
        hot = lane_f == ik
        cur = jnp.where(hot, -jnp.inf, cur)
        vals.append(mk)
        idxs.append(ik.astype(jnp.int32))
        hots.append(hot)
    exps = [jnp.exp(v - vals[0]) for v in vals]
    denom = sum(exps[1:], exps[0])
    chosen = functools.reduce(jnp.logical_or, hots)
    cnt = jnp.where(chosen, 1.0, 0.0)
    r = lax.broadcasted_iota(jnp.int32, (tm, tm), 0)
    c = lax.broadcasted_iota(jnp.int32, (tm, tm), 1)
    before = _dot((c < r).astype(BF16), cnt.astype(BF16)) + run_ref[...]
    ri = jnp.zeros(logits.shape, jnp.int32)
    rf = jnp.zeros(logits.shape, F32)
    for k in range(TOP_K):
        rank = jnp.sum(jnp.where(hots[k], before, 0.0), axis=1, keepdims=True).astype(jnp.int32)
        ri = jnp.where(lane == k, idxs[k], ri)
        ri = jnp.where(lane == TOP_K + k, rank, ri)
        rf = jnp.where(lane == k, exps[k] / denom, rf)
    ri_ref[...] = ri.T[:2 * TOP_K]
    rf_ref[...] = rf
    run_ref[...] = run_ref[...] + jnp.sum(cnt, axis=0, keepdims=True)
    cnt_ref[...] = run_ref[...]


def _merge(x2d, gates, y_tb, o, n_batch, t_len, tm, wbs, wba, wout, gffn, wr, br, ne):
    n, d = x2d.shape
    w = o.shape[1]
    nt = t_len // tm
    row = lambda b, t: (b * nt + t, 0)
    const = lambda b, t: (0, 0)
    return pl.pallas_call(
        functools.partial(_merge_kernel, ne=ne),
        grid=(n_batch, nt),
        in_specs=[pl.BlockSpec((tm, d), row),
                  pl.BlockSpec((tm, 2 * d), row),
                  pl.BlockSpec((tm, w), lambda b, t: (t, b)),
                  pl.BlockSpec((tm, w), row),
                  pl.BlockSpec(wbs.shape, const),
                  pl.BlockSpec(wba.shape, const),
                  pl.BlockSpec(wout.shape, const),
                  pl.BlockSpec((1, d), const),
                  pl.BlockSpec(wr.shape, const),
                  pl.BlockSpec((1, LANES), const)],
        out_specs=[pl.BlockSpec((tm, d), row),
                   pl.BlockSpec((tm, d // 2), row),
                   pl.BlockSpec((2 * TOP_K, tm), lambda b, t: (0, b * nt + t)),
                   pl.BlockSpec((tm, LANES), row),
                   pl.BlockSpec((1, LANES), const)],
        out_shape=[jax.ShapeDtypeStruct((n, d), F32),
                   jax.ShapeDtypeStruct((n, d // 2), jnp.uint32),
                   jax.ShapeDtypeStruct((2 * TOP_K, n), jnp.int32),
                   jax.ShapeDtypeStruct((n, LANES), F32),
                   jax.ShapeDtypeStruct((1, LANES), F32)],
        scratch_shapes=[pltpu.VMEM((1, LANES), F32)],
        compiler_params=_cparams(("arbitrary", "arbitrary")),
        name="merge",
    )(x2d, gates, y_tb, o, wbs, wba, wout, gffn, wr, br)


def _moe_kernel(be_ref, nu_ref, x_ref, wgu_ref, bgu_ref, wd_ref, bd_ref, o_ref, wgu_s, wd_s):
    i = pl.program_id(0)
    active = i < nu_ref[0]
    fresh = jnp.logical_or(i == 0, be_ref[i] != be_ref[jnp.maximum(i - 1, 0)])
    pc = PAIR_COLS
    hc = pc // 2
    n_chunk = wgu_ref.shape[2] // pc

    @pl.when(jnp.logical_and(active, fresh))
    def _stage_weights():
        r = lax.broadcasted_iota(jnp.int32, (pc, pc), 0)
        c = lax.broadcasted_iota(jnp.int32, (pc, pc), 1)
        perm = (r == jnp.where(c < hc, 2 * c, 2 * (c - hc) + 1)).astype(BF16)
        for j in range(n_chunk):
            wj = wgu_ref[0, :, j * pc:(j + 1) * pc].astype(BF16)
            wgu_s[:, j * pc:(j + 1) * pc] = _dot(wj, perm).astype(BF16)
        wd_s[...] = wd_ref[0].astype(BF16)

    @pl.when(active)
    def _compute():
        words = x_ref[...]
        x = jnp.concatenate(
            [lax.bitcast_convert_type(words << 16, F32),
             lax.bitcast_convert_type(words & jnp.uint32(0xFFFF0000), F32)], axis=1).astype(BF16)
        gu = _dot(x, wgu_s[...]) + bgu_ref[0]
        acts = []
        for j in range(n_chunk):
            g_lin = jnp.minimum(gu[:, j * pc:j * pc + hc], SWIGLU_LIMIT)
            up = jnp.clip(gu[:, j * pc + hc:(j + 1) * pc], -SWIGLU_LIMIT, SWIGLU_LIMIT)
            acts.append(((up + 1.0) * (g_lin * jax.nn.sigmoid(SWIGLU_ALPHA * g_lin))).astype(BF16))
        o_ref[...] = _dot(jnp.concatenate(acts, axis=1), wd_s[...]) + bd_ref[0]

    @pl.when(jnp.logical_not(active))
    def _skip():
        o_ref[...] = jnp.zeros(o_ref.shape, o_ref.dtype)


def _moe_gemm(xs, blk_expert, n_used, wgu, bgu, wd, bd):
    n_rows = xs.shape[0]
    d = wd.shape[2]
    bm = MOE_BLOCK_ROWS
    wspec = lambda a: pl.BlockSpec((1,) + a.shape[1:], lambda i, be, nu: (be[i], 0, 0))
    grid_spec = pltpu.PrefetchScalarGridSpec(
        num_scalar_prefetch=2,
        grid=(n_rows // bm,),
        in_specs=[pl.BlockSpec((bm, xs.shape[1]), lambda i, be, nu: (i, 0)),
                  wspec(wgu), wspec(bgu), wspec(wd), wspec(bd)],
        out_specs=pl.BlockSpec((bm, d), lambda i, be, nu: (i, 0)),
        scratch_shapes=[pltpu.VMEM(wgu.shape[1:], BF16),
                        pltpu.VMEM(wd.shape[1:], BF16)])
    return pl.pallas_call(
        _moe_kernel,
        grid_spec=grid_spec,
        out_shape=jax.ShapeDtypeStruct((n_rows, d), F32),
        compiler_params=_cparams(("arbitrary",)),
        name="moe_gemm",
    )(blk_expert, n_used, xs, wgu, bgu, wd, bd)


def _combine_kernel(x_ref, *rest, final_norm):
    yg_refs = rest[:TOP_K]
    gate_ref, g_ref, o_ref = rest[TOP_K:]
    acc = x_ref[...]
    gate = gate_ref[...]
    for k in range(TOP_K):
        acc = acc + gate[:, k:k + 1] * yg_refs[k][...]
    o_ref[...] = _rms(acc, g_ref[...]) if final_norm else acc


def _combine(x1, ygs, gate, g, tm, final_norm):
    n, d = x1.shape
    row = pl.BlockSpec((tm, d), lambda i: (i, 0))
    return pl.pallas_call(
        functools.partial(_combine_kernel, final_norm=final_norm),
        grid=(n // tm,),
        in_specs=[row] + [row] * TOP_K + [pl.BlockSpec((tm, LANES), lambda i: (i, 0)),
                                          pl.BlockSpec((1, d), lambda i: (0, 0))],
        out_specs=row,
        out_shape=jax.ShapeDtypeStruct((n, d), F32),
        compiler_params=_cparams(("parallel",)),
        name="combine",
    )(x1, *ygs, gate, g)


def _moe_dispatch(h2p, route_i, count_row, ne):
    n = h2p.shape[0]
    bm = MOE_BLOCK_ROWS
    m = n * TOP_K
    counts = count_row[0, :ne].astype(jnp.int32)
    start = jnp.cumsum(counts) - counts
    padded = (counts + bm - 1) // bm * bm
    pend = jnp.cumsum(padded)
    pstart = pend - padded
    dest = [pstart[route_i[k]] + route_i[TOP_K + k] for k in range(TOP_K)]
    n_rows = (m + bm - 1) // bm * bm + ne * bm
    blk_start = jnp.arange(n_rows // bm, dtype=jnp.int32) * bm
    blk_expert = jnp.minimum(
        jnp.sum((blk_start[:, None] >= pend[None, :]).astype(jnp.int32), axis=1), ne - 1)
    n_used = (pend[-1] // bm).astype(jnp.int32).reshape(1)
    tok = jnp.tile(jnp.arange(n, dtype=jnp.int32), TOP_K)
    _, tok_sorted = lax.sort_key_val(jnp.concatenate(dest), tok)
    blk_shift = (pstart - start)[blk_expert]
    src = jnp.arange(n_rows, dtype=jnp.int32) - jnp.repeat(blk_shift, bm)
    row_tok = tok_sorted[jnp.clip(src, 0, m - 1)]
    return dict(xs=h2p[row_tok], blk_expert=blk_expert, n_used=n_used, dest=dest)


def _moe_apply(x1, disp, route_f, mp, g_final, final_norm):
    ys = _moe_gemm(disp['xs'], disp['blk_expert'], disp['n_used'],
                   mp['wgu'], mp['bgu'], mp['wd'], mp['bd'])
    ygs = [ys[disp['dest'][k]] for k in range(TOP_K)]
    tm = min(256, x1.shape[0])
    return _combine(x1, ygs, route_f, g_final, tm, final_norm)


def _rope_tables(pos, hd):
    half = hd // 2
    inv_freq = ROPE_THETA ** (-jnp.arange(half, dtype=F32) / half)
    ang = pos.astype(F32)[:, None] * inv_freq[None, :]
    cos = jnp.cos(ang)
    sin = jnp.sin(ang)
    reps = LANES // hd
    cos_t = jnp.tile(jnp.concatenate([cos, cos], axis=1), (1, reps))
    sin_t = jnp.tile(jnp.concatenate([-sin, sin], axis=1), (1, reps))
    return cos_t, sin_t


def _layer_params(lp):
    d = lp['w_in'].shape[0]
    ne = lp['w_router'].shape[1]
    wr = jnp.zeros((d, LANES), F32).at[:, :ne].set(lp['w_router']).astype(BF16)
    br = jnp.zeros((1, LANES), F32).at[0, :ne].set(lp['b_router'])
    return dict(
        g_mix=lp['norm_mix_g'].reshape(1, d),
        win=lp['w_in'].astype(BF16),
        wgate=lp['w_gate'].astype(BF16),
        bgate=lp['b_gate'].reshape(1, -1),
        s5=_s5_layout(lp),
        lam=[lp[k].reshape(1, -1) for k in ('lambda_q1', 'lambda_k1', 'lambda_q2', 'lambda_k2')],
        g_sub=lp['attn_subln_g'].reshape(1, -1),
        wbs=lp['w_branch_ssm'].astype(BF16),
        wba=lp['w_branch_attn'].astype(BF16),
        wout=lp['w_out'].astype(BF16),
        g_ffn=lp['norm_ffn_g'].reshape(1, d),
        wr=wr, br=br,
        moe=dict(ne=ne,
                 wgu=lp['w_gate_up'],
                 bgu=lp['b_gate_up'].reshape(ne, -1, PAIR_COLS // 2, 2).transpose(0, 1, 3, 2)
                 .reshape(ne, 1, -1),
                 wd=lp['w_down'],
                 bd=lp['b_down'][:, None, :]))


def _query_rows(qb, bsz, dq, nh, hd):
    qt = qb.astype(F32).reshape(bsz, dq, nh, 2, hd).transpose(0, 2, 3, 1, 4)
    ec = jnp.eye(2, dtype=jnp.bool_)[None, None, :, None, :, None]
    out = jnp.where(ec, qt[:, :, :, :, None, :], 0.0)
    return out.reshape(bsz, nh * 2 * dq, 2 * hd)


def _layer(x, pos, x0_re, x0_im, past, P, lambda_init):
    bsz, t_len, d = x.shape
    n = bsz * t_len
    x2d = x.reshape(n, d)
    sp = P['s5']
    g, p = x0_re.shape[1], x0_re.shape[2]
    w = P['win'].shape[1] // 4
    nh = w // LANES
    hd = LANES // 2
    cos_t, sin_t = _rope_tables(pos, hd)
    x0 = _state_to_cols(x0_re.astype(F32), x0_im.astype(F32), sp['ns'])
    if past is None:
        n_batch, rows_t, tm = bsz, t_len, min(256, t_len)
    else:
        n_batch, rows_t, tm = 1, n, n
        cos_t = jnp.tile(cos_t, (bsz, 1))
        sin_t = jnp.tile(sin_t, (bsz, 1))
    u, qb, k, v, kb, vb, gates = _proj(x2d, n_batch, rows_t, tm, P['g_mix'], P['win'], P['wgate'],
                                       P['bgate'], cos_t, sin_t, hd // 2, hd ** -0.5)
    if past is None:
        u_tb = u.reshape(t_len * bsz, w)
        y_tb, st = _s5(u_tb, x0, sp, bsz, t_len, min(64, t_len))
        y_in = y_tb.reshape(t_len, bsz * w)
        o = _attn(qb, kb, vb, P['lam'], P['g_sub'], bsz, t_len, min(256, t_len), hd, lambda_init)
    else:
        cache_k, cache_v, page_table, layer = past
        u_tb = u.reshape(bsz, t_len, w).transpose(1, 0, 2).reshape(t_len * bsz, w)
        y_tb, st = _s5(u_tb, x0, sp, bsz, t_len, t_len)
        y_in = y_tb.reshape(t_len, bsz, w).transpose(1, 0, 2).reshape(n, w)
        npg = math.gcd(8, page_table.shape[1])
        cshape = cache_k.shape[:2] + (cache_k.shape[2] * nh, LANES)
        o = _sattn(page_table, _query_rows(qb, bsz, t_len, nh, hd),
                   k.reshape(bsz, t_len * nh, LANES), v.reshape(bsz, t_len * nh, LANES),
                   cache_k.reshape(cshape), cache_v.reshape(cshape), layer,
                   P['lam'], P['g_sub'], npg, t_len, nh, lambda_init)
        o = o.reshape(bsz, nh, t_len, LANES).transpose(0, 2, 1, 3).reshape(n, w)
    x1, h2p, route_i, route_f, count_row = _merge(
        x2d, gates, y_in, o, n_batch, rows_t, tm, P['wbs'], P['wba'], P['wout'], P['g_ffn'],
        P['wr'], P['br'], P['moe']['ne'])
    disp = _moe_dispatch(h2p, route_i, count_row, P['moe']['ne'])
    st_re, st_im = _cols_to_state(st, sp['ns'], g, p)
    side = (k.reshape(bsz, t_len, nh, LANES), v.reshape(bsz, t_len, nh, LANES),
            st_re.astype(x.dtype), st_im.astype(x.dtype))
    return (x1, disp, route_f), side


def kernel(x_prompt, x_sample, cache_k, cache_v, state_ssm_re, state_ssm_im, page_table, norm_mix_g, w_in, ssm_lambda_re, ssm_lambda_im, ssm_log_dt, ssm_b_re, ssm_b_im, ssm_c_re, ssm_c_im, ssm_d, ssm_w_glu, ssm_b_glu, lambda_q1, lambda_k1, lambda_q2, lambda_k2, attn_subln_g, w_branch_ssm, w_branch_attn, w_gate, b_gate, w_out, norm_ffn_g, w_router, b_router, w_gate_up, b_gate_up, w_down, b_down, norm_final_g):
    depth = w_in.shape[0]
    past_len = page_table.shape[1] * cache_k.shape[2]
    pos_prompt = jnp.arange(x_prompt.shape[1], dtype=jnp.int32)
    pos_sample = past_len + jnp.arange(x_sample.shape[1], dtype=jnp.int32)
    g, p = state_ssm_re.shape[2], state_ssm_re.shape[3]
    zero_state = jnp.zeros((x_prompt.shape[0], g, p), F32)
    g_final = norm_final_g.reshape(1, -1)
    names = ('norm_mix_g', 'w_in', 'ssm_lambda_re', 'ssm_lambda_im', 'ssm_log_dt', 'ssm_b_re',
             'ssm_b_im', 'ssm_c_re', 'ssm_c_im', 'ssm_d', 'ssm_w_glu', 'ssm_b_glu', 'lambda_q1',
             'lambda_k1', 'lambda_q2', 'lambda_k2', 'attn_subln_g', 'w_branch_ssm',
             'w_branch_attn', 'w_gate', 'b_gate', 'w_out', 'norm_ffn_g', 'w_router', 'b_router',
             'w_gate_up', 'b_gate_up', 'w_down', 'b_down')
    stacked = (norm_mix_g, w_in, ssm_lambda_re, ssm_lambda_im, ssm_log_dt, ssm_b_re, ssm_b_im,
               ssm_c_re, ssm_c_im, ssm_d, ssm_w_glu, ssm_b_glu, lambda_q1, lambda_k1, lambda_q2,
               lambda_k2, attn_subln_g, w_branch_ssm, w_branch_attn, w_gate, b_gate, w_out,
               norm_ffn_g, w_router, b_router, w_gate_up, b_gate_up, w_down, b_down)
    xp, xs = x_prompt, x_sample
    outs_p, outs_s = [], []
    for l in range(depth):
        P = _layer_params({nm: a[l] for nm, a in zip(names, stacked)})
        lambda_init = 0.8 - 0.6 * math.exp(-0.3 * l)
        last = l == depth - 1
        moe_p, side_p = _layer(xp, pos_prompt, zero_state, zero_state, None, P, lambda_init)
        moe_s, side_s = _layer(xs, pos_sample, state_ssm_re[l], state_ssm_im[l],
                               (cache_k, cache_v, page_table, l), P, lambda_init)
        (x1_p, disp_p, rf_p), (x1_s, disp_s, rf_s) = moe_p, moe_s
        n_used_p, x1_s = lax.optimization_barrier((disp_p['n_used'], x1_s))
        disp_p = dict(disp_p, n_used=n_used_p)
        xp = _moe_apply(x1_p, disp_p, rf_p, P['moe'], g_final, last).reshape(xp.shape)
        xs = _moe_apply(x1_s, disp_s, rf_s, P['moe'], g_final, last).reshape(xs.shape)
        outs_p.append(side_p)
        outs_s.append(side_s)
    stack = lambda outs, i: jnp.stack([o[i] for o in outs])
    return (xp, xs, stack(outs_p, 0), stack(outs_p, 1), stack(outs_p, 2), stack(outs_p, 3),
            stack(outs_s, 0), stack(outs_s, 1), stack(outs_s, 2), stack(outs_s, 3))
```

```python
import functools
import math

import jax
import jax.numpy as jnp
from jax import lax
from jax.experimental import pallas as pl
from jax.experimental.pallas import tpu as pltpu

F32 = jnp.float32
BF16 = jnp.bfloat16

NORM_EPS = 1e-6
ROPE_THETA = 10000.0
NEG_INF = -1e30
TOP_K = 4
SWIGLU_LIMIT = 7.0
SWIGLU_ALPHA = 1.702
LANES = 128
MOE_BLOCK_ROWS = 512
MOE_BLOCK_ROWS_SMALL = 128
ROW_TILE = 512
PAIR_COLS = 256
ATTN_KEY_CHUNK = 1024
VMEM_LIMIT = 56 * 1024 * 1024


def _cparams(sem):
    return pltpu.CompilerParams(dimension_semantics=sem, vmem_limit_bytes=VMEM_LIMIT)


def _dot(a, b):
    return jnp.dot(a, b, preferred_element_type=F32)


def _dot_nt(a, b):
    return lax.dot_general(a, b, (((1,), (1,)), ((), ())), preferred_element_type=F32)


def _rms(x, g):
    return x * lax.rsqrt(jnp.mean(x * x, axis=-1, keepdims=True) + NORM_EPS) * g


def _proj_kernel(x_ref, g_ref, win_ref, wgate_ref, bgate_ref, cos_ref, sin_ref,
                 u_ref, q_ref, k_ref, v_ref, kb_ref, vb_ref, gates_ref, *, half, scale):
    hb = _rms(x_ref[...], g_ref[...]).astype(BF16)
    proj = _dot(hb, win_ref[...])
    w = u_ref.shape[-1]
    u_ref[...] = proj[:, :w]
    cos = cos_ref[...]
    sin = sin_ref[...]
    lane = lax.broadcasted_iota(jnp.int32, cos.shape, 1)
    lo = (lane % (2 * half)) < half

    def rot(xh):
        fwd = pltpu.roll(xh, LANES - half, axis=1)
        bwd = pltpu.roll(xh, half, axis=1)
        return xh * cos + jnp.where(lo, fwd, bwd) * sin

    nh = w // LANES
    tm = x_ref.shape[0]
    for hh in range(nh):
        sl = slice(hh * LANES, (hh + 1) * LANES)
        qh = rot(proj[:, w + hh * LANES:w + (hh + 1) * LANES])
        q_ref[:, sl] = (qh * scale).astype(BF16)
        kh = rot(proj[:, 2 * w + hh * LANES:2 * w + (hh + 1) * LANES])
        k_ref[pl.ds(hh, tm, stride=nh), :] = kh
        kb_ref[:, sl] = kh.astype(BF16)
        v_ref[pl.ds(hh, tm, stride=nh), :] = proj[:, 3 * w + hh * LANES:3 * w + (hh + 1) * LANES]
    vb_ref[...] = proj[:, 3 * w:].astype(BF16)
    gl = _dot(hb, wgate_ref[...]) + bgate_ref[...]
    gates_ref[...] = jax.nn.sigmoid(gl).astype(BF16)


def _proj(x2d, n_batch, t_len, tm, g, win_b, wgate_b, bgate, cos_t, sin_t, half, scale):
    n, d = x2d.shape
    w = win_b.shape[1] // 4
    nh = w // LANES
    nt = t_len // tm
    row = lambda b, t: (b * nt + t, 0)
    const = lambda b, t: (0, 0)
    outs = pl.pallas_call(
        functools.partial(_proj_kernel, half=half, scale=scale),
        grid=(n_batch, nt),
        in_specs=[pl.BlockSpec((tm, d), row),
                  pl.BlockSpec((1, d), const),
                  pl.BlockSpec(win_b.shape, const),
                  pl.BlockSpec(wgate_b.shape, const),
                  pl.BlockSpec((1, wgate_b.shape[1]), const),
                  pl.BlockSpec((tm, LANES), lambda b, t: (t, 0)),
                  pl.BlockSpec((tm, LANES), lambda b, t: (t, 0))],
        out_specs=[pl.BlockSpec((tm, w), lambda b, t: (t, b)),
                   pl.BlockSpec((tm, w), row),
                   pl.BlockSpec((tm * nh, LANES), row),
                   pl.BlockSpec((tm * nh, LANES), row),
                   pl.BlockSpec((tm, w), row),
                   pl.BlockSpec((tm, w), row),
                   pl.BlockSpec((tm, 2 * d), row)],
        out_shape=[jax.ShapeDtypeStruct((t_len, n_batch * w), F32),
                   jax.ShapeDtypeStruct((n, w), BF16),
                   jax.ShapeDtypeStruct((n * nh, LANES), F32),
                   jax.ShapeDtypeStruct((n * nh, LANES), F32),
                   jax.ShapeDtypeStruct((n, w), BF16),
                   jax.ShapeDtypeStruct((n, w), BF16),
                   jax.ShapeDtypeStruct((n, 2 * d), BF16)],
        compiler_params=_cparams(("parallel", "parallel")),
        name="proj",
    )(x2d, g, win_b, wgate_b, bgate, cos_t, sin_t)
    return outs


def _s5_kernel(u_ref, x0_ref, lre_ref, lim_ref, ldt_ref, bre_ref, bim_ref, cre_ref, cim_ref,
               d_ref, wglu_ref, bglu_ref, y_ref, st_ref,
               zx_ref, wz_ref, are_ref, aim_ref, state_ref, *, nb, tc, ns, sw):
    step = pl.program_id(0)

    @pl.when(step == 0)
    def _init():
        lre = lre_ref[...]
        lim = lim_ref[...]
        dt = jnp.exp(ldt_ref[...])
        mag = jnp.exp(lre * dt)
        are = mag * jnp.cos(lim * dt)
        aim = mag * jnp.sin(lim * dt)
        den = lre * lre + lim * lim
        fre = ((are - 1.0) * lre + aim * lim) / den
        fim = (aim * lre - (are - 1.0) * lim) / den
        are_ref[...] = are
        aim_ref[...] = aim
        for i in range(ns):
            fr = fre[:, i * sw:(i + 1) * sw]
            fi = fim[:, i * sw:(i + 1) * sw]
            br = bre_ref[i]
            bi = bim_ref[i]
            wz_ref[i, :, :sw] = (fr * br - fi * bi).astype(BF16)
            wz_ref[i, :, sw:] = (fr * bi + fi * br).astype(BF16)
        state_ref[...] = x0_ref[...]

    ub = u_ref[...]
    ubb = ub.astype(BF16)
    for i in range(ns):
        zx_ref[:, 2 * sw * i:2 * sw * (i + 1)] = _dot(ubb[:, LANES * i:LANES * (i + 1)], wz_ref[i])

    for i in range(ns):
        c0 = 2 * sw * i
        arb = jnp.broadcast_to(are_ref[:, i * sw:(i + 1) * sw], (nb, sw))
        aib = jnp.broadcast_to(aim_ref[:, i * sw:(i + 1) * sw], (nb, sw))

        def body(t, carry, c0=c0, arb=arb, aib=aib):
            xr, xi = carry
            r0 = pl.multiple_of(t * nb, nb)
            zr = zx_ref[pl.ds(r0, nb), c0:c0 + sw]
            zi = zx_ref[pl.ds(r0, nb), c0 + sw:c0 + 2 * sw]
            nxr = arb * xr - aib * xi + zr
            nxi = arb * xi + aib * xr + zi
            zx_ref[pl.ds(r0, nb), c0:c0 + sw] = nxr
            zx_ref[pl.ds(r0, nb), c0 + sw:c0 + 2 * sw] = nxi
            return nxr, nxi

        xr, xi = lax.fori_loop(0, tc, body,
                               (state_ref[:, c0:c0 + sw], state_ref[:, c0 + sw:c0 + 2 * sw]))
        state_ref[:, c0:c0 + sw] = xr
        state_ref[:, c0 + sw:c0 + 2 * sw] = xi

    ys = []
    for i in range(ns):
        c0 = 2 * sw * i
        xr = zx_ref[:, c0:c0 + sw].astype(BF16)
        xi = zx_ref[:, c0 + sw:c0 + 2 * sw].astype(BF16)
        ys.append(_dot(xr, cre_ref[i]) - _dot(xi, cim_ref[i]))
    y = jnp.concatenate(ys, axis=1) + d_ref[...] * ub
    y = jax.nn.gelu(y)
    gl = _dot(y.astype(BF16), wglu_ref[...]) + bglu_ref[...]
    y_ref[...] = (y * jax.nn.sigmoid(gl)).astype(BF16)

    @pl.when(step == pl.num_programs(0) - 1)
    def _fin():
        st_ref[...] = state_ref[...]


def _s5_layout(lp):
    g, p, h = lp['ssm_b_re'].shape
    gps = LANES // h
    ns = g // gps
    eye = jnp.eye(gps, dtype=jnp.bool_)

    def bd_b(b):
        bb = b.reshape(ns, gps, p, h).transpose(0, 1, 3, 2)
        out = jnp.where(eye[None, :, None, :, None], bb[:, :, :, None, :], 0.0)
        return out.reshape(ns, gps * h, gps * p)

    def bd_c(c):
        cc = c.reshape(ns, gps, h, p).transpose(0, 1, 3, 2)
        out = jnp.where(eye[None, :, None, :, None], cc[:, :, :, None, :], 0.0)
        return out.reshape(ns, gps * p, gps * h)

    return dict(
        lre=lp['ssm_lambda_re'].reshape(1, g * p),
        lim=lp['ssm_lambda_im'].reshape(1, g * p),
        ldt=jnp.repeat(lp['ssm_log_dt'], p).reshape(1, g * p),
        bre=bd_b(lp['ssm_b_re']), bim=bd_b(lp['ssm_b_im']),
        cre=bd_c(lp['ssm_c_re']).astype(BF16), cim=bd_c(lp['ssm_c_im']).astype(BF16),
        d=lp['ssm_d'].reshape(1, g * h),
        wglu=lp['ssm_w_glu'].astype(BF16), bglu=lp['ssm_b_glu'].reshape(1, -1),
        ns=ns, sw=gps * p)


def _state_to_cols(re, im, ns):
    b = re.shape[0]
    return jnp.stack([re.reshape(b, ns, -1), im.reshape(b, ns, -1)], axis=2).reshape(b, -1)


def _cols_to_state(st, ns, g, p):
    b = st.shape[0]
    s = st.reshape(b, ns, 2, -1)
    return s[:, :, 0].reshape(b, g, p), s[:, :, 1].reshape(b, g, p)


def _s5(u_tb, x0, sp, nb, t_len, tc):
    rows = nb * tc
    w = u_tb.shape[1]
    ns, sw = sp['ns'], sp['sw']
    ncol = ns * 2 * sw
    full = lambda a: pl.BlockSpec(a.shape, lambda i: (0,) * a.ndim)
    args = (u_tb, x0, sp['lre'], sp['lim'], sp['ldt'], sp['bre'], sp['bim'], sp['cre'], sp['cim'],
            sp['d'], sp['wglu'], sp['bglu'])
    return pl.pallas_call(
        functools.partial(_s5_kernel, nb=nb, tc=tc, ns=ns, sw=sw),
        grid=(t_len // tc,),
        in_specs=[pl.BlockSpec((rows, w), lambda i: (i, 0))] + [full(a) for a in args[1:]],
        out_specs=[pl.BlockSpec((rows, w), lambda i: (i, 0)),
                   pl.BlockSpec((nb, ncol), lambda i: (0, 0))],
        out_shape=[jax.ShapeDtypeStruct((t_len * nb, w), BF16),
                   jax.ShapeDtypeStruct((nb, ncol), F32)],
        scratch_shapes=[pltpu.VMEM((rows, ncol), F32),
                        pltpu.VMEM((ns, LANES, 2 * sw), BF16),
                        pltpu.VMEM((1, ns * sw), F32),
                        pltpu.VMEM((1, ns * sw), F32),
                        pltpu.VMEM((nb, ncol), F32)],
        compiler_params=_cparams(("arbitrary",)),
        name="s5",
    )(*args)


def _diff_lambda(lq1, lk1, lq2, lk2, lambda_init):
    return (jnp.exp(jnp.sum(lq1[...] * lk1[...], keepdims=True))
            - jnp.exp(jnp.sum(lq2[...] * lk2[...], keepdims=True)) + lambda_init)


def _attn_kernel(q_ref, k_ref, v_ref, lq1, lk1, lq2, lk2, g_ref, o_ref, *, tq, hd, lambda_init):
    t_len = q_ref.shape[0]
    lam = _diff_lambda(lq1, lk1, lq2, lk2, lambda_init)
    lo = lax.broadcasted_iota(jnp.int32, (tq, LANES), 1) < hd
    max_blocks = max(ATTN_KEY_CHUNK // tq, 1)

    for qi in range(t_len // tq):
        q = q_ref[qi * tq:(qi + 1) * tq, :]
        zero = jnp.zeros_like(q)
        qs = jnp.concatenate([jnp.where(lo, q, zero), jnp.where(lo, zero, q)], axis=0)
        m = jnp.full((1, 2 * tq), NEG_INF, F32)
        l = jnp.zeros((1, 2 * tq), F32)
        acc = jnp.zeros((LANES, 2 * tq), F32)
        n_blocks = qi + 1
        n_chunks = -(-n_blocks // max_blocks)
        k0 = 0
        for ci in range(n_chunks):
            ksz = (n_blocks // n_chunks + (1 if ci < n_blocks % n_chunks else 0)) * tq
            s = _dot_nt(k_ref[k0:k0 + ksz, :], qs)
            if ci == n_chunks - 1:
                kpos = k0 + lax.broadcasted_iota(jnp.int32, (ksz, 2 * tq), 0)
                qc = lax.broadcasted_iota(jnp.int32, (ksz, 2 * tq), 1)
                s = jnp.where(kpos <= qi * tq + jnp.where(qc >= tq, qc - tq, qc), s, NEG_INF)
            m_new = jnp.maximum(m, jnp.max(s, axis=0, keepdims=True))
            alpha = jnp.exp(m - m_new)
            p = jnp.exp(s - m_new)
            l = alpha * l + jnp.sum(p, axis=0, keepdims=True)
            pv = lax.dot_general(v_ref[k0:k0 + ksz, :], p.astype(BF16), (((0,), (0,)), ((), ())),
                                 preferred_element_type=F32)
            acc = alpha * acc + pv
            m = m_new
            k0 += ksz
        o_t = (acc[:, :tq] / l[:, :tq]) - lam * (acc[:, tq:] / l[:, tq:])
        ms = jnp.mean(o_t * o_t, axis=0, keepdims=True)
        o_t = o_t * lax.rsqrt(ms + NORM_EPS) * g_ref[...] * (1.0 - lambda_init)
        o_ref[qi * tq:(qi + 1) * tq, :] = o_t.T.astype(BF16)


def _attn(qb, kb, vb, lam_params, g, n_batch, t_len, tq, hd, lambda_init):
    n, w = qb.shape
    nh = w // LANES
    blk = pl.BlockSpec((t_len, LANES), lambda b, h: (b, h))
    small = lambda a: pl.BlockSpec(a.shape, lambda b, h: (0, 0))
    g_cols = jnp.broadcast_to(g.reshape(LANES, 1), (LANES, tq))
    return pl.pallas_call(
        functools.partial(_attn_kernel, tq=tq, hd=hd, lambda_init=lambda_init),
        grid=(n_batch, nh),
        in_specs=[blk, blk, blk] + [small(a) for a in lam_params] + [small(g_cols)],
        out_specs=blk,
        out_shape=jax.ShapeDtypeStruct((n, w), BF16),
        compiler_params=_cparams(("parallel", "parallel")),
        name="attn_prompt",
    )(qb, kb, vb, *lam_params, g_cols)


def _sattn_kernel(pt_ref, q_ref, kown_ref, vown_ref, *rest, npg, prow, dq, nh, lambda_init):
    k_refs = rest[:npg]
    v_refs = rest[npg:2 * npg]
    lq1, lk1, lq2, lk2, g_ref, o_ref, kb_ref, vb_ref, m_ref, l_ref, acc_ref = rest[2 * npg:]
    j = pl.program_id(1)
    nq = q_ref.shape[1]
    rq = nq // nh
    page = prow // nh
    q = [q_ref[0, h * rq:(h + 1) * rq, :].astype(BF16) for h in range(nh)]

    @pl.when(j == 0)
    def _init():
        m_ref[...] = jnp.full(m_ref.shape, NEG_INF, F32)
        l_ref[...] = jnp.zeros(l_ref.shape, F32)
        acc_ref[...] = jnp.zeros(acc_ref.shape, F32)

    def update(s, values):
        m_prev = m_ref[...]
        m_new = jnp.maximum(m_prev, jnp.max(s, axis=1, keepdims=True))
        alpha = jnp.exp(m_prev - m_new)
        p = jnp.exp(s - m_new)
        l_ref[...] = alpha * l_ref[...] + jnp.sum(p, axis=1, keepdims=True)
        pv = jnp.concatenate([_dot(p[h * rq:(h + 1) * rq].astype(BF16), values[h])
                              for h in range(nh)], axis=0)
        acc_ref[...] = alpha * acc_ref[...] + pv
        m_ref[...] = m_new

    for p_ in range(npg):
        for h in range(nh):
            rows = pl.ds(h, page, stride=nh)
            kb_ref[h, p_ * page:(p_ + 1) * page, :] = k_refs[p_][0, rows, :].astype(BF16)
            vb_ref[h, p_ * page:(p_ + 1) * page, :] = v_refs[p_][0, rows, :].astype(BF16)
    s = jnp.concatenate([_dot_nt(q[h], kb_ref[h]) for h in range(nh)], axis=0)
    update(s, [vb_ref[h] for h in range(nh)])

    @pl.when(j == pl.num_programs(1) - 1)
    def _fin():
        k_own = [kown_ref[0, pl.ds(h, dq, stride=nh), :].astype(BF16) for h in range(nh)]
        v_own = [vown_ref[0, pl.ds(h, dq, stride=nh), :].astype(BF16) for h in range(nh)]
        s_own = jnp.concatenate([_dot_nt(q[h], k_own[h]) for h in range(nh)], axis=0)
        r = lax.broadcasted_iota(jnp.int32, (nq, dq), 0)
        c = lax.broadcasted_iota(jnp.int32, (nq, dq), 1)
        update(jnp.where(c <= r % dq, s_own, NEG_INF), v_own)
        lam = _diff_lambda(lq1, lk1, lq2, lk2, lambda_init)
        o_all = acc_ref[...] / l_ref[...]
        for h in range(nh):
            o = o_all[h * rq:h * rq + dq] - lam * o_all[h * rq + dq:(h + 1) * rq]
            o_ref[0, h * dq:(h + 1) * dq, :] = (_rms(o, g_ref[...])
                                                * (1.0 - lambda_init)).astype(BF16)


def _sattn(page_table, q_rows, k_own, v_own, cache_k, cache_v, layer, lam_params, g, npg, dq, nh,
           lambda_init):
    nb, nq, _ = q_rows.shape
    prow = cache_k.shape[2]
    n_pages = page_table.shape[1]
    nsteps = n_pages // npg

    def page_spec(p_):
        return pl.BlockSpec((None, 1, prow, LANES),
                            lambda b, j, pt: (layer, pt[b, j * npg + p_], 0, 0))

    per_b = lambda a: pl.BlockSpec((1,) + a.shape[1:], lambda b, j, pt: (b, 0, 0))
    small = lambda a: pl.BlockSpec(a.shape, lambda b, j, pt: (0, 0))
    grid_spec = pltpu.PrefetchScalarGridSpec(
        num_scalar_prefetch=1,
        grid=(nb, nsteps),
        in_specs=([per_b(q_rows), per_b(k_own), per_b(v_own)]
                  + [page_spec(p_) for p_ in range(npg)]
                  + [page_spec(p_) for p_ in range(npg)]
                  + [small(a) for a in lam_params] + [small(g)]),
        out_specs=pl.BlockSpec((1, nq // 2, LANES), lambda b, j, pt: (b, 0, 0)),
        scratch_shapes=[pltpu.VMEM((nh, npg * prow // nh, LANES), BF16),
                        pltpu.VMEM((nh, npg * prow // nh, LANES), BF16),
                        pltpu.VMEM((nq, 1), F32),
                        pltpu.VMEM((nq, 1), F32),
                        pltpu.VMEM((nq, LANES), F32)])
    return pl.pallas_call(
        functools.partial(_sattn_kernel, npg=npg, prow=prow, dq=dq, nh=nh,
                          lambda_init=lambda_init),
        grid_spec=grid_spec,
        out_shape=jax.ShapeDtypeStruct((nb, nq // 2, LANES), BF16),
        compiler_params=_cparams(("parallel", "arbitrary")),
        name="attn_sample",
    )(page_table, q_rows, k_own, v_own, *([cache_k] * npg), *([cache_v] * npg), *lam_params, g)


def _merge_kernel(x_ref, gates_ref, y_ref, o_ref, wbs_ref, wba_ref, wout_ref, gffn_ref,
                  wr_ref, br_ref, x1_ref, h2_ref, ri_ref, rf_ref, cnt_ref, run_ref, *, ne):
    first = jnp.logical_and(pl.program_id(0) == 0, pl.program_id(1) == 0)

    @pl.when(first)
    def _init():
        run_ref[...] = jnp.zeros(run_ref.shape, F32)

    tm, d = x_ref.shape
    a = _dot(y_ref[...], wbs_ref[...])
    b = _dot(o_ref[...], wba_ref[...])
    merged = gates_ref[:, :d].astype(F32) * a + gates_ref[:, d:].astype(F32) * b
    x1 = x_ref[...] + _dot(merged.astype(BF16), wout_ref[...])
    x1_ref[...] = x1
    h2b = _rms(x1, gffn_ref[...]).astype(BF16)
    bits = lax.bitcast_convert_type(h2b.astype(F32), jnp.uint32)
    h2_ref[...] = bits[:, d // 2:] | (bits[:, :d // 2] >> 16)

    logits = _dot(h2b, wr_ref[...]) + br_ref[...]
    lane = lax.broadcasted_iota(jnp.int32, logits.shape, 1)
    lane_f = lane.astype(F32)
    cur = jnp.where(lane < ne, logits, -jnp.inf)
    vals, idxs, hots = [], [], []
    for _ in range(TOP_K):
        mk = jnp.max(cur, axis=1, keepdims=True)
        ik = jnp.min(jnp.where(cur == mk, lane_f, float(LANES)), axis=1, keepdims=True)
        hot = lane_f == ik
        cur = jnp.where(hot, -jnp.inf, cur)
        vals.append(mk)
        idxs.append(ik.astype(jnp.int32))
        hots.append(hot)
    exps = [jnp.exp(v - vals[0]) for v in vals]
    denom = sum(exps[1:], exps[0])
    chosen = functools.reduce(jnp.logical_or, hots)
    cnt = jnp.where(chosen, 1.0, 0.0)
    r = lax.broadcasted_iota(jnp.int32, (tm, tm), 0)
    c = lax.broadcasted_iota(jnp.int32, (tm, tm), 1)
    before = _dot((c < r).astype(BF16), cnt.astype(BF16)) + run_ref[...]
    ri = jnp.zeros(logits.shape, jnp.int32)
    rf = jnp.zeros(logits.shape, F32)
    for k in range(TOP_K):
        rank = jnp.sum(jnp.where(hots[k], before, 0.0), axis=1, keepdims=True).astype(jnp.int32)
        ri = jnp.where(lane == k, idxs[k], ri)
        ri = jnp.where(lane == TOP_K + k, rank, ri)
        rf = jnp.where(lane == k, exps[k] / denom, rf)
    ri_ref[...] = ri.T[:2 * TOP_K]
    rf_ref[...] = rf
    run_ref[...] = run_ref[...] + jnp.sum(cnt, axis=0, keepdims=True)
    cnt_ref[...] = run_ref[...]


def _merge(x2d, gates, y_tb, o, n_batch, t_len, tm, wbs, wba, wout, gffn, wr, br, ne):
    n, d = x2d.shape
    w = o.shape[1]
    nt = t_len // tm
    row = lambda b, t: (b * nt + t, 0)
    const = lambda b, t: (0, 0)
    return pl.pallas_call(
        functools.partial(_merge_kernel, ne=ne),
        grid=(n_batch, nt),
        in_specs=[pl.BlockSpec((tm, d), row),
                  pl.BlockSpec((tm, 2 * d), row),
                  pl.BlockSpec((tm, w), lambda b, t: (t, b)),
                  pl.BlockSpec((tm, w), row),
                  pl.BlockSpec(wbs.shape, const),
                  pl.BlockSpec(wba.shape, const),
                  pl.BlockSpec(wout.shape, const),
                  pl.BlockSpec((1, d), const),
                  pl.BlockSpec(wr.shape, const),
                  pl.BlockSpec((1, LANES), const)],
        out_specs=[pl.BlockSpec((tm, d), row),
                   pl.BlockSpec((tm, d // 2), row),
                   pl.BlockSpec((2 * TOP_K, tm), lambda b, t: (0, b * nt + t)),
                   pl.BlockSpec((tm, LANES), row),
                   pl.BlockSpec((1, LANES), const)],
        out_shape=[jax.ShapeDtypeStruct((n, d), F32),
                   jax.ShapeDtypeStruct((n, d // 2), jnp.uint32),
                   jax.ShapeDtypeStruct((2 * TOP_K, n), jnp.int32),
                   jax.ShapeDtypeStruct((n, LANES), F32),
                   jax.ShapeDtypeStruct((1, LANES), F32)],
        scratch_shapes=[pltpu.VMEM((1, LANES), F32)],
        compiler_params=_cparams(("arbitrary", "arbitrary")),
        name="merge",
    )(x2d, gates, y_tb, o, wbs, wba, wout, gffn, wr, br)


def _moe_kernel(be_ref, nu_ref, x_ref, wgu_ref, bgu_ref, wd_ref, bd_ref, o_ref, wgu_s, wd_s):
    i = pl.program_id(0)
    active = i < nu_ref[0]
    fresh = jnp.logical_or(i == 0, be_ref[i] != be_ref[jnp.maximum(i - 1, 0)])
    pc = PAIR_COLS
    hc = pc // 2
    n_chunk = wgu_ref.shape[2] // pc

    @pl.when(jnp.logical_and(active, fresh))
    def _stage_weights():
        r = lax.broadcasted_iota(jnp.int32, (pc, pc), 0)
        c = lax.broadcasted_iota(jnp.int32, (pc, pc), 1)
        perm = (r == jnp.where(c < hc, 2 * c, 2 * (c - hc) + 1)).astype(BF16)
        for j in range(n_chunk):
            wj = wgu_ref[0, :, j * pc:(j + 1) * pc].astype(BF16)
            wgu_s[:, j * pc:(j + 1) * pc] = _dot(wj, perm).astype(BF16)
        wd_s[...] = wd_ref[0].astype(BF16)

    @pl.when(active)
    def _compute():
        words = x_ref[...]
        x = jnp.concatenate(
            [lax.bitcast_convert_type(words << 16, F32),
             lax.bitcast_convert_type(words & jnp.uint32(0xFFFF0000), F32)], axis=1).astype(BF16)
        gu = _dot(x, wgu_s[...]) + bgu_ref[0]
        acts = []
        for j in range(n_chunk):
            g_lin = jnp.minimum(gu[:, j * pc:j * pc + hc], SWIGLU_LIMIT)
            up = jnp.clip(gu[:, j * pc + hc:(j + 1) * pc], -SWIGLU_LIMIT, SWIGLU_LIMIT)
            acts.append(((up + 1.0) * (g_lin * jax.nn.sigmoid(SWIGLU_ALPHA * g_lin))).astype(BF16))
        o_ref[...] = _dot(jnp.concatenate(acts, axis=1), wd_s[...]) + bd_ref[0]

    @pl.when(jnp.logical_not(active))
    def _skip():
        o_ref[...] = jnp.zeros(o_ref.shape, o_ref.dtype)


def _moe_block_rows(n_tokens, ne):
    return MOE_BLOCK_ROWS if n_tokens * TOP_K >= 8 * ne * MOE_BLOCK_ROWS else MOE_BLOCK_ROWS_SMALL


def _moe_gemm(xs, blk_expert, n_used, wgu, bgu, wd, bd, bm):
    n_rows = xs.shape[0]
    d = wd.shape[2]
    wspec = lambda a: pl.BlockSpec((1,) + a.shape[1:], lambda i, be, nu: (be[i], 0, 0))
    grid_spec = pltpu.PrefetchScalarGridSpec(
        num_scalar_prefetch=2,
        grid=(n_rows // bm,),
        in_specs=[pl.BlockSpec((bm, xs.shape[1]), lambda i, be, nu: (i, 0)),
                  wspec(wgu), wspec(bgu), wspec(wd), wspec(bd)],
        out_specs=pl.BlockSpec((bm, d), lambda i, be, nu: (i, 0)),
        scratch_shapes=[pltpu.VMEM(wgu.shape[1:], BF16),
                        pltpu.VMEM(wd.shape[1:], BF16)])
    return pl.pallas_call(
        _moe_kernel,
        grid_spec=grid_spec,
        out_shape=jax.ShapeDtypeStruct((n_rows, d), F32),
        compiler_params=_cparams(("arbitrary",)),
        name="moe_gemm",
    )(blk_expert, n_used, xs, wgu, bgu, wd, bd)


def _combine_kernel(x_ref, *rest, final_norm):
    yg_refs = rest[:TOP_K]
    gate_ref, g_ref, o_ref = rest[TOP_K:]
    acc = x_ref[...]
    gate = gate_ref[...]
    for k in range(TOP_K):
        acc = acc + gate[:, k:k + 1] * yg_refs[k][...]
    o_ref[...] = _rms(acc, g_ref[...]) if final_norm else acc


def _combine(x1, ygs, gate, g, tm, final_norm):
    n, d = x1.shape
    row = pl.BlockSpec((tm, d), lambda i: (i, 0))
    return pl.pallas_call(
        functools.partial(_combine_kernel, final_norm=final_norm),
        grid=(n // tm,),
        in_specs=[row] + [row] * TOP_K + [pl.BlockSpec((tm, LANES), lambda i: (i, 0)),
                                          pl.BlockSpec((1, d), lambda i: (0, 0))],
        out_specs=row,
        out_shape=jax.ShapeDtypeStruct((n, d), F32),
        compiler_params=_cparams(("parallel",)),
        name="combine",
    )(x1, *ygs, gate, g)


def _moe_dispatch(h2p, route_i, count_row, ne):
    n = h2p.shape[0]
    bm = _moe_block_rows(n, ne)
    m = n * TOP_K
    counts = count_row[0, :ne].astype(jnp.int32)
    start = jnp.cumsum(counts) - counts
    padded = (counts + bm - 1) // bm * bm
    pend = jnp.cumsum(padded)
    pstart = pend - padded
    dest = [pstart[route_i[k]] + route_i[TOP_K + k] for k in range(TOP_K)]
    n_rows = (m + bm - 1) // bm * bm + ne * bm
    blk_start = jnp.arange(n_rows // bm, dtype=jnp.int32) * bm
    blk_expert = jnp.minimum(
        jnp.sum((blk_start[:, None] >= pend[None, :]).astype(jnp.int32), axis=1), ne - 1)
    n_used = (pend[-1] // bm).astype(jnp.int32).reshape(1)
    tok = jnp.tile(jnp.arange(n, dtype=jnp.int32), TOP_K)
    _, tok_sorted = lax.sort_key_val(jnp.concatenate(dest), tok)
    blk_shift = (pstart - start)[blk_expert]
    src = jnp.arange(n_rows, dtype=jnp.int32) - jnp.repeat(blk_shift, bm)
    row_tok = tok_sorted[jnp.clip(src, 0, m - 1)]
    return dict(xs=h2p[row_tok], blk_expert=blk_expert, n_used=n_used, dest=dest, bm=bm)


def _moe_apply(x1, disp, route_f, mp, g_final, final_norm):
    ys = _moe_gemm(disp['xs'], disp['blk_expert'], disp['n_used'],
                   mp['wgu'], mp['bgu'], mp['wd'], mp['bd'], disp['bm'])
    ygs = [ys[disp['dest'][k]] for k in range(TOP_K)]
    tm = min(ROW_TILE, x1.shape[0])
    return _combine(x1, ygs, route_f, g_final, tm, final_norm)


def _rope_tables(pos, hd):
    half = hd // 2
    inv_freq = ROPE_THETA ** (-jnp.arange(half, dtype=F32) / half)
    ang = pos.astype(F32)[:, None] * inv_freq[None, :]
    cos = jnp.cos(ang)
    sin = jnp.sin(ang)
    reps = LANES // hd
    cos_t = jnp.tile(jnp.concatenate([cos, cos], axis=1), (1, reps))
    sin_t = jnp.tile(jnp.concatenate([-sin, sin], axis=1), (1, reps))
    return cos_t, sin_t


def _layer_params(lp):
    d = lp['w_in'].shape[0]
    ne = lp['w_router'].shape[1]
    wr = jnp.zeros((d, LANES), F32).at[:, :ne].set(lp['w_router']).astype(BF16)
    br = jnp.zeros((1, LANES), F32).at[0, :ne].set(lp['b_router'])
    return dict(
        g_mix=lp['norm_mix_g'].reshape(1, d),
        win=lp['w_in'].astype(BF16),
        wgate=lp['w_gate'].astype(BF16),
        bgate=lp['b_gate'].reshape(1, -1),
        s5=_s5_layout(lp),
        lam=[lp[k].reshape(1, -1) for k in ('lambda_q1', 'lambda_k1', 'lambda_q2', 'lambda_k2')],
        g_sub=lp['attn_subln_g'].reshape(1, -1),
        wbs=lp['w_branch_ssm'].astype(BF16),
        wba=lp['w_branch_attn'].astype(BF16),
        wout=lp['w_out'].astype(BF16),
        g_ffn=lp['norm_ffn_g'].reshape(1, d),
        wr=wr, br=br,
        moe=dict(ne=ne,
                 wgu=lp['w_gate_up'],
                 bgu=lp['b_gate_up'].reshape(ne, -1, PAIR_COLS // 2, 2).transpose(0, 1, 3, 2)
                 .reshape(ne, 1, -1),
                 wd=lp['w_down'],
                 bd=lp['b_down'][:, None, :]))


def _query_rows(qb, bsz, dq, nh, hd):
    qt = qb.astype(F32).reshape(bsz, dq, nh, 2, hd).transpose(0, 2, 3, 1, 4)
    ec = jnp.eye(2, dtype=jnp.bool_)[None, None, :, None, :, None]
    out = jnp.where(ec, qt[:, :, :, :, None, :], 0.0)
    return out.reshape(bsz, nh * 2 * dq, 2 * hd)


def _layer(x, pos, x0_re, x0_im, past, P, lambda_init):
    bsz, t_len, d = x.shape
    n = bsz * t_len
    x2d = x.reshape(n, d)
    sp = P['s5']
    g, p = x0_re.shape[1], x0_re.shape[2]
    w = P['win'].shape[1] // 4
    nh = w // LANES
    hd = LANES // 2
    cos_t, sin_t = _rope_tables(pos, hd)
    x0 = _state_to_cols(x0_re.astype(F32), x0_im.astype(F32), sp['ns'])
    if past is None:
        n_batch, rows_t, tm = bsz, t_len, min(ROW_TILE, t_len)
    else:
        n_batch, rows_t, tm = 1, n, n
        cos_t = jnp.tile(cos_t, (bsz, 1))
        sin_t = jnp.tile(sin_t, (bsz, 1))
    u, qb, k, v, kb, vb, gates = _proj(x2d, n_batch, rows_t, tm, P['g_mix'], P['win'], P['wgate'],
                                       P['bgate'], cos_t, sin_t, hd // 2, hd ** -0.5)
    if past is None:
        u_tb = u.reshape(t_len * bsz, w)
        y_tb, st = _s5(u_tb, x0, sp, bsz, t_len, min(64, t_len))
        y_in = y_tb.reshape(t_len, bsz * w)
        o = _attn(qb, kb, vb, P['lam'], P['g_sub'], bsz, t_len, min(256, t_len), hd, lambda_init)
    else:
        cache_k, cache_v, page_table, layer = past
        u_tb = u.reshape(bsz, t_len, w).transpose(1, 0, 2).reshape(t_len * bsz, w)
        y_tb, st = _s5(u_tb, x0, sp, bsz, t_len, t_len)
        y_in = y_tb.reshape(t_len, bsz, w).transpose(1, 0, 2).reshape(n, w)
        npg = math.gcd(8, page_table.shape[1])
        cshape = cache_k.shape[:2] + (cache_k.shape[2] * nh, LANES)
        o = _sattn(page_table, _query_rows(qb, bsz, t_len, nh, hd),
                   k.reshape(bsz, t_len * nh, LANES), v.reshape(bsz, t_len * nh, LANES),
                   cache_k.reshape(cshape), cache_v.reshape(cshape), layer,
                   P['lam'], P['g_sub'], npg, t_len, nh, lambda_init)
        o = o.reshape(bsz, nh, t_len, LANES).transpose(0, 2, 1, 3).reshape(n, w)
    x1, h2p, route_i, route_f, count_row = _merge(
        x2d, gates, y_in, o, n_batch, rows_t, tm, P['wbs'], P['wba'], P['wout'], P['g_ffn'],
        P['wr'], P['br'], P['moe']['ne'])
    disp = _moe_dispatch(h2p, route_i, count_row, P['moe']['ne'])
    st_re, st_im = _cols_to_state(st, sp['ns'], g, p)
    side = (k.reshape(bsz, t_len, nh, LANES), v.reshape(bsz, t_len, nh, LANES),
            st_re.astype(x.dtype), st_im.astype(x.dtype))
    return (x1, disp, route_f), side


def kernel(x_prompt, x_sample, cache_k, cache_v, state_ssm_re, state_ssm_im, page_table, norm_mix_g, w_in, ssm_lambda_re, ssm_lambda_im, ssm_log_dt, ssm_b_re, ssm_b_im, ssm_c_re, ssm_c_im, ssm_d, ssm_w_glu, ssm_b_glu, lambda_q1, lambda_k1, lambda_q2, lambda_k2, attn_subln_g, w_branch_ssm, w_branch_attn, w_gate, b_gate, w_out, norm_ffn_g, w_router, b_router, w_gate_up, b_gate_up, w_down, b_down, norm_final_g):
    depth = w_in.shape[0]
    past_len = page_table.shape[1] * cache_k.shape[2]
    pos_prompt = jnp.arange(x_prompt.shape[1], dtype=jnp.int32)
    pos_sample = past_len + jnp.arange(x_sample.shape[1], dtype=jnp.int32)
    g, p = state_ssm_re.shape[2], state_ssm_re.shape[3]
    zero_state = jnp.zeros((x_prompt.shape[0], g, p), F32)
    g_final = norm_final_g.reshape(1, -1)
    names = ('norm_mix_g', 'w_in', 'ssm_lambda_re', 'ssm_lambda_im', 'ssm_log_dt', 'ssm_b_re',
             'ssm_b_im', 'ssm_c_re', 'ssm_c_im', 'ssm_d', 'ssm_w_glu', 'ssm_b_glu', 'lambda_q1',
             'lambda_k1', 'lambda_q2', 'lambda_k2', 'attn_subln_g', 'w_branch_ssm',
             'w_branch_attn', 'w_gate', 'b_gate', 'w_out', 'norm_ffn_g', 'w_router', 'b_router',
             'w_gate_up', 'b_gate_up', 'w_down', 'b_down')
    stacked = (norm_mix_g, w_in, ssm_lambda_re, ssm_lambda_im, ssm_log_dt, ssm_b_re, ssm_b_im,
               ssm_c_re, ssm_c_im, ssm_d, ssm_w_glu, ssm_b_glu, lambda_q1, lambda_k1, lambda_q2,
               lambda_k2, attn_subln_g, w_branch_ssm, w_branch_attn, w_gate, b_gate, w_out,
               norm_ffn_g, w_router, b_router, w_gate_up, b_gate_up, w_down, b_down)
    xp, xs = x_prompt, x_sample
    outs_p, outs_s = [], []
    for l in range(depth):
        P = _layer_params({nm: a[l] for nm, a in zip(names, stacked)})
        lambda_init = 0.8 - 0.6 * math.exp(-0.3 * l)
        last = l == depth - 1
        moe_p, side_p = _layer(xp, pos_prompt, zero_state, zero_state, None, P, lambda_init)
        moe_s, side_s = _layer(xs, pos_sample, state_ssm_re[l], state_ssm_im[l],
                               (cache_k, cache_v, page_table, l), P, lambda_init)
        (x1_p, disp_p, rf_p), (x1_s, disp_s, rf_s) = moe_p, moe_s
        n_used_p, x1_s = lax.optimization_barrier((disp_p['n_used'], x1_s))
        disp_p = dict(disp_p, n_used=n_used_p)
        xp = _moe_apply(x1_p, disp_p, rf_p, P['moe'], g_final, last).reshape(xp.shape)
        xs = _moe_apply(x1_s, disp_s, rf_s, P['moe'], g_final, last).reshape(xs.shape)
        outs_p.append(side_p)
        outs_s.append(side_s)
    stack = lambda outs, i: jnp.stack([o[i] for o in outs])
    return (xp, xs, stack(outs_p, 0), stack(outs_p, 1), stack(outs_p, 2), stack(outs_p, 3),
            stack(outs_s, 0), stack(outs_s, 1), stack(outs_s, 2), stack(outs_s, 3))
```

```python
import functools
import math

import jax
import jax.numpy as jnp
from jax import lax
from jax.experimental import pallas as pl
from jax.experimental.pallas import tpu as pltpu

F32 = jnp.float32
BF16 = jnp.bfloat16

NORM_EPS = 1e-6
ROPE_THETA = 10000.0
NEG_INF = -1e30
TOP_K = 4
SWIGLU_LIMIT = 7.0
SWIGLU_ALPHA = 1.702
LANES = 128
MOE_BLOCK_ROWS = 512
MOE_BLOCK_ROWS_SMALL = 128
ROW_TILE = 512
PAIR_COLS = 256
ATTN_KEY_CHUNK = 1024
VMEM_LIMIT = 56 * 1024 * 1024


def _cparams(sem):
    return pltpu.CompilerParams(dimension_semantics=sem, vmem_limit_bytes=VMEM_LIMIT)


def _dot(a, b):
    return jnp.dot(a, b, preferred_element_type=F32)


def _dot_nt(a, b):
    return lax.dot_general(a, b, (((1,), (1,)), ((), ())), preferred_element_type=F32)


def _pack_bf16_pairs(x):
    half = x.shape[1] // 2
    bits = lax.bitcast_convert_type(x.astype(BF16).astype(F32), jnp.uint32)
    return bits[:, half:] | (bits[:, :half] >> 16)


def _unpack_bf16_pairs(words):
    return jnp.concatenate(
        [lax.bitcast_convert_type(words << 16, F32),
         lax.bitcast_convert_type(words & jnp.uint32(0xFFFF0000), F32)], axis=1)


def _rms(x, g):
    return x * lax.rsqrt(jnp.mean(x * x, axis=-1, keepdims=True) + NORM_EPS) * g


def _proj_kernel(x_ref, g_ref, win_ref, wgate_ref, bgate_ref, cos_ref, sin_ref,
                 u_ref, q_ref, k_ref, v_ref, kb_ref, vb_ref, gates_ref, *, half, scale):
    hb = _rms(x_ref[...], g_ref[...]).astype(BF16)
    proj = _dot(hb, win_ref[...])
    w = u_ref.shape[-1]
    u_ref[...] = proj[:, :w]
    cos = cos_ref[...]
    sin = sin_ref[...]
    lane = lax.broadcasted_iota(jnp.int32, cos.shape, 1)
    lo = (lane % (2 * half)) < half

    def rot(xh):
        fwd = pltpu.roll(xh, LANES - half, axis=1)
        bwd = pltpu.roll(xh, half, axis=1)
        return xh * cos + jnp.where(lo, fwd, bwd) * sin

    nh = w // LANES
    tm = x_ref.shape[0]
    for hh in range(nh):
        sl = slice(hh * LANES, (hh + 1) * LANES)
        qh = rot(proj[:, w + hh * LANES:w + (hh + 1) * LANES])
        q_ref[:, sl] = (qh * scale).astype(BF16)
        kh = rot(proj[:, 2 * w + hh * LANES:2 * w + (hh + 1) * LANES])
        k_ref[pl.ds(hh, tm, stride=nh), :] = kh
        kb_ref[:, sl] = kh.astype(BF16)
        v_ref[pl.ds(hh, tm, stride=nh), :] = proj[:, 3 * w + hh * LANES:3 * w + (hh + 1) * LANES]
    vb_ref[...] = proj[:, 3 * w:].astype(BF16)
    gl = _dot(hb, wgate_ref[...]) + bgate_ref[...]
    gates_ref[...] = jax.nn.sigmoid(gl).astype(BF16)


def _proj(x2d, n_batch, t_len, tm, g, win_b, wgate_b, bgate, cos_t, sin_t, half, scale):
    n, d = x2d.shape
    w = win_b.shape[1] // 4
    nh = w // LANES
    nt = t_len // tm
    row = lambda b, t: (b * nt + t, 0)
    const = lambda b, t: (0, 0)
    outs = pl.pallas_call(
        functools.partial(_proj_kernel, half=half, scale=scale),
        grid=(n_batch, nt),
        in_specs=[pl.BlockSpec((tm, d), row),
                  pl.BlockSpec((1, d), const),
                  pl.BlockSpec(win_b.shape, const),
                  pl.BlockSpec(wgate_b.shape, const),
                  pl.BlockSpec((1, wgate_b.shape[1]), const),
                  pl.BlockSpec((tm, LANES), lambda b, t: (t, 0)),
                  pl.BlockSpec((tm, LANES), lambda b, t: (t, 0))],
        out_specs=[pl.BlockSpec((tm, w), lambda b, t: (t, b)),
                   pl.BlockSpec((tm, w), row),
                   pl.BlockSpec((tm * nh, LANES), row),
                   pl.BlockSpec((tm * nh, LANES), row),
                   pl.BlockSpec((tm, w), row),
                   pl.BlockSpec((tm, w), row),
                   pl.BlockSpec((tm, 2 * d), row)],
        out_shape=[jax.ShapeDtypeStruct((t_len, n_batch * w), F32),
                   jax.ShapeDtypeStruct((n, w), BF16),
                   jax.ShapeDtypeStruct((n * nh, LANES), F32),
                   jax.ShapeDtypeStruct((n * nh, LANES), F32),
                   jax.ShapeDtypeStruct((n, w), BF16),
                   jax.ShapeDtypeStruct((n, w), BF16),
                   jax.ShapeDtypeStruct((n, 2 * d), BF16)],
        compiler_params=_cparams(("parallel", "parallel")),
        name="proj",
    )(x2d, g, win_b, wgate_b, bgate, cos_t, sin_t)
    return outs


def _s5_kernel(u_ref, x0_ref, lre_ref, lim_ref, ldt_ref, bre_ref, bim_ref, cre_ref, cim_ref,
               d_ref, wglu_ref, bglu_ref, y_ref, st_ref,
               zx_ref, wz_ref, are_ref, aim_ref, state_ref, us_ref, ys_ref, *, nb, tc, ns, sw):
    step = pl.program_id(0)
    w = ns * LANES

    @pl.when(step == 0)
    def _init():
        lre = lre_ref[...]
        lim = lim_ref[...]
        dt = jnp.exp(ldt_ref[...])
        mag = jnp.exp(lre * dt)
        are = mag * jnp.cos(lim * dt)
        aim = mag * jnp.sin(lim * dt)
        den = lre * lre + lim * lim
        fre = ((are - 1.0) * lre + aim * lim) / den
        fim = (aim * lre - (are - 1.0) * lim) / den
        are_ref[...] = are
        aim_ref[...] = aim
        for i in range(ns):
            fr = fre[:, i * sw:(i + 1) * sw]
            fi = fim[:, i * sw:(i + 1) * sw]
            br = bre_ref[i]
            bi = bim_ref[i]
            wz_ref[i, :, :sw] = (fr * br - fi * bi).astype(BF16)
            wz_ref[i, :, sw:] = (fr * bi + fi * br).astype(BF16)
        state_ref[...] = x0_ref[...]

    for b in range(nb):
        for c in range(ns):
            us_ref[c, pl.ds(b, tc, stride=nb), :] = u_ref[:, b * w + c * LANES:b * w + (c + 1) * LANES]
    ub = jnp.concatenate([us_ref[c] for c in range(ns)], axis=1)
    ubb = ub.astype(BF16)
    for i in range(ns):
        zx_ref[:, 2 * sw * i:2 * sw * (i + 1)] = _dot(ubb[:, LANES * i:LANES * (i + 1)], wz_ref[i])

    for i in range(ns):
        c0 = 2 * sw * i
        arb = jnp.broadcast_to(are_ref[:, i * sw:(i + 1) * sw], (nb, sw))
        aib = jnp.broadcast_to(aim_ref[:, i * sw:(i + 1) * sw], (nb, sw))

        def body(t, carry, c0=c0, arb=arb, aib=aib):
            xr, xi = carry
            r0 = pl.multiple_of(t * nb, nb)
            zr = zx_ref[pl.ds(r0, nb), c0:c0 + sw]
            zi = zx_ref[pl.ds(r0, nb), c0 + sw:c0 + 2 * sw]
            nxr = arb * xr - aib * xi + zr
            nxi = arb * xi + aib * xr + zi
            zx_ref[pl.ds(r0, nb), c0:c0 + sw] = nxr
            zx_ref[pl.ds(r0, nb), c0 + sw:c0 + 2 * sw] = nxi
            return nxr, nxi

        xr, xi = lax.fori_loop(0, tc, body,
                               (state_ref[:, c0:c0 + sw], state_ref[:, c0 + sw:c0 + 2 * sw]))
        state_ref[:, c0:c0 + sw] = xr
        state_ref[:, c0 + sw:c0 + 2 * sw] = xi

    ys = []
    for i in range(ns):
        c0 = 2 * sw * i
        xr = zx_ref[:, c0:c0 + sw].astype(BF16)
        xi = zx_ref[:, c0 + sw:c0 + 2 * sw].astype(BF16)
        ys.append(_dot(xr, cre_ref[i]) - _dot(xi, cim_ref[i]))
    y = jnp.concatenate(ys, axis=1) + d_ref[...] * ub
    y = jax.nn.gelu(y)
    gl = _dot(y.astype(BF16), wglu_ref[...]) + bglu_ref[...]
    yv = y * jax.nn.sigmoid(gl)
    for c in range(ns):
        ys_ref[c] = yv[:, c * LANES:(c + 1) * LANES]
    for b in range(nb):
        for c in range(ns):
            y_ref[:, b * w + c * LANES:b * w + (c + 1) * LANES] = (
                ys_ref[c, pl.ds(b, tc, stride=nb), :].astype(y_ref.dtype))

    @pl.when(step == pl.num_programs(0) - 1)
    def _fin():
        st_ref[...] = state_ref[...]


def _s5_layout(lp):
    g, p, h = lp['ssm_b_re'].shape
    gps = LANES // h
    ns = g // gps
    eye = jnp.eye(gps, dtype=jnp.bool_)

    def bd_b(b):
        bb = b.reshape(ns, gps, p, h).transpose(0, 1, 3, 2)
        out = jnp.where(eye[None, :, None, :, None], bb[:, :, :, None, :], 0.0)
        return out.reshape(ns, gps * h, gps * p)

    def bd_c(c):
        cc = c.reshape(ns, gps, h, p).transpose(0, 1, 3, 2)
        out = jnp.where(eye[None, :, None, :, None], cc[:, :, :, None, :], 0.0)
        return out.reshape(ns, gps * p, gps * h)

    return dict(
        lre=lp['ssm_lambda_re'].reshape(1, g * p),
        lim=lp['ssm_lambda_im'].reshape(1, g * p),
        ldt=jnp.repeat(lp['ssm_log_dt'], p).reshape(1, g * p),
        bre=bd_b(lp['ssm_b_re']), bim=bd_b(lp['ssm_b_im']),
        cre=bd_c(lp['ssm_c_re']).astype(BF16), cim=bd_c(lp['ssm_c_im']).astype(BF16),
        d=lp['ssm_d'].reshape(1, g * h),
        wglu=lp['ssm_w_glu'].astype(BF16), bglu=lp['ssm_b_glu'].reshape(1, -1),
        ns=ns, sw=gps * p)


def _state_to_cols(re, im, ns):
    b = re.shape[0]
    return jnp.stack([re.reshape(b, ns, -1), im.reshape(b, ns, -1)], axis=2).reshape(b, -1)


def _cols_to_state(st, ns, g, p):
    b = st.shape[0]
    s = st.reshape(b, ns, 2, -1)
    return s[:, :, 0].reshape(b, g, p), s[:, :, 1].reshape(b, g, p)


def _s5(u, x0, sp, nb, t_len, tc, y_dtype):
    rows = nb * tc
    ns, sw = sp['ns'], sp['sw']
    ncol = ns * 2 * sw
    full = lambda a: pl.BlockSpec(a.shape, lambda i: (0,) * a.ndim)
    args = (u, x0, sp['lre'], sp['lim'], sp['ldt'], sp['bre'], sp['bim'], sp['cre'], sp['cim'],
            sp['d'], sp['wglu'], sp['bglu'])
    return pl.pallas_call(
        functools.partial(_s5_kernel, nb=nb, tc=tc, ns=ns, sw=sw),
        grid=(t_len // tc,),
        in_specs=[pl.BlockSpec((tc, u.shape[1]), lambda i: (i, 0))] + [full(a) for a in args[1:]],
        out_specs=[pl.BlockSpec((tc, u.shape[1]), lambda i: (i, 0)),
                   pl.BlockSpec((nb, ncol), lambda i: (0, 0))],
        out_shape=[jax.ShapeDtypeStruct(u.shape, y_dtype),
                   jax.ShapeDtypeStruct((nb, ncol), F32)],
        scratch_shapes=[pltpu.VMEM((rows, ncol), F32),
                        pltpu.VMEM((ns, LANES, 2 * sw), BF16),
                        pltpu.VMEM((1, ns * sw), F32),
                        pltpu.VMEM((1, ns * sw), F32),
                        pltpu.VMEM((nb, ncol), F32),
                        pltpu.VMEM((ns, rows, LANES), F32),
                        pltpu.VMEM((ns, rows, LANES), F32)],
        compiler_params=_cparams(("arbitrary",)),
        name="s5",
    )(*args)


def _diff_lambda(lq1, lk1, lq2, lk2, lambda_init):
    return (jnp.exp(jnp.sum(lq1[...] * lk1[...], keepdims=True))
            - jnp.exp(jnp.sum(lq2[...] * lk2[...], keepdims=True)) + lambda_init)


def _attn_kernel(q_ref, k_ref, v_ref, lq1, lk1, lq2, lk2, g_ref, o_ref, *, tq, hd, lambda_init):
    t_len = q_ref.shape[0]
    lam = _diff_lambda(lq1, lk1, lq2, lk2, lambda_init)
    lo = lax.broadcasted_iota(jnp.int32, (tq, LANES), 1) < hd
    max_blocks = max(ATTN_KEY_CHUNK // tq, 1)

    for qi in range(t_len // tq):
        q = q_ref[qi * tq:(qi + 1) * tq, :]
        zero = jnp.zeros_like(q)
        qs = jnp.concatenate([jnp.where(lo, q, zero), jnp.where(lo, zero, q)], axis=0)
        m = jnp.full((1, 2 * tq), NEG_INF, F32)
        l = jnp.zeros((1, 2 * tq), F32)
        acc = jnp.zeros((LANES, 2 * tq), F32)
        n_blocks = qi + 1
        n_chunks = -(-n_blocks // max_blocks)
        k0 = 0
        for ci in range(n_chunks):
            ksz = (n_blocks // n_chunks + (1 if ci < n_blocks % n_chunks else 0)) * tq
            s = _dot_nt(k_ref[k0:k0 + ksz, :], qs)
            if ci == n_chunks - 1:
                kpos = k0 + lax.broadcasted_iota(jnp.int32, (ksz, 2 * tq), 0)
                qc = lax.broadcasted_iota(jnp.int32, (ksz, 2 * tq), 1)
                s = jnp.where(kpos <= qi * tq + jnp.where(qc >= tq, qc - tq, qc), s, NEG_INF)
            m_new = jnp.maximum(m, jnp.max(s, axis=0, keepdims=True))
            alpha = jnp.exp(m - m_new)
            p = jnp.exp(s - m_new)
            l = alpha * l + jnp.sum(p, axis=0, keepdims=True)
            pv = lax.dot_general(v_ref[k0:k0 + ksz, :], p.astype(BF16), (((0,), (0,)), ((), ())),
                                 preferred_element_type=F32)
            acc = alpha * acc + pv
            m = m_new
            k0 += ksz
        o_t = (acc[:, :tq] / l[:, :tq]) - lam * (acc[:, tq:] / l[:, tq:])
        ms = jnp.mean(o_t * o_t, axis=0, keepdims=True)
        o_t = o_t * lax.rsqrt(ms + NORM_EPS) * g_ref[...] * (1.0 - lambda_init)
        o_ref[qi * tq:(qi + 1) * tq, :] = o_t.T.astype(BF16)


def _attn(qb, kb, vb, lam_params, g, n_batch, t_len, tq, hd, lambda_init):
    n, w = qb.shape
    nh = w // LANES
    blk = pl.BlockSpec((t_len, LANES), lambda b, h: (b, h))
    small = lambda a: pl.BlockSpec(a.shape, lambda b, h: (0, 0))
    g_cols = jnp.broadcast_to(g.reshape(LANES, 1), (LANES, tq))
    return pl.pallas_call(
        functools.partial(_attn_kernel, tq=tq, hd=hd, lambda_init=lambda_init),
        grid=(n_batch, nh),
        in_specs=[blk, blk, blk] + [small(a) for a in lam_params] + [small(g_cols)],
        out_specs=blk,
        out_shape=jax.ShapeDtypeStruct((n, w), BF16),
        compiler_params=_cparams(("parallel", "parallel")),
        name="attn_prompt",
    )(qb, kb, vb, *lam_params, g_cols)


def _sattn_kernel(pt_ref, q_ref, kown_ref, vown_ref, *rest, npg, prow, dq, nh, lambda_init):
    k_refs = rest[:npg]
    v_refs = rest[npg:2 * npg]
    lq1, lk1, lq2, lk2, g_ref, o_ref, kb_ref, vb_ref, m_ref, l_ref, acc_ref = rest[2 * npg:]
    j = pl.program_id(1)
    nq = q_ref.shape[1]
    rq = nq // nh
    page = prow // nh
    q = [q_ref[0, h * rq:(h + 1) * rq, :].astype(BF16) for h in range(nh)]

    @pl.when(j == 0)
    def _init():
        m_ref[...] = jnp.full(m_ref.shape, NEG_INF, F32)
        l_ref[...] = jnp.zeros(l_ref.shape, F32)
        acc_ref[...] = jnp.zeros(acc_ref.shape, F32)

    def update(s, values):
        m_prev = m_ref[...]
        m_new = jnp.maximum(m_prev, jnp.max(s, axis=1, keepdims=True))
        alpha = jnp.exp(m_prev - m_new)
        p = jnp.exp(s - m_new)
        l_ref[...] = alpha * l_ref[...] + jnp.sum(p, axis=1, keepdims=True)
        pv = jnp.concatenate([_dot(p[h * rq:(h + 1) * rq].astype(BF16), values[h])
                              for h in range(nh)], axis=0)
        acc_ref[...] = alpha * acc_ref[...] + pv
        m_ref[...] = m_new

    for p_ in range(npg):
        for h in range(nh):
            rows = pl.ds(h, page, stride=nh)
            kb_ref[h, p_ * page:(p_ + 1) * page, :] = k_refs[p_][0, rows, :].astype(BF16)
            vb_ref[h, p_ * page:(p_ + 1) * page, :] = v_refs[p_][0, rows, :].astype(BF16)
    s = jnp.concatenate([_dot_nt(q[h], kb_ref[h]) for h in range(nh)], axis=0)
    update(s, [vb_ref[h] for h in range(nh)])

    @pl.when(j == pl.num_programs(1) - 1)
    def _fin():
        k_own = [kown_ref[0, pl.ds(h, dq, stride=nh), :].astype(BF16) for h in range(nh)]
        v_own = [vown_ref[0, pl.ds(h, dq, stride=nh), :].astype(BF16) for h in range(nh)]
        s_own = jnp.concatenate([_dot_nt(q[h], k_own[h]) for h in range(nh)], axis=0)
        r = lax.broadcasted_iota(jnp.int32, (nq, dq), 0)
        c = lax.broadcasted_iota(jnp.int32, (nq, dq), 1)
        update(jnp.where(c <= r % dq, s_own, NEG_INF), v_own)
        lam = _diff_lambda(lq1, lk1, lq2, lk2, lambda_init)
        o_all = acc_ref[...] / l_ref[...]
        for h in range(nh):
            o = o_all[h * rq:h * rq + dq] - lam * o_all[h * rq + dq:(h + 1) * rq]
            o_ref[0, h * dq:(h + 1) * dq, :] = (_rms(o, g_ref[...])
                                                * (1.0 - lambda_init)).astype(BF16)


def _sattn(page_table, q_rows, k_own, v_own, cache_k, cache_v, layer, lam_params, g, npg, dq, nh,
           lambda_init):
    nb, nq, _ = q_rows.shape
    prow = cache_k.shape[2]
    n_pages = page_table.shape[1]
    nsteps = n_pages // npg

    def page_spec(p_):
        return pl.BlockSpec((None, 1, prow, LANES),
                            lambda b, j, pt: (layer, pt[b, j * npg + p_], 0, 0))

    per_b = lambda a: pl.BlockSpec((1,) + a.shape[1:], lambda b, j, pt: (b, 0, 0))
    small = lambda a: pl.BlockSpec(a.shape, lambda b, j, pt: (0, 0))
    grid_spec = pltpu.PrefetchScalarGridSpec(
        num_scalar_prefetch=1,
        grid=(nb, nsteps),
        in_specs=([per_b(q_rows), per_b(k_own), per_b(v_own)]
                  + [page_spec(p_) for p_ in range(npg)]
                  + [page_spec(p_) for p_ in range(npg)]
                  + [small(a) for a in lam_params] + [small(g)]),
        out_specs=pl.BlockSpec((1, nq // 2, LANES), lambda b, j, pt: (b, 0, 0)),
        scratch_shapes=[pltpu.VMEM((nh, npg * prow // nh, LANES), BF16),
                        pltpu.VMEM((nh, npg * prow // nh, LANES), BF16),
                        pltpu.VMEM((nq, 1), F32),
                        pltpu.VMEM((nq, 1), F32),
                        pltpu.VMEM((nq, LANES), F32)])
    return pl.pallas_call(
        functools.partial(_sattn_kernel, npg=npg, prow=prow, dq=dq, nh=nh,
                          lambda_init=lambda_init),
        grid_spec=grid_spec,
        out_shape=jax.ShapeDtypeStruct((nb, nq // 2, LANES), BF16),
        compiler_params=_cparams(("parallel", "arbitrary")),
        name="attn_sample",
    )(page_table, q_rows, k_own, v_own, *([cache_k] * npg), *([cache_v] * npg), *lam_params, g)


def _merge_kernel(x_ref, gates_ref, y_ref, o_ref, wbs_ref, wba_ref, wout_ref, gffn_ref,
                  wr_ref, br_ref, x1_ref, h2_ref, ri_ref, rf_ref, cnt_ref, run_ref, *, ne):
    first = jnp.logical_and(pl.program_id(0) == 0, pl.program_id(1) == 0)

    @pl.when(first)
    def _init():
        run_ref[...] = jnp.zeros(run_ref.shape, F32)

    tm, d = x_ref.shape
    a = _dot(y_ref[...].astype(BF16), wbs_ref[...])
    b = _dot(o_ref[...], wba_ref[...])
    merged = gates_ref[:, :d].astype(F32) * a + gates_ref[:, d:].astype(F32) * b
    x1 = x_ref[...] + _dot(merged.astype(BF16), wout_ref[...])
    x1_ref[...] = x1
    h2 = _rms(x1, gffn_ref[...])
    h2b = h2.astype(BF16)
    h2_ref[...] = _pack_bf16_pairs(h2)

    logits = _dot(h2b, wr_ref[...]) + br_ref[...]
    lane = lax.broadcasted_iota(jnp.int32, logits.shape, 1)
    lane_f = lane.astype(F32)
    cur = jnp.where(lane < ne, logits, -jnp.inf)
    vals, idxs, hots = [], [], []
    for _ in range(TOP_K):
        mk = jnp.max(cur, axis=1, keepdims=True)
        ik = jnp.min(jnp.where(cur == mk, lane_f, float(LANES)), axis=1, keepdims=True)
        hot = lane_f == ik
        cur = jnp.where(hot, -jnp.inf, cur)
        vals.append(mk)
        idxs.append(ik.astype(jnp.int32))
        hots.append(hot)
    exps = [jnp.exp(v - vals[0]) for v in vals]
    denom = sum(exps[1:], exps[0])
    chosen = functools.reduce(jnp.logical_or, hots)
    cnt = jnp.where(chosen, 1.0, 0.0)
    r = lax.broadcasted_iota(jnp.int32, (tm, tm), 0)
    c = lax.broadcasted_iota(jnp.int32, (tm, tm), 1)
    before = _dot((c < r).astype(BF16), cnt.astype(BF16)) + run_ref[...]
    ri = jnp.zeros(logits.shape, jnp.int32)
    rf = jnp.zeros(logits.shape, F32)
    for k in range(TOP_K):
        rank = jnp.sum(jnp.where(hots[k], before, 0.0), axis=1, keepdims=True).astype(jnp.int32)
        ri = jnp.where(lane == k, idxs[k], ri)
        ri = jnp.where(lane == TOP_K + k, rank, ri)
        rf = jnp.where(lane == k, exps[k] / denom, rf)
    ri_ref[...] = ri.T[:2 * TOP_K]
    rf_ref[...] = rf
    run_ref[...] = run_ref[...] + jnp.sum(cnt, axis=0, keepdims=True)
    cnt_ref[...] = run_ref[...]


def _merge(x2d, gates, y_tb, o, n_batch, t_len, tm, wbs, wba, wout, gffn, wr, br, ne):
    n, d = x2d.shape
    w = o.shape[1]
    nt = t_len // tm
    row = lambda b, t: (b * nt + t, 0)
    const = lambda b, t: (0, 0)
    return pl.pallas_call(
        functools.partial(_merge_kernel, ne=ne),
        grid=(n_batch, nt),
        in_specs=[pl.BlockSpec((tm, d), row),
                  pl.BlockSpec((tm, 2 * d), row),
                  pl.BlockSpec((tm, w), lambda b, t: (t, b)),
                  pl.BlockSpec((tm, w), row),
                  pl.BlockSpec(wbs.shape, const),
                  pl.BlockSpec(wba.shape, const),
                  pl.BlockSpec(wout.shape, const),
                  pl.BlockSpec((1, d), const),
                  pl.BlockSpec(wr.shape, const),
                  pl.BlockSpec((1, LANES), const)],
        out_specs=[pl.BlockSpec((tm, d), row),
                   pl.BlockSpec((tm, d // 2), row),
                   pl.BlockSpec((2 * TOP_K, tm), lambda b, t: (0, b * nt + t)),
                   pl.BlockSpec((tm, LANES), row),
                   pl.BlockSpec((1, LANES), const)],
        out_shape=[jax.ShapeDtypeStruct((n, d), F32),
                   jax.ShapeDtypeStruct((n, d // 2), jnp.uint32),
                   jax.ShapeDtypeStruct((2 * TOP_K, n), jnp.int32),
                   jax.ShapeDtypeStruct((n, LANES), F32),
                   jax.ShapeDtypeStruct((1, LANES), F32)],
        scratch_shapes=[pltpu.VMEM((1, LANES), F32)],
        compiler_params=_cparams(("arbitrary", "arbitrary")),
        name="merge",
    )(x2d, gates, y_tb, o, wbs, wba, wout, gffn, wr, br)


def _moe_kernel(be_ref, nu_ref, x_ref, wgu_ref, bgu_ref, wd_ref, bd_ref, o_ref, wgu_s, wd_s):
    i = pl.program_id(0)
    active = i < nu_ref[0]
    fresh = jnp.logical_or(i == 0, be_ref[i] != be_ref[jnp.maximum(i - 1, 0)])
    pc = PAIR_COLS
    hc = pc // 2
    n_chunk = wgu_ref.shape[2] // pc

    @pl.when(jnp.logical_and(active, fresh))
    def _stage_weights():
        r = lax.broadcasted_iota(jnp.int32, (pc, pc), 0)
        c = lax.broadcasted_iota(jnp.int32, (pc, pc), 1)
        perm = (r == jnp.where(c < hc, 2 * c, 2 * (c - hc) + 1)).astype(BF16)
        for j in range(n_chunk):
            wj = wgu_ref[0, :, j * pc:(j + 1) * pc].astype(BF16)
            wgu_s[:, j * pc:(j + 1) * pc] = _dot(wj, perm).astype(BF16)
        wd_s[...] = wd_ref[0].astype(BF16)

    @pl.when(active)
    def _compute():
        x = _unpack_bf16_pairs(x_ref[...]).astype(BF16)
        gu = _dot(x, wgu_s[...]) + bgu_ref[0]
        acts = []
        for j in range(n_chunk):
            g_lin = jnp.minimum(gu[:, j * pc:j * pc + hc], SWIGLU_LIMIT)
            up = jnp.clip(gu[:, j * pc + hc:(j + 1) * pc], -SWIGLU_LIMIT, SWIGLU_LIMIT)
            acts.append(((up + 1.0) * (g_lin * jax.nn.sigmoid(SWIGLU_ALPHA * g_lin))).astype(BF16))
        o_ref[...] = _pack_bf16_pairs(_dot(jnp.concatenate(acts, axis=1), wd_s[...]) + bd_ref[0])

    @pl.when(jnp.logical_not(active))
    def _skip():
        o_ref[...] = jnp.zeros(o_ref.shape, o_ref.dtype)


def _moe_block_rows(n_tokens, ne):
    return MOE_BLOCK_ROWS if n_tokens * TOP_K >= 8 * ne * MOE_BLOCK_ROWS else MOE_BLOCK_ROWS_SMALL


def _moe_gemm(xs, blk_expert, n_used, wgu, bgu, wd, bd, bm):
    n_rows = xs.shape[0]
    d = wd.shape[2]
    wspec = lambda a: pl.BlockSpec((1,) + a.shape[1:], lambda i, be, nu: (be[i], 0, 0))
    grid_spec = pltpu.PrefetchScalarGridSpec(
        num_scalar_prefetch=2,
        grid=(n_rows // bm,),
        in_specs=[pl.BlockSpec((bm, xs.shape[1]), lambda i, be, nu: (i, 0)),
                  wspec(wgu), wspec(bgu), wspec(wd), wspec(bd)],
        out_specs=pl.BlockSpec((bm, d // 2), lambda i, be, nu: (i, 0)),
        scratch_shapes=[pltpu.VMEM(wgu.shape[1:], BF16),
                        pltpu.VMEM(wd.shape[1:], BF16)])
    return pl.pallas_call(
        _moe_kernel,
        grid_spec=grid_spec,
        out_shape=jax.ShapeDtypeStruct((n_rows, d // 2), jnp.uint32),
        compiler_params=_cparams(("arbitrary",)),
        name="moe_gemm",
    )(blk_expert, n_used, xs, wgu, bgu, wd, bd)


def _combine_kernel(x_ref, *rest, final_norm):
    yg_refs = rest[:TOP_K]
    gate_ref, g_ref, o_ref = rest[TOP_K:]
    acc = x_ref[...]
    gate = gate_ref[...]
    for k in range(TOP_K):
        acc = acc + gate[:, k:k + 1] * _unpack_bf16_pairs(yg_refs[k][...])
    o_ref[...] = _rms(acc, g_ref[...]) if final_norm else acc


def _combine(x1, ygs, gate, g, tm, final_norm):
    n, d = x1.shape
    row = pl.BlockSpec((tm, d), lambda i: (i, 0))
    packed = pl.BlockSpec((tm, d // 2), lambda i: (i, 0))
    return pl.pallas_call(
        functools.partial(_combine_kernel, final_norm=final_norm),
        grid=(n // tm,),
        in_specs=[row] + [packed] * TOP_K + [pl.BlockSpec((tm, LANES), lambda i: (i, 0)),
                                          pl.BlockSpec((1, d), lambda i: (0, 0))],
        out_specs=row,
        out_shape=jax.ShapeDtypeStruct((n, d), F32),
        compiler_params=_cparams(("parallel",)),
        name="combine",
    )(x1, *ygs, gate, g)


def _moe_dispatch(h2p, route_i, count_row, ne):
    n = h2p.shape[0]
    bm = _moe_block_rows(n, ne)
    m = n * TOP_K
    counts = count_row[0, :ne].astype(jnp.int32)
    start = jnp.cumsum(counts) - counts
    padded = (counts + bm - 1) // bm * bm
    pend = jnp.cumsum(padded)
    pstart = pend - padded
    dest = [pstart[route_i[k]] + route_i[TOP_K + k] for k in range(TOP_K)]
    n_rows = (m + bm - 1) // bm * bm + ne * bm
    blk_start = jnp.arange(n_rows // bm, dtype=jnp.int32) * bm
    blk_expert = jnp.minimum(
        jnp.sum((blk_start[:, None] >= pend[None, :]).astype(jnp.int32), axis=1), ne - 1)
    n_used = (pend[-1] // bm).astype(jnp.int32).reshape(1)
    tok = jnp.tile(jnp.arange(n, dtype=jnp.int32), TOP_K)
    _, tok_sorted = lax.sort_key_val(jnp.concatenate(dest), tok)
    blk_shift = (pstart - start)[blk_expert]
    src = jnp.arange(n_rows, dtype=jnp.int32) - jnp.repeat(blk_shift, bm)
    row_tok = tok_sorted[jnp.clip(src, 0, m - 1)]
    return dict(xs=h2p[row_tok], blk_expert=blk_expert, n_used=n_used, dest=dest, bm=bm)


def _moe_apply(x1, disp, route_f, mp, g_final, final_norm):
    ys = _moe_gemm(disp['xs'], disp['blk_expert'], disp['n_used'],
                   mp['wgu'], mp['bgu'], mp['wd'], mp['bd'], disp['bm'])
    ygs = [ys[disp['dest'][k]] for k in range(TOP_K)]
    tm = min(ROW_TILE, x1.shape[0])
    return _combine(x1, ygs, route_f, g_final, tm, final_norm)


def _rope_tables(pos, hd):
    half = hd // 2
    inv_freq = ROPE_THETA ** (-jnp.arange(half, dtype=F32) / half)
    ang = pos.astype(F32)[:, None] * inv_freq[None, :]
    cos = jnp.cos(ang)
    sin = jnp.sin(ang)
    reps = LANES // hd
    cos_t = jnp.tile(jnp.concatenate([cos, cos], axis=1), (1, reps))
    sin_t = jnp.tile(jnp.concatenate([-sin, sin], axis=1), (1, reps))
    return cos_t, sin_t


def _layer_params(lp):
    d = lp['w_in'].shape[0]
    ne = lp['w_router'].shape[1]
    wr = jnp.zeros((d, LANES), F32).at[:, :ne].set(lp['w_router']).astype(BF16)
    br = jnp.zeros((1, LANES), F32).at[0, :ne].set(lp['b_router'])
    return dict(
        g_mix=lp['norm_mix_g'].reshape(1, d),
        win=lp['w_in'].astype(BF16),
        wgate=lp['w_gate'].astype(BF16),
        bgate=lp['b_gate'].reshape(1, -1),
        s5=_s5_layout(lp),
        lam=[lp[k].reshape(1, -1) for k in ('lambda_q1', 'lambda_k1', 'lambda_q2', 'lambda_k2')],
        g_sub=lp['attn_subln_g'].reshape(1, -1),
        wbs=lp['w_branch_ssm'].astype(BF16),
        wba=lp['w_branch_attn'].astype(BF16),
        wout=lp['w_out'].astype(BF16),
        g_ffn=lp['norm_ffn_g'].reshape(1, d),
        wr=wr, br=br,
        moe=dict(ne=ne,
                 wgu=lp['w_gate_up'],
                 bgu=lp['b_gate_up'].reshape(ne, -1, PAIR_COLS // 2, 2).transpose(0, 1, 3, 2)
                 .reshape(ne, 1, -1),
                 wd=lp['w_down'],
                 bd=lp['b_down'][:, None, :]))


def _query_rows(qb, bsz, dq, nh, hd):
    qt = qb.astype(F32).reshape(bsz, dq, nh, 2, hd).transpose(0, 2, 3, 1, 4)
    ec = jnp.eye(2, dtype=jnp.bool_)[None, None, :, None, :, None]
    out = jnp.where(ec, qt[:, :, :, :, None, :], 0.0)
    return out.reshape(bsz, nh * 2 * dq, 2 * hd)


def _layer(x, pos, x0_re, x0_im, past, P, lambda_init):
    bsz, t_len, d = x.shape
    n = bsz * t_len
    x2d = x.reshape(n, d)
    sp = P['s5']
    g, p = x0_re.shape[1], x0_re.shape[2]
    w = P['win'].shape[1] // 4
    nh = w // LANES
    hd = LANES // 2
    cos_t, sin_t = _rope_tables(pos, hd)
    x0 = _state_to_cols(x0_re.astype(F32), x0_im.astype(F32), sp['ns'])
    if past is None:
        n_batch, rows_t, tm = bsz, t_len, min(ROW_TILE, t_len)
    else:
        n_batch, rows_t, tm = 1, n, n
        cos_t = jnp.tile(cos_t, (bsz, 1))
        sin_t = jnp.tile(sin_t, (bsz, 1))
    u, qb, k, v, kb, vb, gates = _proj(x2d, n_batch, rows_t, tm, P['g_mix'], P['win'], P['wgate'],
                                       P['bgate'], cos_t, sin_t, hd // 2, hd ** -0.5)
    if past is None:
        y_in, st = _s5(u, x0, sp, bsz, t_len, min(64, t_len), BF16)
        o = _attn(qb, kb, vb, P['lam'], P['g_sub'], bsz, t_len, min(256, t_len), hd, lambda_init)
    else:
        cache_k, cache_v, page_table, layer = past
        u_tb = u.reshape(bsz, t_len, w).transpose(1, 0, 2).reshape(t_len, bsz * w)
        y_tb, st = _s5(u_tb, x0, sp, bsz, t_len, t_len, F32)
        y_in = y_tb.reshape(t_len, bsz, w).transpose(1, 0, 2).reshape(n, w)
        npg = math.gcd(8, page_table.shape[1])
        cshape = cache_k.shape[:2] + (cache_k.shape[2] * nh, LANES)
        o = _sattn(page_table, _query_rows(qb, bsz, t_len, nh, hd),
                   k.reshape(bsz, t_len * nh, LANES), v.reshape(bsz, t_len * nh, LANES),
                   cache_k.reshape(cshape), cache_v.reshape(cshape), layer,
                   P['lam'], P['g_sub'], npg, t_len, nh, lambda_init)
        o = o.reshape(bsz, nh, t_len, LANES).transpose(0, 2, 1, 3).reshape(n, w)
    x1, h2p, route_i, route_f, count_row = _merge(
        x2d, gates, y_in, o, n_batch, rows_t, tm, P['wbs'], P['wba'], P['wout'], P['g_ffn'],
        P['wr'], P['br'], P['moe']['ne'])
    disp = _moe_dispatch(h2p, route_i, count_row, P['moe']['ne'])
    st_re, st_im = _cols_to_state(st, sp['ns'], g, p)
    side = (k.reshape(bsz, t_len, nh, LANES), v.reshape(bsz, t_len, nh, LANES),
            st_re.astype(x.dtype), st_im.astype(x.dtype))
    return (x1, disp, route_f), side


def kernel(x_prompt, x_sample, cache_k, cache_v, state_ssm_re, state_ssm_im, page_table, norm_mix_g, w_in, ssm_lambda_re, ssm_lambda_im, ssm_log_dt, ssm_b_re, ssm_b_im, ssm_c_re, ssm_c_im, ssm_d, ssm_w_glu, ssm_b_glu, lambda_q1, lambda_k1, lambda_q2, lambda_k2, attn_subln_g, w_branch_ssm, w_branch_attn, w_gate, b_gate, w_out, norm_ffn_g, w_router, b_router, w_gate_up, b_gate_up, w_down, b_down, norm_final_g):
    depth = w_in.shape[0]
    past_len = page_table.shape[1] * cache_k.shape[2]
    pos_prompt = jnp.arange(x_prompt.shape[1], dtype=jnp.int32)
    pos_sample = past_len + jnp.arange(x_sample.shape[1], dtype=jnp.int32)
    g, p = state_ssm_re.shape[2], state_ssm_re.shape[3]
    zero_state = jnp.zeros((x_prompt.shape[0], g, p), F32)
    g_final = norm_final_g.reshape(1, -1)
    names = ('norm_mix_g', 'w_in', 'ssm_lambda_re', 'ssm_lambda_im', 'ssm_log_dt', 'ssm_b_re',
             'ssm_b_im', 'ssm_c_re', 'ssm_c_im', 'ssm_d', 'ssm_w_glu', 'ssm_b_glu', 'lambda_q1',
             'lambda_k1', 'lambda_q2', 'lambda_k2', 'attn_subln_g', 'w_branch_ssm',
             'w_branch_attn', 'w_gate', 'b_gate', 'w_out', 'norm_ffn_g', 'w_router', 'b_router',
             'w_gate_up', 'b_gate_up', 'w_down', 'b_down')
    stacked = (norm_mix_g, w_in, ssm_lambda_re, ssm_lambda_im, ssm_log_dt, ssm_b_re, ssm_b_im,
               ssm_c_re, ssm_c_im, ssm_d, ssm_w_glu, ssm_b_glu, lambda_q1, lambda_k1, lambda_q2,
               lambda_k2, attn_subln_g, w_branch_ssm, w_branch_attn, w_gate, b_gate, w_out,
               norm_ffn_g, w_router, b_router, w_gate_up, b_gate_up, w_down, b_down)
    xp, xs = x_prompt, x_sample
    outs_p, outs_s = [], []
    for l in range(depth):
        P = _layer_params({nm: a[l] for nm, a in zip(names, stacked)})
        lambda_init = 0.8 - 0.6 * math.exp(-0.3 * l)
        last = l == depth - 1
        moe_p, side_p = _layer(xp, pos_prompt, zero_state, zero_state, None, P, lambda_init)
        moe_s, side_s = _layer(xs, pos_sample, state_ssm_re[l], state_ssm_im[l],
                               (cache_k, cache_v, page_table, l), P, lambda_init)
        (x1_p, disp_p, rf_p), (x1_s, disp_s, rf_s) = moe_p, moe_s
        n_used_p, x1_s = lax.optimization_barrier((disp_p['n_used'], x1_s))
        disp_p = dict(disp_p, n_used=n_used_p)
        xp = _moe_apply(x1_p, disp_p, rf_p, P['moe'], g_final, last).reshape(xp.shape)
        xs = _moe_apply(x1_s, disp_s, rf_s, P['moe'], g_final, last).reshape(xs.shape)
        outs_p.append(side_p)
        outs_s.append(side_s)
    stack = lambda outs, i: jnp.stack([o[i] for o in outs])
    return (xp, xs, stack(outs_p, 0), stack(outs_p, 1), stack(outs_p, 2), stack(outs_p, 3),
            stack(outs_s, 0), stack(outs_s, 1), stack(outs_s, 2), stack(outs_s, 3))
```

```python
import functools
import math

import jax
import jax.numpy as jnp
from jax import lax
from jax.experimental import pallas as pl
from jax.experimental.pallas import tpu as pltpu

F32 = jnp.float32
BF16 = jnp.bfloat16

NORM_EPS = 1e-6
ROPE_THETA = 10000.0
NEG_INF = -1e30
TOP_K = 4
SWIGLU_LIMIT = 7.0
SWIGLU_ALPHA = 1.702
LANES = 128
MOE_BLOCK_ROWS = 512
MOE_BLOCK_ROWS_SMALL = 128
ROW_TILE = 512
PAIR_COLS = 256
ATTN_KEY_CHUNK = 1024
PAGE_RING = 3
VMEM_LIMIT = 56 * 1024 * 1024


def _cparams(sem):
    return pltpu.CompilerParams(dimension_semantics=sem, vmem_limit_bytes=VMEM_LIMIT)


def _dot(a, b):
    return jnp.dot(a, b, preferred_element_type=F32)


def _dot_nt(a, b):
    return lax.dot_general(a, b, (((1,), (1,)), ((), ())), preferred_element_type=F32)


def _pack_bf16_pairs(x):
    half = x.shape[1] // 2
    bits = lax.bitcast_convert_type(x.astype(BF16).astype(F32), jnp.uint32)
    return bits[:, half:] | (bits[:, :half] >> 16)


def _unpack_bf16_pairs(words):
    return jnp.concatenate(
        [lax.bitcast_convert_type(words << 16, F32),
         lax.bitcast_convert_type(words & jnp.uint32(0xFFFF0000), F32)], axis=1)


def _rms(x, g):
    return x * lax.rsqrt(jnp.mean(x * x, axis=-1, keepdims=True) + NORM_EPS) * g


def _proj_kernel(x_ref, g_ref, win_ref, wgate_ref, bgate_ref, cos_ref, sin_ref,
                 u_ref, q_ref, k_ref, v_ref, kb_ref, vb_ref, gates_ref, *, half, scale):
    hb = _rms(x_ref[...], g_ref[...]).astype(BF16)
    proj = _dot(hb, win_ref[...])
    w = u_ref.shape[-1]
    u_ref[...] = proj[:, :w]
    cos = cos_ref[...]
    sin = sin_ref[...]
    lane = lax.broadcasted_iota(jnp.int32, cos.shape, 1)
    lo = (lane % (2 * half)) < half

    def rot(xh):
        fwd = pltpu.roll(xh, LANES - half, axis=1)
        bwd = pltpu.roll(xh, half, axis=1)
        return xh * cos + jnp.where(lo, fwd, bwd) * sin

    nh = w // LANES
    tm = x_ref.shape[0]
    for hh in range(nh):
        sl = slice(hh * LANES, (hh + 1) * LANES)
        qh = rot(proj[:, w + hh * LANES:w + (hh + 1) * LANES])
        q_ref[:, sl] = (qh * scale).astype(BF16)
        kh = rot(proj[:, 2 * w + hh * LANES:2 * w + (hh + 1) * LANES])
        k_ref[pl.ds(hh, tm, stride=nh), :] = kh
        kb_ref[:, sl] = kh.astype(BF16)
        v_ref[pl.ds(hh, tm, stride=nh), :] = proj[:, 3 * w + hh * LANES:3 * w + (hh + 1) * LANES]
    vb_ref[...] = proj[:, 3 * w:].astype(BF16)
    gl = _dot(hb, wgate_ref[...]) + bgate_ref[...]
    gates_ref[...] = jax.nn.sigmoid(gl).astype(BF16)


def _proj(x2d, n_batch, t_len, tm, g, win_b, wgate_b, bgate, cos_t, sin_t, half, scale):
    n, d = x2d.shape
    w = win_b.shape[1] // 4
    nh = w // LANES
    nt = t_len // tm
    row = lambda b, t: (b * nt + t, 0)
    const = lambda b, t: (0, 0)
    outs = pl.pallas_call(
        functools.partial(_proj_kernel, half=half, scale=scale),
        grid=(n_batch, nt),
        in_specs=[pl.BlockSpec((tm, d), row),
                  pl.BlockSpec((1, d), const),
                  pl.BlockSpec(win_b.shape, const),
                  pl.BlockSpec(wgate_b.shape, const),
                  pl.BlockSpec((1, wgate_b.shape[1]), const),
                  pl.BlockSpec((tm, LANES), lambda b, t: (t, 0)),
                  pl.BlockSpec((tm, LANES), lambda b, t: (t, 0))],
        out_specs=[pl.BlockSpec((tm, w), lambda b, t: (t, b)),
                   pl.BlockSpec((tm, w), row),
                   pl.BlockSpec((tm * nh, LANES), row),
                   pl.BlockSpec((tm * nh, LANES), row),
                   pl.BlockSpec((tm, w), row),
                   pl.BlockSpec((tm, w), row),
                   pl.BlockSpec((tm, 2 * d), row)],
        out_shape=[jax.ShapeDtypeStruct((t_len, n_batch * w), F32),
                   jax.ShapeDtypeStruct((n, w), BF16),
                   jax.ShapeDtypeStruct((n * nh, LANES), F32),
                   jax.ShapeDtypeStruct((n * nh, LANES), F32),
                   jax.ShapeDtypeStruct((n, w), BF16),
                   jax.ShapeDtypeStruct((n, w), BF16),
                   jax.ShapeDtypeStruct((n, 2 * d), BF16)],
        compiler_params=_cparams(("parallel", "parallel")),
        name="proj",
    )(x2d, g, win_b, wgate_b, bgate, cos_t, sin_t)
    return outs


def _s5_kernel(u_ref, x0_ref, lre_ref, lim_ref, ldt_ref, bre_ref, bim_ref, cre_ref, cim_ref,
               d_ref, wglu_ref, bglu_ref, y_ref, st_ref,
               zx_ref, wz_ref, are_ref, aim_ref, state_ref, us_ref, ys_ref, *, nb, tc, ns, sw):
    step = pl.program_id(0)
    w = ns * LANES

    @pl.when(step == 0)
    def _init():
        lre = lre_ref[...]
        lim = lim_ref[...]
        dt = jnp.exp(ldt_ref[...])
        mag = jnp.exp(lre * dt)
        are = mag * jnp.cos(lim * dt)
        aim = mag * jnp.sin(lim * dt)
        den = lre * lre + lim * lim
        fre = ((are - 1.0) * lre + aim * lim) / den
        fim = (aim * lre - (are - 1.0) * lim) / den
        are_ref[...] = are
        aim_ref[...] = aim
        for i in range(ns):
            fr = fre[:, i * sw:(i + 1) * sw]
            fi = fim[:, i * sw:(i + 1) * sw]
            br = bre_ref[i]
            bi = bim_ref[i]
            wz_ref[i, :, :sw] = (fr * br - fi * bi).astype(BF16)
            wz_ref[i, :, sw:] = (fr * bi + fi * br).astype(BF16)
        state_ref[...] = x0_ref[...]

    for b in range(nb):
        for c in range(ns):
            us_ref[c, pl.ds(b, tc, stride=nb), :] = u_ref[:, b * w + c * LANES:b * w + (c + 1) * LANES]
    ub = jnp.concatenate([us_ref[c] for c in range(ns)], axis=1)
    ubb = ub.astype(BF16)
    for i in range(ns):
        zx_ref[:, 2 * sw * i:2 * sw * (i + 1)] = _dot(ubb[:, LANES * i:LANES * (i + 1)], wz_ref[i])

    for i in range(ns):
        c0 = 2 * sw * i
        arb = jnp.broadcast_to(are_ref[:, i * sw:(i + 1) * sw], (nb, sw))
        aib = jnp.broadcast_to(aim_ref[:, i * sw:(i + 1) * sw], (nb, sw))

        def body(t, carry, c0=c0, arb=arb, aib=aib):
            xr, xi = carry
            r0 = pl.multiple_of(t * nb, nb)
            zr = zx_ref[pl.ds(r0, nb), c0:c0 + sw]
            zi = zx_ref[pl.ds(r0, nb), c0 + sw:c0 + 2 * sw]
            nxr = arb * xr - aib * xi + zr
            nxi = arb * xi + aib * xr + zi
            zx_ref[pl.ds(r0, nb), c0:c0 + sw] = nxr
            zx_ref[pl.ds(r0, nb), c0 + sw:c0 + 2 * sw] = nxi
            return nxr, nxi

        xr, xi = lax.fori_loop(0, tc, body,
                               (state_ref[:, c0:c0 + sw], state_ref[:, c0 + sw:c0 + 2 * sw]))
        state_ref[:, c0:c0 + sw] = xr
        state_ref[:, c0 + sw:c0 + 2 * sw] = xi

    ys = []
    for i in range(ns):
        c0 = 2 * sw * i
        xr = zx_ref[:, c0:c0 + sw].astype(BF16)
        xi = zx_ref[:, c0 + sw:c0 + 2 * sw].astype(BF16)
        ys.append(_dot(xr, cre_ref[i]) - _dot(xi, cim_ref[i]))
    y = jnp.concatenate(ys, axis=1) + d_ref[...] * ub
    y = jax.nn.gelu(y)
    gl = _dot(y.astype(BF16), wglu_ref[...]) + bglu_ref[...]
    yv = y * jax.nn.sigmoid(gl)
    for c in range(ns):
        ys_ref[c] = yv[:, c * LANES:(c + 1) * LANES]
    for b in range(nb):
        for c in range(ns):
            y_ref[:, b * w + c * LANES:b * w + (c + 1) * LANES] = (
                ys_ref[c, pl.ds(b, tc, stride=nb), :].astype(y_ref.dtype))

    @pl.when(step == pl.num_programs(0) - 1)
    def _fin():
        st_ref[...] = state_ref[...]


def _s5_layout(lp):
    g, p, h = lp['ssm_b_re'].shape
    gps = LANES // h
    ns = g // gps
    eye = jnp.eye(gps, dtype=jnp.bool_)

    def bd_b(b):
        bb = b.reshape(ns, gps, p, h).transpose(0, 1, 3, 2)
        out = jnp.where(eye[None, :, None, :, None], bb[:, :, :, None, :], 0.0)
        return out.reshape(ns, gps * h, gps * p)

    def bd_c(c):
        cc = c.reshape(ns, gps, h, p).transpose(0, 1, 3, 2)
        out = jnp.where(eye[None, :, None, :, None], cc[:, :, :, None, :], 0.0)
        return out.reshape(ns, gps * p, gps * h)

    return dict(
        lre=lp['ssm_lambda_re'].reshape(1, g * p),
        lim=lp['ssm_lambda_im'].reshape(1, g * p),
        ldt=jnp.repeat(lp['ssm_log_dt'], p).reshape(1, g * p),
        bre=bd_b(lp['ssm_b_re']), bim=bd_b(lp['ssm_b_im']),
        cre=bd_c(lp['ssm_c_re']).astype(BF16), cim=bd_c(lp['ssm_c_im']).astype(BF16),
        d=lp['ssm_d'].reshape(1, g * h),
        wglu=lp['ssm_w_glu'].astype(BF16), bglu=lp['ssm_b_glu'].reshape(1, -1),
        ns=ns, sw=gps * p)


def _state_to_cols(re, im, ns):
    b = re.shape[0]
    return jnp.stack([re.reshape(b, ns, -1), im.reshape(b, ns, -1)], axis=2).reshape(b, -1)


def _cols_to_state(st, ns, g, p):
    b = st.shape[0]
    s = st.reshape(b, ns, 2, -1)
    return s[:, :, 0].reshape(b, g, p), s[:, :, 1].reshape(b, g, p)


def _s5(u, x0, sp, nb, t_len, tc, y_dtype):
    rows = nb * tc
    ns, sw = sp['ns'], sp['sw']
    ncol = ns * 2 * sw
    full = lambda a: pl.BlockSpec(a.shape, lambda i: (0,) * a.ndim)
    args = (u, x0, sp['lre'], sp['lim'], sp['ldt'], sp['bre'], sp['bim'], sp['cre'], sp['cim'],
            sp['d'], sp['wglu'], sp['bglu'])
    return pl.pallas_call(
        functools.partial(_s5_kernel, nb=nb, tc=tc, ns=ns, sw=sw),
        grid=(t_len // tc,),
        in_specs=[pl.BlockSpec((tc, u.shape[1]), lambda i: (i, 0))] + [full(a) for a in args[1:]],
        out_specs=[pl.BlockSpec((tc, u.shape[1]), lambda i: (i, 0)),
                   pl.BlockSpec((nb, ncol), lambda i: (0, 0))],
        out_shape=[jax.ShapeDtypeStruct(u.shape, y_dtype),
                   jax.ShapeDtypeStruct((nb, ncol), F32)],
        scratch_shapes=[pltpu.VMEM((rows, ncol), F32),
                        pltpu.VMEM((ns, LANES, 2 * sw), BF16),
                        pltpu.VMEM((1, ns * sw), F32),
                        pltpu.VMEM((1, ns * sw), F32),
                        pltpu.VMEM((nb, ncol), F32),
                        pltpu.VMEM((ns, rows, LANES), F32),
                        pltpu.VMEM((ns, rows, LANES), F32)],
        compiler_params=_cparams(("arbitrary",)),
        name="s5",
    )(*args)


def _diff_lambda(lq1, lk1, lq2, lk2, lambda_init):
    return (jnp.exp(jnp.sum(lq1[...] * lk1[...], keepdims=True))
            - jnp.exp(jnp.sum(lq2[...] * lk2[...], keepdims=True)) + lambda_init)


def _attn_kernel(q_ref, k_ref, v_ref, lq1, lk1, lq2, lk2, g_ref, o_ref, *, tq, hd, lambda_init):
    t_len = q_ref.shape[0]
    lam = _diff_lambda(lq1, lk1, lq2, lk2, lambda_init)
    lo = lax.broadcasted_iota(jnp.int32, (tq, LANES), 1) < hd
    max_blocks = max(ATTN_KEY_CHUNK // tq, 1)

    for qi in range(t_len // tq):
        q = q_ref[qi * tq:(qi + 1) * tq, :]
        zero = jnp.zeros_like(q)
        qs = jnp.concatenate([jnp.where(lo, q, zero), jnp.where(lo, zero, q)], axis=0)
        m = jnp.full((1, 2 * tq), NEG_INF, F32)
        l = jnp.zeros((1, 2 * tq), F32)
        acc = jnp.zeros((LANES, 2 * tq), F32)
        n_blocks = qi + 1
        n_chunks = -(-n_blocks // max_blocks)
        k0 = 0
        for ci in range(n_chunks):
            ksz = (n_blocks // n_chunks + (1 if ci < n_blocks % n_chunks else 0)) * tq
            s = _dot_nt(k_ref[k0:k0 + ksz, :], qs)
            if ci == n_chunks - 1:
                kpos = k0 + lax.broadcasted_iota(jnp.int32, (ksz, 2 * tq), 0)
                qc = lax.broadcasted_iota(jnp.int32, (ksz, 2 * tq), 1)
                s = jnp.where(kpos <= qi * tq + jnp.where(qc >= tq, qc - tq, qc), s, NEG_INF)
            m_new = jnp.maximum(m, jnp.max(s, axis=0, keepdims=True))
            alpha = jnp.exp(m - m_new)
            p = jnp.exp(s - m_new)
            l = alpha * l + jnp.sum(p, axis=0, keepdims=True)
            pv = lax.dot_general(v_ref[k0:k0 + ksz, :], p.astype(BF16), (((0,), (0,)), ((), ())),
                                 preferred_element_type=F32)
            acc = alpha * acc + pv
            m = m_new
            k0 += ksz
        o_t = (acc[:, :tq] / l[:, :tq]) - lam * (acc[:, tq:] / l[:, tq:])
        ms = jnp.mean(o_t * o_t, axis=0, keepdims=True)
        o_t = o_t * lax.rsqrt(ms + NORM_EPS) * g_ref[...] * (1.0 - lambda_init)
        o_ref[qi * tq:(qi + 1) * tq, :] = o_t.T.astype(BF16)


def _attn(qb, kb, vb, lam_params, g, n_batch, t_len, tq, hd, lambda_init):
    n, w = qb.shape
    nh = w // LANES
    blk = pl.BlockSpec((t_len, LANES), lambda b, h: (b, h))
    small = lambda a: pl.BlockSpec(a.shape, lambda b, h: (0, 0))
    g_cols = jnp.broadcast_to(g.reshape(LANES, 1), (LANES, tq))
    return pl.pallas_call(
        functools.partial(_attn_kernel, tq=tq, hd=hd, lambda_init=lambda_init),
        grid=(n_batch, nh),
        in_specs=[blk, blk, blk] + [small(a) for a in lam_params] + [small(g_cols)],
        out_specs=blk,
        out_shape=jax.ShapeDtypeStruct((n, w), BF16),
        compiler_params=_cparams(("parallel", "parallel")),
        name="attn_prompt",
    )(qb, kb, vb, *lam_params, g_cols)


def _sattn_kernel(pt_ref, q_ref, kown_ref, vown_ref, ck_hbm, cv_hbm, lq1, lk1, lq2, lk2, g_ref,
                  o_ref, kbuf, vbuf, sem, kb_ref, vb_ref, m_ref, l_ref, acc_ref,
                  *, layer, nsteps, total, npg, prow, dq, nh, lambda_init):
    j = pl.program_id(1)
    t = pl.program_id(0) * nsteps + j
    nq = q_ref.shape[1]
    rq = nq // nh
    page = prow // nh
    q = [q_ref[0, h * rq:(h + 1) * rq, :].astype(BF16) for h in range(nh)]

    def page_copy(step, p_, cache, buf, kind):
        page_id = pt_ref[step // nsteps, (step % nsteps) * npg + p_]
        slot = step % PAGE_RING
        return pltpu.make_async_copy(cache.at[layer, page_id], buf.at[slot, p_],
                                     sem.at[kind, slot, p_])

    def start_step(step):
        for p_ in range(npg):
            page_copy(step, p_, ck_hbm, kbuf, 0).start(priority=p_ % 2)
            page_copy(step, p_, cv_hbm, vbuf, 1).start(priority=(p_ + 1) % 2)

    @pl.when(t == 0)
    def _prime():
        for s_ in range(min(PAGE_RING - 1, total)):
            start_step(s_)

    @pl.when(t + (PAGE_RING - 1) < total)
    def _prefetch():
        start_step(t + (PAGE_RING - 1))

    for p_ in range(npg):
        page_copy(t, p_, ck_hbm, kbuf, 0).wait()
        page_copy(t, p_, cv_hbm, vbuf, 1).wait()
    slot = t % PAGE_RING

    @pl.when(j == 0)
    def _init():
        m_ref[...] = jnp.full(m_ref.shape, NEG_INF, F32)
        l_ref[...] = jnp.zeros(l_ref.shape, F32)
        acc_ref[...] = jnp.zeros(acc_ref.shape, F32)

    def update(s, values):
        m_prev = m_ref[...]
        m_new = jnp.maximum(m_prev, jnp.max(s, axis=1, keepdims=True))
        alpha = jnp.exp(m_prev - m_new)
        p = jnp.exp(s - m_new)
        l_ref[...] = alpha * l_ref[...] + jnp.sum(p, axis=1, keepdims=True)
        pv = jnp.concatenate([_dot(p[h * rq:(h + 1) * rq].astype(BF16), values[h])
                              for h in range(nh)], axis=0)
        acc_ref[...] = alpha * acc_ref[...] + pv
        m_ref[...] = m_new

    for p_ in range(npg):
        for h in range(nh):
            rows = pl.ds(h, page, stride=nh)
            kb_ref[h, p_ * page:(p_ + 1) * page, :] = kbuf[slot, p_, rows, :].astype(BF16)
            vb_ref[h, p_ * page:(p_ + 1) * page, :] = vbuf[slot, p_, rows, :].astype(BF16)
    s = jnp.concatenate([_dot_nt(q[h], kb_ref[h]) for h in range(nh)], axis=0)
    update(s, [vb_ref[h] for h in range(nh)])

    @pl.when(j == nsteps - 1)
    def _fin():
        k_own = [kown_ref[0, pl.ds(h, dq, stride=nh), :].astype(BF16) for h in range(nh)]
        v_own = [vown_ref[0, pl.ds(h, dq, stride=nh), :].astype(BF16) for h in range(nh)]
        s_own = jnp.concatenate([_dot_nt(q[h], k_own[h]) for h in range(nh)], axis=0)
        r = lax.broadcasted_iota(jnp.int32, (nq, dq), 0)
        c = lax.broadcasted_iota(jnp.int32, (nq, dq), 1)
        update(jnp.where(c <= r % dq, s_own, NEG_INF), v_own)
        lam = _diff_lambda(lq1, lk1, lq2, lk2, lambda_init)
        o_all = acc_ref[...] / l_ref[...]
        for h in range(nh):
            o = o_all[h * rq:h * rq + dq] - lam * o_all[h * rq + dq:(h + 1) * rq]
            o_ref[0, h * dq:(h + 1) * dq, :] = (_rms(o, g_ref[...])
                                                * (1.0 - lambda_init)).astype(BF16)


def _sattn(page_table, q_rows, k_own, v_own, cache_k, cache_v, layer, lam_params, g, npg, dq, nh,
           lambda_init):
    nb, nq, _ = q_rows.shape
    prow = cache_k.shape[2]
    n_pages = page_table.shape[1]
    nsteps = n_pages // npg
    per_b = lambda a: pl.BlockSpec((1,) + a.shape[1:], lambda b, j, pt: (b, 0, 0))
    small = lambda a: pl.BlockSpec(a.shape, lambda b, j, pt: (0, 0))
    in_hbm = pl.BlockSpec(memory_space=pl.ANY)
    grid_spec = pltpu.PrefetchScalarGridSpec(
        num_scalar_prefetch=1,
        grid=(nb, nsteps),
        in_specs=([per_b(q_rows), per_b(k_own), per_b(v_own), in_hbm, in_hbm]
                  + [small(a) for a in lam_params] + [small(g)]),
        out_specs=pl.BlockSpec((1, nq // 2, LANES), lambda b, j, pt: (b, 0, 0)),
        scratch_shapes=[pltpu.VMEM((PAGE_RING, npg, prow, LANES), cache_k.dtype),
                        pltpu.VMEM((PAGE_RING, npg, prow, LANES), cache_v.dtype),
                        pltpu.SemaphoreType.DMA((2, PAGE_RING, npg)),
                        pltpu.VMEM((nh, npg * prow // nh, LANES), BF16),
                        pltpu.VMEM((nh, npg * prow // nh, LANES), BF16),
                        pltpu.VMEM((nq, 1), F32),
                        pltpu.VMEM((nq, 1), F32),
                        pltpu.VMEM((nq, LANES), F32)])
    return pl.pallas_call(
        functools.partial(_sattn_kernel, layer=layer, nsteps=nsteps, total=nb * nsteps, npg=npg,
                          prow=prow, dq=dq, nh=nh, lambda_init=lambda_init),
        grid_spec=grid_spec,
        out_shape=jax.ShapeDtypeStruct((nb, nq // 2, LANES), BF16),
        compiler_params=_cparams(("arbitrary", "arbitrary")),
        name="attn_sample",
    )(page_table, q_rows, k_own, v_own, cache_k, cache_v, *lam_params, g)


def _merge_kernel(x_ref, gates_ref, y_ref, o_ref, wbs_ref, wba_ref, wout_ref, gffn_ref,
                  wr_ref, br_ref, x1_ref, h2_ref, ri_ref, rf_ref, cnt_ref, run_ref, *, ne):
    first = jnp.logical_and(pl.program_id(0) == 0, pl.program_id(1) == 0)

    @pl.when(first)
    def _init():
        run_ref[...] = jnp.zeros(run_ref.shape, F32)

    tm, d = x_ref.shape
    a = _dot(y_ref[...].astype(BF16), wbs_ref[...])
    b = _dot(o_ref[...], wba_ref[...])
    merged = gates_ref[:, :d].astype(F32) * a + gates_ref[:, d:].astype(F32) * b
    x1 = x_ref[...] + _dot(merged.astype(BF16), wout_ref[...])
    x1_ref[...] = x1
    h2 = _rms(x1, gffn_ref[...])
    h2b = h2.astype(BF16)
    h2_ref[...] = _pack_bf16_pairs(h2)

    logits = _dot(h2b, wr_ref[...]) + br_ref[...]
    lane = lax.broadcasted_iota(jnp.int32, logits.shape, 1)
    lane_f = lane.astype(F32)
    cur = jnp.where(lane < ne, logits, -jnp.inf)
    vals, idxs, hots = [], [], []
    for _ in range(TOP_K):
        mk = jnp.max(cur, axis=1, keepdims=True)
        ik = jnp.min(jnp.where(cur == mk, lane_f, float(LANES)), axis=1, keepdims=True)
        hot = lane_f == ik
        cur = jnp.where(hot, -jnp.inf, cur)
        vals.append(mk)
        idxs.append(ik.astype(jnp.int32))
        hots.append(hot)
    exps = [jnp.exp(v - vals[0]) for v in vals]
    denom = sum(exps[1:], exps[0])
    chosen = functools.reduce(jnp.logical_or, hots)
    cnt = jnp.where(chosen, 1.0, 0.0)
    r = lax.broadcasted_iota(jnp.int32, (tm, tm), 0)
    c = lax.broadcasted_iota(jnp.int32, (tm, tm), 1)
    before = _dot((c < r).astype(BF16), cnt.astype(BF16)) + run_ref[...]
    ri = jnp.zeros(logits.shape, jnp.int32)
    rf = jnp.zeros(logits.shape, F32)
    for k in range(TOP_K):
        rank = jnp.sum(jnp.where(hots[k], before, 0.0), axis=1, keepdims=True).astype(jnp.int32)
        ri = jnp.where(lane == k, idxs[k], ri)
        ri = jnp.where(lane == TOP_K + k, rank, ri)
        rf = jnp.where(lane == k, exps[k] / denom, rf)
    ri_ref[...] = ri.T[:2 * TOP_K]
    rf_ref[...] = rf
    run_ref[...] = run_ref[...] + jnp.sum(cnt, axis=0, keepdims=True)
    cnt_ref[...] = run_ref[...]


def _merge(x2d, gates, y_tb, o, n_batch, t_len, tm, wbs, wba, wout, gffn, wr, br, ne):
    n, d = x2d.shape
    w = o.shape[1]
    nt = t_len // tm
    row = lambda b, t: (b * nt + t, 0)
    const = lambda b, t: (0, 0)
    return pl.pallas_call(
        functools.partial(_merge_kernel, ne=ne),
        grid=(n_batch, nt),
        in_specs=[pl.BlockSpec((tm, d), row),
                  pl.BlockSpec((tm, 2 * d), row),
                  pl.BlockSpec((tm, w), lambda b, t: (t, b)),
                  pl.BlockSpec((tm, w), row),
                  pl.BlockSpec(wbs.shape, const),
                  pl.BlockSpec(wba.shape, const),
                  pl.BlockSpec(wout.shape, const),
                  pl.BlockSpec((1, d), const),
                  pl.BlockSpec(wr.shape, const),
                  pl.BlockSpec((1, LANES), const)],
        out_specs=[pl.BlockSpec((tm, d), row),
                   pl.BlockSpec((tm, d // 2), row),
                   pl.BlockSpec((2 * TOP_K, tm), lambda b, t: (0, b * nt + t)),
                   pl.BlockSpec((tm, LANES), row),
                   pl.BlockSpec((1, LANES), const)],
        out_shape=[jax.ShapeDtypeStruct((n, d), F32),
                   jax.ShapeDtypeStruct((n, d // 2), jnp.uint32),
                   jax.ShapeDtypeStruct((2 * TOP_K, n), jnp.int32),
                   jax.ShapeDtypeStruct((n, LANES), F32),
                   jax.ShapeDtypeStruct((1, LANES), F32)],
        scratch_shapes=[pltpu.VMEM((1, LANES), F32)],
        compiler_params=_cparams(("arbitrary", "arbitrary")),
        name="merge",
    )(x2d, gates, y_tb, o, wbs, wba, wout, gffn, wr, br)


def _moe_kernel(be_ref, nu_ref, x_ref, wgu_ref, bgu_ref, wd_ref, bd_ref, o_ref, wgu_s, wd_s):
    i = pl.program_id(0)
    active = i < nu_ref[0]
    fresh = jnp.logical_or(i == 0, be_ref[i] != be_ref[jnp.maximum(i - 1, 0)])
    pc = PAIR_COLS
    hc = pc // 2
    n_chunk = wgu_ref.shape[2] // pc

    @pl.when(jnp.logical_and(active, fresh))
    def _stage_weights():
        r = lax.broadcasted_iota(jnp.int32, (pc, pc), 0)
        c = lax.broadcasted_iota(jnp.int32, (pc, pc), 1)
        perm = (r == jnp.where(c < hc, 2 * c, 2 * (c - hc) + 1)).astype(BF16)
        for j in range(n_chunk):
            wj = wgu_ref[0, :, j * pc:(j + 1) * pc].astype(BF16)
            wgu_s[:, j * pc:(j + 1) * pc] = _dot(wj, perm).astype(BF16)
        wd_s[...] = wd_ref[0].astype(BF16)

    @pl.when(active)
    def _compute():
        x = _unpack_bf16_pairs(x_ref[...]).astype(BF16)
        gu = _dot(x, wgu_s[...]) + bgu_ref[0]
        acts = []
        for j in range(n_chunk):
            g_lin = jnp.minimum(gu[:, j * pc:j * pc + hc], SWIGLU_LIMIT)
            up = jnp.clip(gu[:, j * pc + hc:(j + 1) * pc], -SWIGLU_LIMIT, SWIGLU_LIMIT)
            acts.append(((up + 1.0) * (g_lin * jax.nn.sigmoid(SWIGLU_ALPHA * g_lin))).astype(BF16))
        o_ref[...] = _pack_bf16_pairs(_dot(jnp.concatenate(acts, axis=1), wd_s[...]) + bd_ref[0])

    @pl.when(jnp.logical_not(active))
    def _skip():
        o_ref[...] = jnp.zeros(o_ref.shape, o_ref.dtype)


def _moe_block_rows(n_tokens, ne):
    return MOE_BLOCK_ROWS if n_tokens * TOP_K >= 8 * ne * MOE_BLOCK_ROWS else MOE_BLOCK_ROWS_SMALL


def _moe_gemm(xs, blk_expert, n_used, wgu, bgu, wd, bd, bm):
    n_rows = xs.shape[0]
    d = wd.shape[2]
    wspec = lambda a: pl.BlockSpec((1,) + a.shape[1:], lambda i, be, nu: (be[i], 0, 0))
    grid_spec = pltpu.PrefetchScalarGridSpec(
        num_scalar_prefetch=2,
        grid=(n_rows // bm,),
        in_specs=[pl.BlockSpec((bm, xs.shape[1]), lambda i, be, nu: (i, 0)),
                  wspec(wgu), wspec(bgu), wspec(wd), wspec(bd)],
        out_specs=pl.BlockSpec((bm, d // 2), lambda i, be, nu: (i, 0)),
        scratch_shapes=[pltpu.VMEM(wgu.shape[1:], BF16),
                        pltpu.VMEM(wd.shape[1:], BF16)])
    return pl.pallas_call(
        _moe_kernel,
        grid_spec=grid_spec,
        out_shape=jax.ShapeDtypeStruct((n_rows, d // 2), jnp.uint32),
        compiler_params=_cparams(("arbitrary",)),
        name="moe_gemm",
    )(blk_expert, n_used, xs, wgu, bgu, wd, bd)


def _combine_kernel(x_ref, *rest, final_norm):
    yg_refs = rest[:TOP_K]
    gate_ref, g_ref, o_ref = rest[TOP_K:]
    acc = x_ref[...]
    gate = gate_ref[...]
    for k in range(TOP_K):
        acc = acc + gate[:, k:k + 1] * _unpack_bf16_pairs(yg_refs[k][...])
    o_ref[...] = _rms(acc, g_ref[...]) if final_norm else acc


def _combine(x1, ygs, gate, g, tm, final_norm):
    n, d = x1.shape
    row = pl.BlockSpec((tm, d), lambda i: (i, 0))
    packed = pl.BlockSpec((tm, d // 2), lambda i: (i, 0))
    return pl.pallas_call(
        functools.partial(_combine_kernel, final_norm=final_norm),
        grid=(n // tm,),
        in_specs=[row] + [packed] * TOP_K + [pl.BlockSpec((tm, LANES), lambda i: (i, 0)),
                                          pl.BlockSpec((1, d), lambda i: (0, 0))],
        out_specs=row,
        out_shape=jax.ShapeDtypeStruct((n, d), F32),
        compiler_params=_cparams(("parallel",)),
        name="combine",
    )(x1, *ygs, gate, g)


def _moe_dispatch(h2p, route_i, count_row, ne):
    n = h2p.shape[0]
    bm = _moe_block_rows(n, ne)
    m = n * TOP_K
    counts = count_row[0, :ne].astype(jnp.int32)
    start = jnp.cumsum(counts) - counts
    padded = (counts + bm - 1) // bm * bm
    pend = jnp.cumsum(padded)
    pstart = pend - padded
    dest = [pstart[route_i[k]] + route_i[TOP_K + k] for k in range(TOP_K)]
    n_rows = (m + bm - 1) // bm * bm + ne * bm
    blk_start = jnp.arange(n_rows // bm, dtype=jnp.int32) * bm
    blk_expert = jnp.minimum(
        jnp.sum((blk_start[:, None] >= pend[None, :]).astype(jnp.int32), axis=1), ne - 1)
    n_used = (pend[-1] // bm).astype(jnp.int32).reshape(1)
    tok = jnp.tile(jnp.arange(n, dtype=jnp.int32), TOP_K)
    _, tok_sorted = lax.sort_key_val(jnp.concatenate(dest), tok)
    blk_shift = (pstart - start)[blk_expert]
    src = jnp.arange(n_rows, dtype=jnp.int32) - jnp.repeat(blk_shift, bm)
    row_tok = tok_sorted[jnp.clip(src, 0, m - 1)]
    return dict(xs=h2p[row_tok], blk_expert=blk_expert, n_used=n_used, dest=dest, bm=bm)


def _moe_apply(x1, disp, route_f, mp, g_final, final_norm):
    ys = _moe_gemm(disp['xs'], disp['blk_expert'], disp['n_used'],
                   mp['wgu'], mp['bgu'], mp['wd'], mp['bd'], disp['bm'])
    ygs = [ys[disp['dest'][k]] for k in range(TOP_K)]
    tm = min(ROW_TILE, x1.shape[0])
    return _combine(x1, ygs, route_f, g_final, tm, final_norm)


def _rope_tables(pos, hd):
    half = hd // 2
    inv_freq = ROPE_THETA ** (-jnp.arange(half, dtype=F32) / half)
    ang = pos.astype(F32)[:, None] * inv_freq[None, :]
    cos = jnp.cos(ang)
    sin = jnp.sin(ang)
    reps = LANES // hd
    cos_t = jnp.tile(jnp.concatenate([cos, cos], axis=1), (1, reps))
    sin_t = jnp.tile(jnp.concatenate([-sin, sin], axis=1), (1, reps))
    return cos_t, sin_t


def _layer_params(lp):
    d = lp['w_in'].shape[0]
    ne = lp['w_router'].shape[1]
    wr = jnp.zeros((d, LANES), F32).at[:, :ne].set(lp['w_router']).astype(BF16)
    br = jnp.zeros((1, LANES), F32).at[0, :ne].set(lp['b_router'])
    return dict(
        g_mix=lp['norm_mix_g'].reshape(1, d),
        win=lp['w_in'].astype(BF16),
        wgate=lp['w_gate'].astype(BF16),
        bgate=lp['b_gate'].reshape(1, -1),
        s5=_s5_layout(lp),
        lam=[lp[k].reshape(1, -1) for k in ('lambda_q1', 'lambda_k1', 'lambda_q2', 'lambda_k2')],
        g_sub=lp['attn_subln_g'].reshape(1, -1),
        wbs=lp['w_branch_ssm'].astype(BF16),
        wba=lp['w_branch_attn'].astype(BF16),
        wout=lp['w_out'].astype(BF16),
        g_ffn=lp['norm_ffn_g'].reshape(1, d),
        wr=wr, br=br,
        moe=dict(ne=ne,
                 wgu=lp['w_gate_up'],
                 bgu=lp['b_gate_up'].reshape(ne, -1, PAIR_COLS // 2, 2).transpose(0, 1, 3, 2)
                 .reshape(ne, 1, -1),
                 wd=lp['w_down'],
                 bd=lp['b_down'][:, None, :]))


def _query_rows(qb, bsz, dq, nh, hd):
    qt = qb.astype(F32).reshape(bsz, dq, nh, 2, hd).transpose(0, 2, 3, 1, 4)
    ec = jnp.eye(2, dtype=jnp.bool_)[None, None, :, None, :, None]
    out = jnp.where(ec, qt[:, :, :, :, None, :], 0.0)
    return out.reshape(bsz, nh * 2 * dq, 2 * hd)


def _layer(x, pos, x0_re, x0_im, past, P, lambda_init):
    bsz, t_len, d = x.shape
    n = bsz * t_len
    x2d = x.reshape(n, d)
    sp = P['s5']
    g, p = x0_re.shape[1], x0_re.shape[2]
    w = P['win'].shape[1] // 4
    nh = w // LANES
    hd = LANES // 2
    cos_t, sin_t = _rope_tables(pos, hd)
    x0 = _state_to_cols(x0_re.astype(F32), x0_im.astype(F32), sp['ns'])
    if past is None:
        n_batch, rows_t, tm = bsz, t_len, min(ROW_TILE, t_len)
    else:
        n_batch, rows_t, tm = 1, n, n
        cos_t = jnp.tile(cos_t, (bsz, 1))
        sin_t = jnp.tile(sin_t, (bsz, 1))
    u, qb, k, v, kb, vb, gates = _proj(x2d, n_batch, rows_t, tm, P['g_mix'], P['win'], P['wgate'],
                                       P['bgate'], cos_t, sin_t, hd // 2, hd ** -0.5)
    if past is None:
        y_in, st = _s5(u, x0, sp, bsz, t_len, min(64, t_len), BF16)
        o = _attn(qb, kb, vb, P['lam'], P['g_sub'], bsz, t_len, min(256, t_len), hd, lambda_init)
    else:
        cache_k, cache_v, page_table, layer = past
        u_tb = u.reshape(bsz, t_len, w).transpose(1, 0, 2).reshape(t_len, bsz * w)
        y_tb, st = _s5(u_tb, x0, sp, bsz, t_len, t_len, F32)
        y_in = y_tb.reshape(t_len, bsz, w).transpose(1, 0, 2).reshape(n, w)
        npg = math.gcd(8, page_table.shape[1])
        cshape = cache_k.shape[:2] + (cache_k.shape[2] * nh, LANES)
        o = _sattn(page_table, _query_rows(qb, bsz, t_len, nh, hd),
                   k.reshape(bsz, t_len * nh, LANES), v.reshape(bsz, t_len * nh, LANES),
                   cache_k.reshape(cshape), cache_v.reshape(cshape), layer,
                   P['lam'], P['g_sub'], npg, t_len, nh, lambda_init)
        o = o.reshape(bsz, nh, t_len, LANES).transpose(0, 2, 1, 3).reshape(n, w)
    x1, h2p, route_i, route_f, count_row = _merge(
        x2d, gates, y_in, o, n_batch, rows_t, tm, P['wbs'], P['wba'], P['wout'], P['g_ffn'],
        P['wr'], P['br'], P['moe']['ne'])
    disp = _moe_dispatch(h2p, route_i, count_row, P['moe']['ne'])
    st_re, st_im = _cols_to_state(st, sp['ns'], g, p)
    side = (k.reshape(bsz, t_len, nh, LANES), v.reshape(bsz, t_len, nh, LANES),
            st_re.astype(x.dtype), st_im.astype(x.dtype))
    return (x1, disp, route_f), side


def kernel(x_prompt, x_sample, cache_k, cache_v, state_ssm_re, state_ssm_im, page_table, norm_mix_g, w_in, ssm_lambda_re, ssm_lambda_im, ssm_log_dt, ssm_b_re, ssm_b_im, ssm_c_re, ssm_c_im, ssm_d, ssm_w_glu, ssm_b_glu, lambda_q1, lambda_k1, lambda_q2, lambda_k2, attn_subln_g, w_branch_ssm, w_branch_attn, w_gate, b_gate, w_out, norm_ffn_g, w_router, b_router, w_gate_up, b_gate_up, w_down, b_down, norm_final_g):
    depth = w_in.shape[0]
    past_len = page_table.shape[1] * cache_k.shape[2]
    pos_prompt = jnp.arange(x_prompt.shape[1], dtype=jnp.int32)
    pos_sample = past_len + jnp.arange(x_sample.shape[1], dtype=jnp.int32)
    g, p = state_ssm_re.shape[2], state_ssm_re.shape[3]
    zero_state = jnp.zeros((x_prompt.shape[0], g, p), F32)
    g_final = norm_final_g.reshape(1, -1)
    names = ('norm_mix_g', 'w_in', 'ssm_lambda_re', 'ssm_lambda_im', 'ssm_log_dt', 'ssm_b_re',
             'ssm_b_im', 'ssm_c_re', 'ssm_c_im', 'ssm_d', 'ssm_w_glu', 'ssm_b_glu', 'lambda_q1',
             'lambda_k1', 'lambda_q2', 'lambda_k2', 'attn_subln_g', 'w_branch_ssm',
             'w_branch_attn', 'w_gate', 'b_gate', 'w_out', 'norm_ffn_g', 'w_router', 'b_router',
             'w_gate_up', 'b_gate_up', 'w_down', 'b_down')
    stacked = (norm_mix_g, w_in, ssm_lambda_re, ssm_lambda_im, ssm_log_dt, ssm_b_re, ssm_b_im,
               ssm_c_re, ssm_c_im, ssm_d, ssm_w_glu, ssm_b_glu, lambda_q1, lambda_k1, lambda_q2,
               lambda_k2, attn_subln_g, w_branch_ssm, w_branch_attn, w_gate, b_gate, w_out,
               norm_ffn_g, w_router, b_router, w_gate_up, b_gate_up, w_down, b_down)
    xp, xs = x_prompt, x_sample
    outs_p, outs_s = [], []
    for l in range(depth):
        P = _layer_params({nm: a[l] for nm, a in zip(names, stacked)})
        lambda_init = 0.8 - 0.6 * math.exp(-0.3 * l)
        last = l == depth - 1
        moe_p, side_p = _layer(xp, pos_prompt, zero_state, zero_state, None, P, lambda_init)
        moe_s, side_s = _layer(xs, pos_sample, state_ssm_re[l], state_ssm_im[l],
                               (cache_k, cache_v, page_table, l), P, lambda_init)
        (x1_p, disp_p, rf_p), (x1_s, disp_s, rf_s) = moe_p, moe_s
        n_used_p, x1_s = lax.optimization_barrier((disp_p['n_used'], x1_s))
        disp_p = dict(disp_p, n_used=n_used_p)
        xp = _moe_apply(x1_p, disp_p, rf_p, P['moe'], g_final, last).reshape(xp.shape)
        xs = _moe_apply(x1_s, disp_s, rf_s, P['moe'], g_final, last).reshape(xs.shape)
        outs_p.append(side_p)
        outs_s.append(side_s)
    stack = lambda outs, i: jnp.stack([o[i] for o in outs])
    return (xp, xs, stack(outs_p, 0), stack(outs_p, 1), stack(outs_p, 2), stack(outs_p, 3),
            stack(outs_s, 0), stack(outs_s, 1), stack(outs_s, 2), stack(outs_s, 3))
```

```python
import functools
import math

import jax
import jax.numpy as jnp
from jax import lax
from jax.experimental import pallas as pl
from jax.experimental.pallas import tpu as pltpu

F32 = jnp.float32
BF16 = jnp.bfloat16

NORM_EPS = 1e-6
ROPE_THETA = 10000.0
NEG_INF = -1e30
TOP_K = 4
SWIGLU_LIMIT = 7.0
SWIGLU_ALPHA = 1.702
LANES = 128
MOE_BLOCK_ROWS = 512
MOE_BLOCK_ROWS_SMALL = 128
ROW_TILE = 512
PAIR_COLS = 256
ATTN_KEY_CHUNK = 1024
PAGE_RING = 3
PAGES_PER_STEP = 16
VMEM_LIMIT = 56 * 1024 * 1024


def _cparams(sem):
    return pltpu.CompilerParams(dimension_semantics=sem, vmem_limit_bytes=VMEM_LIMIT)


def _dot(a, b):
    return jnp.dot(a, b, preferred_element_type=F32)


def _dot_nt(a, b):
    return lax.dot_general(a, b, (((1,), (1,)), ((), ())), preferred_element_type=F32)


def _pack_bf16_pairs(x):
    half = x.shape[1] // 2
    bits = lax.bitcast_convert_type(x.astype(BF16).astype(F32), jnp.uint32)
    return bits[:, half:] | (bits[:, :half] >> 16)


def _unpack_bf16_pairs(words):
    return jnp.concatenate(
        [lax.bitcast_convert_type(words << 16, F32),
         lax.bitcast_convert_type(words & jnp.uint32(0xFFFF0000), F32)], axis=1)


def _rms(x, g):
    return x * lax.rsqrt(jnp.mean(x * x, axis=-1, keepdims=True) + NORM_EPS) * g


def _proj_kernel(x_ref, g_ref, win_ref, wgate_ref, bgate_ref, cos_ref, sin_ref,
                 u_ref, q_ref, k_ref, v_ref, kb_ref, vb_ref, gates_ref, *, half, scale):
    hb = _rms(x_ref[...], g_ref[...]).astype(BF16)
    proj = _dot(hb, win_ref[...])
    w = u_ref.shape[-1]
    u_ref[...] = proj[:, :w]
    cos = cos_ref[...]
    sin = sin_ref[...]
    lane = lax.broadcasted_iota(jnp.int32, cos.shape, 1)
    lo = (lane % (2 * half)) < half

    def rot(xh):
        fwd = pltpu.roll(xh, LANES - half, axis=1)
        bwd = pltpu.roll(xh, half, axis=1)
        return xh * cos + jnp.where(lo, fwd, bwd) * sin

    nh = w // LANES
    tm = x_ref.shape[0]
    for hh in range(nh):
        sl = slice(hh * LANES, (hh + 1) * LANES)
        qh = rot(proj[:, w + hh * LANES:w + (hh + 1) * LANES])
        q_ref[:, sl] = (qh * scale).astype(BF16)
        kh = rot(proj[:, 2 * w + hh * LANES:2 * w + (hh + 1) * LANES])
        k_ref[pl.ds(hh, tm, stride=nh), :] = kh
        kb_ref[:, sl] = kh.astype(BF16)
        v_ref[pl.ds(hh, tm, stride=nh), :] = proj[:, 3 * w + hh * LANES:3 * w + (hh + 1) * LANES]
    vb_ref[...] = proj[:, 3 * w:].astype(BF16)
    gl = _dot(hb, wgate_ref[...]) + bgate_ref[...]
    gates_ref[...] = jax.nn.sigmoid(gl).astype(BF16)


def _proj(x2d, n_batch, t_len, tm, g, win_b, wgate_b, bgate, cos_t, sin_t, half, scale):
    n, d = x2d.shape
    w = win_b.shape[1] // 4
    nh = w // LANES
    nt = t_len // tm
    row = lambda b, t: (b * nt + t, 0)
    const = lambda b, t: (0, 0)
    outs = pl.pallas_call(
        functools.partial(_proj_kernel, half=half, scale=scale),
        grid=(n_batch, nt),
        in_specs=[pl.BlockSpec((tm, d), row),
                  pl.BlockSpec((1, d), const),
                  pl.BlockSpec(win_b.shape, const),
                  pl.BlockSpec(wgate_b.shape, const),
                  pl.BlockSpec((1, wgate_b.shape[1]), const),
                  pl.BlockSpec((tm, LANES), lambda b, t: (t, 0)),
                  pl.BlockSpec((tm, LANES), lambda b, t: (t, 0))],
        out_specs=[pl.BlockSpec((tm, w), lambda b, t: (t, b)),
                   pl.BlockSpec((tm, w), row),
                   pl.BlockSpec((tm * nh, LANES), row),
                   pl.BlockSpec((tm * nh, LANES), row),
                   pl.BlockSpec((tm, w), row),
                   pl.BlockSpec((tm, w), row),
                   pl.BlockSpec((tm, 2 * d), row)],
        out_shape=[jax.ShapeDtypeStruct((t_len, n_batch * w), F32),
                   jax.ShapeDtypeStruct((n, w), BF16),
                   jax.ShapeDtypeStruct((n * nh, LANES), F32),
                   jax.ShapeDtypeStruct((n * nh, LANES), F32),
                   jax.ShapeDtypeStruct((n, w), BF16),
                   jax.ShapeDtypeStruct((n, w), BF16),
                   jax.ShapeDtypeStruct((n, 2 * d), BF16)],
        compiler_params=_cparams(("parallel", "parallel")),
        name="proj",
    )(x2d, g, win_b, wgate_b, bgate, cos_t, sin_t)
    return outs


def _s5_kernel(u_ref, x0_ref, lre_ref, lim_ref, ldt_ref, bre_ref, bim_ref, cre_ref, cim_ref,
               d_ref, wglu_ref, bglu_ref, y_ref, st_ref,
               zx_ref, wz_ref, are_ref, aim_ref, state_ref, us_ref, ys_ref, *, nb, tc, ns, sw):
    step = pl.program_id(0)
    w = ns * LANES

    @pl.when(step == 0)
    def _init():
        lre = lre_ref[...]
        lim = lim_ref[...]
        dt = jnp.exp(ldt_ref[...])
        mag = jnp.exp(lre * dt)
        are = mag * jnp.cos(lim * dt)
        aim = mag * jnp.sin(lim * dt)
        den = lre * lre + lim * lim
        fre = ((are - 1.0) * lre + aim * lim) / den
        fim = (aim * lre - (are - 1.0) * lim) / den
        are_ref[...] = are
        aim_ref[...] = aim
        for i in range(ns):
            fr = fre[:, i * sw:(i + 1) * sw]
            fi = fim[:, i * sw:(i + 1) * sw]
            br = bre_ref[i]
            bi = bim_ref[i]
            wz_ref[i, :, :sw] = (fr * br - fi * bi).astype(BF16)
            wz_ref[i, :, sw:] = (fr * bi + fi * br).astype(BF16)
        state_ref[...] = x0_ref[...]

    for b in range(nb):
        for c in range(ns):
            us_ref[c, pl.ds(b, tc, stride=nb), :] = u_ref[:, b * w + c * LANES:b * w + (c + 1) * LANES]
    ub = jnp.concatenate([us_ref[c] for c in range(ns)], axis=1)
    ubb = ub.astype(BF16)
    for i in range(ns):
        zx_ref[:, 2 * sw * i:2 * sw * (i + 1)] = _dot(ubb[:, LANES * i:LANES * (i + 1)], wz_ref[i])

    for i in range(ns):
        c0 = 2 * sw * i
        arb = jnp.broadcast_to(are_ref[:, i * sw:(i + 1) * sw], (nb, sw))
        aib = jnp.broadcast_to(aim_ref[:, i * sw:(i + 1) * sw], (nb, sw))

        def body(t, carry, c0=c0, arb=arb, aib=aib):
            xr, xi = carry
            r0 = pl.multiple_of(t * nb, nb)
            zr = zx_ref[pl.ds(r0, nb), c0:c0 + sw]
            zi = zx_ref[pl.ds(r0, nb), c0 + sw:c0 + 2 * sw]
            nxr = arb * xr - aib * xi + zr
            nxi = arb * xi + aib * xr + zi
            zx_ref[pl.ds(r0, nb), c0:c0 + sw] = nxr
            zx_ref[pl.ds(r0, nb), c0 + sw:c0 + 2 * sw] = nxi
            return nxr, nxi

        xr, xi = lax.fori_loop(0, tc, body,
                               (state_ref[:, c0:c0 + sw], state_ref[:, c0 + sw:c0 + 2 * sw]))
        state_ref[:, c0:c0 + sw] = xr
        state_ref[:, c0 + sw:c0 + 2 * sw] = xi

    ys = []
    for i in range(ns):
        c0 = 2 * sw * i
        xr = zx_ref[:, c0:c0 + sw].astype(BF16)
        xi = zx_ref[:, c0 + sw:c0 + 2 * sw].astype(BF16)
        ys.append(_dot(xr, cre_ref[i]) - _dot(xi, cim_ref[i]))
    y = jnp.concatenate(ys, axis=1) + d_ref[...] * ub
    y = jax.nn.gelu(y)
    gl = _dot(y.astype(BF16), wglu_ref[...]) + bglu_ref[...]
    yv = y * jax.nn.sigmoid(gl)
    for c in range(ns):
        ys_ref[c] = yv[:, c * LANES:(c + 1) * LANES]
    for b in range(nb):
        for c in range(ns):
            y_ref[:, b * w + c * LANES:b * w + (c + 1) * LANES] = (
                ys_ref[c, pl.ds(b, tc, stride=nb), :].astype(y_ref.dtype))

    @pl.when(step == pl.num_programs(0) - 1)
    def _fin():
        st_ref[...] = state_ref[...]


def _s5_layout(lp):
    g, p, h = lp['ssm_b_re'].shape
    gps = LANES // h
    ns = g // gps
    eye = jnp.eye(gps, dtype=jnp.bool_)

    def bd_b(b):
        bb = b.reshape(ns, gps, p, h).transpose(0, 1, 3, 2)
        out = jnp.where(eye[None, :, None, :, None], bb[:, :, :, None, :], 0.0)
        return out.reshape(ns, gps * h, gps * p)

    def bd_c(c):
        cc = c.reshape(ns, gps, h, p).transpose(0, 1, 3, 2)
        out = jnp.where(eye[None, :, None, :, None], cc[:, :, :, None, :], 0.0)
        return out.reshape(ns, gps * p, gps * h)

    return dict(
        lre=lp['ssm_lambda_re'].reshape(1, g * p),
        lim=lp['ssm_lambda_im'].reshape(1, g * p),
        ldt=jnp.repeat(lp['ssm_log_dt'], p).reshape(1, g * p),
        bre=bd_b(lp['ssm_b_re']), bim=bd_b(lp['ssm_b_im']),
        cre=bd_c(lp['ssm_c_re']).astype(BF16), cim=bd_c(lp['ssm_c_im']).astype(BF16),
        d=lp['ssm_d'].reshape(1, g * h),
        wglu=lp['ssm_w_glu'].astype(BF16), bglu=lp['ssm_b_glu'].reshape(1, -1),
        ns=ns, sw=gps * p)


def _state_to_cols(re, im, ns):
    b = re.shape[0]
    return jnp.stack([re.reshape(b, ns, -1), im.reshape(b, ns, -1)], axis=2).reshape(b, -1)


def _cols_to_state(st, ns, g, p):
    b = st.shape[0]
    s = st.reshape(b, ns, 2, -1)
    return s[:, :, 0].reshape(b, g, p), s[:, :, 1].reshape(b, g, p)


def _s5(u, x0, sp, nb, t_len, tc, y_dtype):
    rows = nb * tc
    ns, sw = sp['ns'], sp['sw']
    ncol = ns * 2 * sw
    full = lambda a: pl.BlockSpec(a.shape, lambda i: (0,) * a.ndim)
    args = (u, x0, sp['lre'], sp['lim'], sp['ldt'], sp['bre'], sp['bim'], sp['cre'], sp['cim'],
            sp['d'], sp['wglu'], sp['bglu'])
    return pl.pallas_call(
        functools.partial(_s5_kernel, nb=nb, tc=tc, ns=ns, sw=sw),
        grid=(t_len // tc,),
        in_specs=[pl.BlockSpec((tc, u.shape[1]), lambda i: (i, 0))] + [full(a) for a in args[1:]],
        out_specs=[pl.BlockSpec((tc, u.shape[1]), lambda i: (i, 0)),
                   pl.BlockSpec((nb, ncol), lambda i: (0, 0))],
        out_shape=[jax.ShapeDtypeStruct(u.shape, y_dtype),
                   jax.ShapeDtypeStruct((nb, ncol), F32)],
        scratch_shapes=[pltpu.VMEM((rows, ncol), F32),
                        pltpu.VMEM((ns, LANES, 2 * sw), BF16),
                        pltpu.VMEM((1, ns * sw), F32),
                        pltpu.VMEM((1, ns * sw), F32),
                        pltpu.VMEM((nb, ncol), F32),
                        pltpu.VMEM((ns, rows, LANES), F32),
                        pltpu.VMEM((ns, rows, LANES), F32)],
        compiler_params=_cparams(("arbitrary",)),
        name="s5",
    )(*args)


def _diff_lambda(lq1, lk1, lq2, lk2, lambda_init):
    return (jnp.exp(jnp.sum(lq1[...] * lk1[...], keepdims=True))
            - jnp.exp(jnp.sum(lq2[...] * lk2[...], keepdims=True)) + lambda_init)


def _attn_kernel(q_ref, k_ref, v_ref, lq1, lk1, lq2, lk2, g_ref, o_ref, *, tq, hd, lambda_init):
    t_len = q_ref.shape[0]
    lam = _diff_lambda(lq1, lk1, lq2, lk2, lambda_init)
    lo = lax.broadcasted_iota(jnp.int32, (tq, LANES), 1) < hd
    max_blocks = max(ATTN_KEY_CHUNK // tq, 1)

    for qi in range(t_len // tq):
        q = q_ref[qi * tq:(qi + 1) * tq, :]
        zero = jnp.zeros_like(q)
        qs = jnp.concatenate([jnp.where(lo, q, zero), jnp.where(lo, zero, q)], axis=0)
        m = jnp.full((1, 2 * tq), NEG_INF, F32)
        l = jnp.zeros((1, 2 * tq), F32)
        acc = jnp.zeros((LANES, 2 * tq), F32)
        n_blocks = qi + 1
        n_chunks = -(-n_blocks // max_blocks)
        k0 = 0
        for ci in range(n_chunks):
            ksz = (n_blocks // n_chunks + (1 if ci < n_blocks % n_chunks else 0)) * tq
            s = _dot_nt(k_ref[k0:k0 + ksz, :], qs)
            if ci == n_chunks - 1:
                kpos = k0 + lax.broadcasted_iota(jnp.int32, (ksz, 2 * tq), 0)
                qc = lax.broadcasted_iota(jnp.int32, (ksz, 2 * tq), 1)
                s = jnp.where(kpos <= qi * tq + jnp.where(qc >= tq, qc - tq, qc), s, NEG_INF)
            m_new = jnp.maximum(m, jnp.max(s, axis=0, keepdims=True))
            alpha = jnp.exp(m - m_new)
            p = jnp.exp(s - m_new)
            l = alpha * l + jnp.sum(p, axis=0, keepdims=True)
            pv = lax.dot_general(v_ref[k0:k0 + ksz, :], p.astype(BF16), (((0,), (0,)), ((), ())),
                                 preferred_element_type=F32)
            acc = alpha * acc + pv
            m = m_new
            k0 += ksz
        o_t = (acc[:, :tq] / l[:, :tq]) - lam * (acc[:, tq:] / l[:, tq:])
        ms = jnp.mean(o_t * o_t, axis=0, keepdims=True)
        o_t = o_t * lax.rsqrt(ms + NORM_EPS) * g_ref[...] * (1.0 - lambda_init)
        o_ref[qi * tq:(qi + 1) * tq, :] = o_t.T.astype(BF16)


def _attn(qb, kb, vb, lam_params, g, n_batch, t_len, tq, hd, lambda_init):
    n, w = qb.shape
    nh = w // LANES
    blk = pl.BlockSpec((t_len, LANES), lambda b, h: (b, h))
    small = lambda a: pl.BlockSpec(a.shape, lambda b, h: (0, 0))
    g_cols = jnp.broadcast_to(g.reshape(LANES, 1), (LANES, tq))
    return pl.pallas_call(
        functools.partial(_attn_kernel, tq=tq, hd=hd, lambda_init=lambda_init),
        grid=(n_batch, nh),
        in_specs=[blk, blk, blk] + [small(a) for a in lam_params] + [small(g_cols)],
        out_specs=blk,
        out_shape=jax.ShapeDtypeStruct((n, w), BF16),
        compiler_params=_cparams(("parallel", "parallel")),
        name="attn_prompt",
    )(qb, kb, vb, *lam_params, g_cols)


def _sattn_kernel(pt_ref, q_ref, kown_ref, vown_ref, ck_hbm, cv_hbm, lq1, lk1, lq2, lk2, g_ref,
                  o_ref, kbuf, vbuf, sem, kb_ref, vb_ref, m_ref, l_ref, acc_ref,
                  *, layer, nsteps, total, npg, prow, dq, nh, lambda_init):
    j = pl.program_id(1)
    t = pl.program_id(0) * nsteps + j
    nq = q_ref.shape[1]
    rq = nq // nh
    page = prow // nh
    q = [q_ref[0, h * rq:(h + 1) * rq, :].astype(BF16) for h in range(nh)]

    def page_copy(step, p_, cache, buf, kind):
        page_id = pt_ref[step // nsteps, (step % nsteps) * npg + p_]
        slot = step % PAGE_RING
        return pltpu.make_async_copy(cache.at[layer, page_id], buf.at[slot, p_],
                                     sem.at[kind, slot, p_])

    def start_step(step):
        for p_ in range(npg):
            page_copy(step, p_, ck_hbm, kbuf, 0).start(priority=p_ % 2)
            page_copy(step, p_, cv_hbm, vbuf, 1).start(priority=(p_ + 1) % 2)

    @pl.when(t == 0)
    def _prime():
        for s_ in range(min(PAGE_RING - 1, total)):
            start_step(s_)

    @pl.when(t + (PAGE_RING - 1) < total)
    def _prefetch():
        start_step(t + (PAGE_RING - 1))

    for p_ in range(npg):
        page_copy(t, p_, ck_hbm, kbuf, 0).wait()
        page_copy(t, p_, cv_hbm, vbuf, 1).wait()
    slot = t % PAGE_RING

    @pl.when(j == 0)
    def _init():
        m_ref[...] = jnp.full(m_ref.shape, NEG_INF, F32)
        l_ref[...] = jnp.zeros(l_ref.shape, F32)
        acc_ref[...] = jnp.zeros(acc_ref.shape, F32)

    def update(s, values):
        m_prev = m_ref[...]
        m_new = jnp.maximum(m_prev, jnp.max(s, axis=1, keepdims=True))
        alpha = jnp.exp(m_prev - m_new)
        p = jnp.exp(s - m_new)
        l_ref[...] = alpha * l_ref[...] + jnp.sum(p, axis=1, keepdims=True)
        pv = jnp.concatenate([_dot(p[h * rq:(h + 1) * rq].astype(BF16), values[h])
                              for h in range(nh)], axis=0)
        acc_ref[...] = alpha * acc_ref[...] + pv
        m_ref[...] = m_new

    for p_ in range(npg):
        for h in range(nh):
            rows = pl.ds(h, page, stride=nh)
            kb_ref[h, p_ * page:(p_ + 1) * page, :] = kbuf[slot, p_, rows, :].astype(BF16)
            vb_ref[h, p_ * page:(p_ + 1) * page, :] = vbuf[slot, p_, rows, :].astype(BF16)
    s = jnp.concatenate([_dot_nt(q[h], kb_ref[h]) for h in range(nh)], axis=0)
    update(s, [vb_ref[h] for h in range(nh)])

    @pl.when(j == nsteps - 1)
    def _fin():
        k_own = [kown_ref[0, pl.ds(h, dq, stride=nh), :].astype(BF16) for h in range(nh)]
        v_own = [vown_ref[0, pl.ds(h, dq, stride=nh), :].astype(BF16) for h in range(nh)]
        s_own = jnp.concatenate([_dot_nt(q[h], k_own[h]) for h in range(nh)], axis=0)
        r = lax.broadcasted_iota(jnp.int32, (nq, dq), 0)
        c = lax.broadcasted_iota(jnp.int32, (nq, dq), 1)
        update(jnp.where(c <= r % dq, s_own, NEG_INF), v_own)
        lam = _diff_lambda(lq1, lk1, lq2, lk2, lambda_init)
        o_all = acc_ref[...] / l_ref[...]
        for h in range(nh):
            o = o_all[h * rq:h * rq + dq] - lam * o_all[h * rq + dq:(h + 1) * rq]
            o_ref[0, h * dq:(h + 1) * dq, :] = (_rms(o, g_ref[...])
                                                * (1.0 - lambda_init)).astype(BF16)


def _sattn(page_table, q_rows, k_own, v_own, cache_k, cache_v, layer, lam_params, g, npg, dq, nh,
           lambda_init):
    nb, nq, _ = q_rows.shape
    prow = cache_k.shape[2]
    n_pages = page_table.shape[1]
    nsteps = n_pages // npg
    per_b = lambda a: pl.BlockSpec((1,) + a.shape[1:], lambda b, j, pt: (b, 0, 0))
    small = lambda a: pl.BlockSpec(a.shape, lambda b, j, pt: (0, 0))
    in_hbm = pl.BlockSpec(memory_space=pl.ANY)
    grid_spec = pltpu.PrefetchScalarGridSpec(
        num_scalar_prefetch=1,
        grid=(nb, nsteps),
        in_specs=([per_b(q_rows), per_b(k_own), per_b(v_own), in_hbm, in_hbm]
                  + [small(a) for a in lam_params] + [small(g)]),
        out_specs=pl.BlockSpec((1, nq // 2, LANES), lambda b, j, pt: (b, 0, 0)),
        scratch_shapes=[pltpu.VMEM((PAGE_RING, npg, prow, LANES), cache_k.dtype),
                        pltpu.VMEM((PAGE_RING, npg, prow, LANES), cache_v.dtype),
                        pltpu.SemaphoreType.DMA((2, PAGE_RING, npg)),
                        pltpu.VMEM((nh, npg * prow // nh, LANES), BF16),
                        pltpu.VMEM((nh, npg * prow // nh, LANES), BF16),
                        pltpu.VMEM((nq, 1), F32),
                        pltpu.VMEM((nq, 1), F32),
                        pltpu.VMEM((nq, LANES), F32)])
    return pl.pallas_call(
        functools.partial(_sattn_kernel, layer=layer, nsteps=nsteps, total=nb * nsteps, npg=npg,
                          prow=prow, dq=dq, nh=nh, lambda_init=lambda_init),
        grid_spec=grid_spec,
        out_shape=jax.ShapeDtypeStruct((nb, nq // 2, LANES), BF16),
        compiler_params=_cparams(("arbitrary", "arbitrary")),
        name="attn_sample",
    )(page_table, q_rows, k_own, v_own, cache_k, cache_v, *lam_params, g)


def _merge_kernel(x_ref, gates_ref, y_ref, o_ref, wbs_ref, wba_ref, wout_ref, gffn_ref,
                  wr_ref, br_ref, x1_ref, h2_ref, ri_ref, rf_ref, cnt_ref, run_ref, *, ne):
    first = jnp.logical_and(pl.program_id(0) == 0, pl.program_id(1) == 0)

    @pl.when(first)
    def _init():
        run_ref[...] = jnp.zeros(run_ref.shape, F32)

    tm, d = x_ref.shape
    a = _dot(y_ref[...].astype(BF16), wbs_ref[...])
    b = _dot(o_ref[...], wba_ref[...])
    merged = gates_ref[:, :d].astype(F32) * a + gates_ref[:, d:].astype(F32) * b
    x1 = x_ref[...] + _dot(merged.astype(BF16), wout_ref[...])
    x1_ref[...] = x1
    h2 = _rms(x1, gffn_ref[...])
    h2b = h2.astype(BF16)
    h2_ref[...] = _pack_bf16_pairs(h2)

    logits = _dot(h2b, wr_ref[...]) + br_ref[...]
    lane = lax.broadcasted_iota(jnp.int32, logits.shape, 1)
    lane_f = lane.astype(F32)
    cur = jnp.where(lane < ne, logits, -jnp.inf)
    vals, idxs, hots = [], [], []
    for _ in range(TOP_K):
        mk = jnp.max(cur, axis=1, keepdims=True)
        ik = jnp.min(jnp.where(cur == mk, lane_f, float(LANES)), axis=1, keepdims=True)
        hot = lane_f == ik
        cur = jnp.where(hot, -jnp.inf, cur)
        vals.append(mk)
        idxs.append(ik.astype(jnp.int32))
        hots.append(hot)
    exps = [jnp.exp(v - vals[0]) for v in vals]
    denom = sum(exps[1:], exps[0])
    chosen = functools.reduce(jnp.logical_or, hots)
    cnt = jnp.where(chosen, 1.0, 0.0)
    r = lax.broadcasted_iota(jnp.int32, (tm, tm), 0)
    c = lax.broadcasted_iota(jnp.int32, (tm, tm), 1)
    before = _dot((c < r).astype(BF16), cnt.astype(BF16)) + run_ref[...]
    ri = jnp.zeros(logits.shape, jnp.int32)
    rf = jnp.zeros(logits.shape, F32)
    for k in range(TOP_K):
        rank = jnp.sum(jnp.where(hots[k], before, 0.0), axis=1, keepdims=True).astype(jnp.int32)
        ri = jnp.where(lane == k, idxs[k], ri)
        ri = jnp.where(lane == TOP_K + k, rank, ri)
        rf = jnp.where(lane == k, exps[k] / denom, rf)
    ri_ref[...] = ri.T[:2 * TOP_K]
    rf_ref[...] = rf
    run_ref[...] = run_ref[...] + jnp.sum(cnt, axis=0, keepdims=True)
    cnt_ref[...] = run_ref[...]


def _merge(x2d, gates, y_tb, o, n_batch, t_len, tm, wbs, wba, wout, gffn, wr, br, ne):
    n, d = x2d.shape
    w = o.shape[1]
    nt = t_len // tm
    row = lambda b, t: (b * nt + t, 0)
    const = lambda b, t: (0, 0)
    return pl.pallas_call(
        functools.partial(_merge_kernel, ne=ne),
        grid=(n_batch, nt),
        in_specs=[pl.BlockSpec((tm, d), row),
                  pl.BlockSpec((tm, 2 * d), row),
                  pl.BlockSpec((tm, w), lambda b, t: (t, b)),
                  pl.BlockSpec((tm, w), row),
                  pl.BlockSpec(wbs.shape, const),
                  pl.BlockSpec(wba.shape, const),
                  pl.BlockSpec(wout.shape, const),
                  pl.BlockSpec((1, d), const),
                  pl.BlockSpec(wr.shape, const),
                  pl.BlockSpec((1, LANES), const)],
        out_specs=[pl.BlockSpec((tm, d), row),
                   pl.BlockSpec((tm, d // 2), row),
                   pl.BlockSpec((2 * TOP_K, tm), lambda b, t: (0, b * nt + t)),
                   pl.BlockSpec((tm, LANES), row),
                   pl.BlockSpec((1, LANES), const)],
        out_shape=[jax.ShapeDtypeStruct((n, d), F32),
                   jax.ShapeDtypeStruct((n, d // 2), jnp.uint32),
                   jax.ShapeDtypeStruct((2 * TOP_K, n), jnp.int32),
                   jax.ShapeDtypeStruct((n, LANES), F32),
                   jax.ShapeDtypeStruct((1, LANES), F32)],
        scratch_shapes=[pltpu.VMEM((1, LANES), F32)],
        compiler_params=_cparams(("arbitrary", "arbitrary")),
        name="merge",
    )(x2d, gates, y_tb, o, wbs, wba, wout, gffn, wr, br)


def _moe_kernel(be_ref, nu_ref, x_ref, wgu_ref, bgu_ref, wd_ref, bd_ref, o_ref, wgu_s, wd_s):
    i = pl.program_id(0)
    active = i < nu_ref[0]
    fresh = jnp.logical_or(i == 0, be_ref[i] != be_ref[jnp.maximum(i - 1, 0)])
    pc = PAIR_COLS
    hc = pc // 2
    n_chunk = wgu_ref.shape[2] // pc

    @pl.when(jnp.logical_and(active, fresh))
    def _stage_weights():
        r = lax.broadcasted_iota(jnp.int32, (pc, pc), 0)
        c = lax.broadcasted_iota(jnp.int32, (pc, pc), 1)
        perm = (r == jnp.where(c < hc, 2 * c, 2 * (c - hc) + 1)).astype(BF16)
        for j in range(n_chunk):
            wj = wgu_ref[0, :, j * pc:(j + 1) * pc].astype(BF16)
            wgu_s[:, j * pc:(j + 1) * pc] = _dot(wj, perm).astype(BF16)
        wd_s[...] = wd_ref[0].astype(BF16)

    @pl.when(active)
    def _compute():
        x = _unpack_bf16_pairs(x_ref[...]).astype(BF16)
        gu = _dot(x, wgu_s[...]) + bgu_ref[0]
        acts = []
        for j in range(n_chunk):
            g_lin = jnp.minimum(gu[:, j * pc:j * pc + hc], SWIGLU_LIMIT)
            up = jnp.clip(gu[:, j * pc + hc:(j + 1) * pc], -SWIGLU_LIMIT, SWIGLU_LIMIT)
            acts.append(((up + 1.0) * (g_lin * jax.nn.sigmoid(SWIGLU_ALPHA * g_lin))).astype(BF16))
        o_ref[...] = _pack_bf16_pairs(_dot(jnp.concatenate(acts, axis=1), wd_s[...]) + bd_ref[0])

    @pl.when(jnp.logical_not(active))
    def _skip():
        o_ref[...] = jnp.zeros(o_ref.shape, o_ref.dtype)


def _moe_block_rows(n_tokens, ne):
    return MOE_BLOCK_ROWS if n_tokens * TOP_K >= 8 * ne * MOE_BLOCK_ROWS else MOE_BLOCK_ROWS_SMALL


def _moe_gemm(xs, blk_expert, n_used, wgu, bgu, wd, bd, bm):
    n_rows = xs.shape[0]
    d = wd.shape[2]
    wspec = lambda a: pl.BlockSpec((1,) + a.shape[1:], lambda i, be, nu: (be[i], 0, 0))
    grid_spec = pltpu.PrefetchScalarGridSpec(
        num_scalar_prefetch=2,
        grid=(n_rows // bm,),
        in_specs=[pl.BlockSpec((bm, xs.shape[1]), lambda i, be, nu: (i, 0)),
                  wspec(wgu), wspec(bgu), wspec(wd), wspec(bd)],
        out_specs=pl.BlockSpec((bm, d // 2), lambda i, be, nu: (i, 0)),
        scratch_shapes=[pltpu.VMEM(wgu.shape[1:], BF16),
                        pltpu.VMEM(wd.shape[1:], BF16)])
    return pl.pallas_call(
        _moe_kernel,
        grid_spec=grid_spec,
        out_shape=jax.ShapeDtypeStruct((n_rows, d // 2), jnp.uint32),
        compiler_params=_cparams(("arbitrary",)),
        name="moe_gemm",
    )(blk_expert, n_used, xs, wgu, bgu, wd, bd)


def _combine_kernel(x_ref, *rest, final_norm):
    yg_refs = rest[:TOP_K]
    gate_ref, g_ref, o_ref = rest[TOP_K:]
    acc = x_ref[...]
    gate = gate_ref[...]
    for k in range(TOP_K):
        acc = acc + gate[:, k:k + 1] * _unpack_bf16_pairs(yg_refs[k][...])
    o_ref[...] = _rms(acc, g_ref[...]) if final_norm else acc


def _combine(x1, ygs, gate, g, tm, final_norm):
    n, d = x1.shape
    row = pl.BlockSpec((tm, d), lambda i: (i, 0))
    packed = pl.BlockSpec((tm, d // 2), lambda i: (i, 0))
    return pl.pallas_call(
        functools.partial(_combine_kernel, final_norm=final_norm),
        grid=(n // tm,),
        in_specs=[row] + [packed] * TOP_K + [pl.BlockSpec((tm, LANES), lambda i: (i, 0)),
                                          pl.BlockSpec((1, d), lambda i: (0, 0))],
        out_specs=row,
        out_shape=jax.ShapeDtypeStruct((n, d), F32),
        compiler_params=_cparams(("parallel",)),
        name="combine",
    )(x1, *ygs, gate, g)


def _dest_kernel(pstart_ref, ri_ref, o_ref):
    ri = ri_ref[...]
    base = jnp.zeros(ri.shape, jnp.int32)
    for e in range(pstart_ref.shape[0]):
        base = jnp.where(ri == e, pstart_ref[e], base)
    o_ref[...] = (base + pltpu.roll(ri, TOP_K, axis=0))[:TOP_K]


def _dest_rows(pstart, route_i):
    n = route_i.shape[1]
    grid_spec = pltpu.PrefetchScalarGridSpec(
        num_scalar_prefetch=1,
        grid=(1,),
        in_specs=[pl.BlockSpec(route_i.shape, lambda i, ps: (0, 0))],
        out_specs=pl.BlockSpec((TOP_K, n), lambda i, ps: (0, 0)))
    return pl.pallas_call(
        _dest_kernel,
        grid_spec=grid_spec,
        out_shape=jax.ShapeDtypeStruct((TOP_K, n), jnp.int32),
        compiler_params=_cparams(("arbitrary",)),
        name="dest_rows",
    )(pstart, route_i)


def _moe_dispatch(h2p, route_i, count_row, ne):
    n = h2p.shape[0]
    bm = _moe_block_rows(n, ne)
    m = n * TOP_K
    counts = count_row[0, :ne].astype(jnp.int32)
    start = jnp.cumsum(counts) - counts
    padded = (counts + bm - 1) // bm * bm
    pend = jnp.cumsum(padded)
    pstart = pend - padded
    dest_rows = _dest_rows(pstart.astype(jnp.int32), route_i)
    dest = [dest_rows[k] for k in range(TOP_K)]
    n_rows = (m + bm - 1) // bm * bm + ne * bm
    blk_start = jnp.arange(n_rows // bm, dtype=jnp.int32) * bm
    blk_expert = jnp.minimum(
        jnp.sum((blk_start[:, None] >= pend[None, :]).astype(jnp.int32), axis=1), ne - 1)
    n_used = (pend[-1] // bm).astype(jnp.int32).reshape(1)
    tok = jnp.tile(jnp.arange(n, dtype=jnp.int32), TOP_K)
    _, tok_sorted = lax.sort_key_val(dest_rows.reshape(m), tok)
    blk_shift = (pstart - start)[blk_expert]
    src = jnp.arange(n_rows, dtype=jnp.int32) - jnp.repeat(blk_shift, bm)
    row_tok = tok_sorted[jnp.clip(src, 0, m - 1)]
    return dict(xs=h2p[row_tok], blk_expert=blk_expert, n_used=n_used, dest=dest, bm=bm)


def _moe_apply(x1, disp, route_f, mp, g_final, final_norm):
    ys = _moe_gemm(disp['xs'], disp['blk_expert'], disp['n_used'],
                   mp['wgu'], mp['bgu'], mp['wd'], mp['bd'], disp['bm'])
    ygs = [ys[disp['dest'][k]] for k in range(TOP_K)]
    tm = min(ROW_TILE, x1.shape[0])
    return _combine(x1, ygs, route_f, g_final, tm, final_norm)


def _rope_tables(pos, hd):
    half = hd // 2
    inv_freq = ROPE_THETA ** (-jnp.arange(half, dtype=F32) / half)
    ang = pos.astype(F32)[:, None] * inv_freq[None, :]
    cos = jnp.cos(ang)
    sin = jnp.sin(ang)
    reps = LANES // hd
    cos_t = jnp.tile(jnp.concatenate([cos, cos], axis=1), (1, reps))
    sin_t = jnp.tile(jnp.concatenate([-sin, sin], axis=1), (1, reps))
    return cos_t, sin_t


def _layer_params(lp):
    d = lp['w_in'].shape[0]
    ne = lp['w_router'].shape[1]
    wr = jnp.zeros((d, LANES), F32).at[:, :ne].set(lp['w_router']).astype(BF16)
    br = jnp.zeros((1, LANES), F32).at[0, :ne].set(lp['b_router'])
    return dict(
        g_mix=lp['norm_mix_g'].reshape(1, d),
        win=lp['w_in'].astype(BF16),
        wgate=lp['w_gate'].astype(BF16),
        bgate=lp['b_gate'].reshape(1, -1),
        s5=_s5_layout(lp),
        lam=[lp[k].reshape(1, -1) for k in ('lambda_q1', 'lambda_k1', 'lambda_q2', 'lambda_k2')],
        g_sub=lp['attn_subln_g'].reshape(1, -1),
        wbs=lp['w_branch_ssm'].astype(BF16),
        wba=lp['w_branch_attn'].astype(BF16),
        wout=lp['w_out'].astype(BF16),
        g_ffn=lp['norm_ffn_g'].reshape(1, d),
        wr=wr, br=br,
        moe=dict(ne=ne,
                 wgu=lp['w_gate_up'],
                 bgu=lp['b_gate_up'].reshape(ne, -1, PAIR_COLS // 2, 2).transpose(0, 1, 3, 2)
                 .reshape(ne, 1, -1),
                 wd=lp['w_down'],
                 bd=lp['b_down'][:, None, :]))


def _query_rows(qb, bsz, dq, nh, hd):
    qt = qb.astype(F32).reshape(bsz, dq, nh, 2, hd).transpose(0, 2, 3, 1, 4)
    ec = jnp.eye(2, dtype=jnp.bool_)[None, None, :, None, :, None]
    out = jnp.where(ec, qt[:, :, :, :, None, :], 0.0)
    return out.reshape(bsz, nh * 2 * dq, 2 * hd)


def _layer(x, pos, x0_re, x0_im, past, P, lambda_init):
    bsz, t_len, d = x.shape
    n = bsz * t_len
    x2d = x.reshape(n, d)
    sp = P['s5']
    g, p = x0_re.shape[1], x0_re.shape[2]
    w = P['win'].shape[1] // 4
    nh = w // LANES
    hd = LANES // 2
    cos_t, sin_t = _rope_tables(pos, hd)
    x0 = _state_to_cols(x0_re.astype(F32), x0_im.astype(F32), sp['ns'])
    if past is None:
        n_batch, rows_t, tm = bsz, t_len, min(ROW_TILE, t_len)
    else:
        n_batch, rows_t, tm = 1, n, n
        cos_t = jnp.tile(cos_t, (bsz, 1))
        sin_t = jnp.tile(sin_t, (bsz, 1))
    u, qb, k, v, kb, vb, gates = _proj(x2d, n_batch, rows_t, tm, P['g_mix'], P['win'], P['wgate'],
                                       P['bgate'], cos_t, sin_t, hd // 2, hd ** -0.5)
    if past is None:
        y_in, st = _s5(u, x0, sp, bsz, t_len, min(64, t_len), BF16)
        o = _attn(qb, kb, vb, P['lam'], P['g_sub'], bsz, t_len, min(256, t_len), hd, lambda_init)
    else:
        cache_k, cache_v, page_table, layer = past
        u_tb = u.reshape(bsz, t_len, w).transpose(1, 0, 2).reshape(t_len, bsz * w)
        y_tb, st = _s5(u_tb, x0, sp, bsz, t_len, t_len, F32)
        y_in = y_tb.reshape(t_len, bsz, w).transpose(1, 0, 2).reshape(n, w)
        npg = math.gcd(PAGES_PER_STEP, page_table.shape[1])
        cshape = cache_k.shape[:2] + (cache_k.shape[2] * nh, LANES)
        o = _sattn(page_table, _query_rows(qb, bsz, t_len, nh, hd),
                   k.reshape(bsz, t_len * nh, LANES), v.reshape(bsz, t_len * nh, LANES),
                   cache_k.reshape(cshape), cache_v.reshape(cshape), layer,
                   P['lam'], P['g_sub'], npg, t_len, nh, lambda_init)
        o = o.reshape(bsz, nh, t_len, LANES).transpose(0, 2, 1, 3).reshape(n, w)
    x1, h2p, route_i, route_f, count_row = _merge(
        x2d, gates, y_in, o, n_batch, rows_t, tm, P['wbs'], P['wba'], P['wout'], P['g_ffn'],
        P['wr'], P['br'], P['moe']['ne'])
    disp = _moe_dispatch(h2p, route_i, count_row, P['moe']['ne'])
    st_re, st_im = _cols_to_state(st, sp['ns'], g, p)
    side = (k.reshape(bsz, t_len, nh, LANES), v.reshape(bsz, t_len, nh, LANES),
            st_re.astype(x.dtype), st_im.astype(x.dtype))
    return (x1, disp, route_f), side


def kernel(x_prompt, x_sample, cache_k, cache_v, state_ssm_re, state_ssm_im, page_table, norm_mix_g, w_in, ssm_lambda_re, ssm_lambda_im, ssm_log_dt, ssm_b_re, ssm_b_im, ssm_c_re, ssm_c_im, ssm_d, ssm_w_glu, ssm_b_glu, lambda_q1, lambda_k1, lambda_q2, lambda_k2, attn_subln_g, w_branch_ssm, w_branch_attn, w_gate, b_gate, w_out, norm_ffn_g, w_router, b_router, w_gate_up, b_gate_up, w_down, b_down, norm_final_g):
    depth = w_in.shape[0]
    past_len = page_table.shape[1] * cache_k.shape[2]
    pos_prompt = jnp.arange(x_prompt.shape[1], dtype=jnp.int32)
    pos_sample = past_len + jnp.arange(x_sample.shape[1], dtype=jnp.int32)
    g, p = state_ssm_re.shape[2], state_ssm_re.shape[3]
    zero_state = jnp.zeros((x_prompt.shape[0], g, p), F32)
    g_final = norm_final_g.reshape(1, -1)
    names = ('norm_mix_g', 'w_in', 'ssm_lambda_re', 'ssm_lambda_im', 'ssm_log_dt', 'ssm_b_re',
             'ssm_b_im', 'ssm_c_re', 'ssm_c_im', 'ssm_d', 'ssm_w_glu', 'ssm_b_glu', 'lambda_q1',
             'lambda_k1', 'lambda_q2', 'lambda_k2', 'attn_subln_g', 'w_branch_ssm',
             'w_branch_attn', 'w_gate', 'b_gate', 'w_out', 'norm_ffn_g', 'w_router', 'b_router',
             'w_gate_up', 'b_gate_up', 'w_down', 'b_down')
    stacked = (norm_mix_g, w_in, ssm_lambda_re, ssm_lambda_im, ssm_log_dt, ssm_b_re, ssm_b_im,
               ssm_c_re, ssm_c_im, ssm_d, ssm_w_glu, ssm_b_glu, lambda_q1, lambda_k1, lambda_q2,
               lambda_k2, attn_subln_g, w_branch_ssm, w_branch_attn, w_gate, b_gate, w_out,
               norm_ffn_g, w_router, b_router, w_gate_up, b_gate_up, w_down, b_down)
    xp, xs = x_prompt, x_sample
    outs_p, outs_s = [], []
    for l in range(depth):
        P = _layer_params({nm: a[l] for nm, a in zip(names, stacked)})
        lambda_init = 0.8 - 0.6 * math.exp(-0.3 * l)
        last = l == depth - 1
        moe_p, side_p = _layer(xp, pos_prompt, zero_state, zero_state, None, P, lambda_init)
        moe_s, side_s = _layer(xs, pos_sample, state_ssm_re[l], state_ssm_im[l],
                               (cache_k, cache_v, page_table, l), P, lambda_init)
        (x1_p, disp_p, rf_p), (x1_s, disp_s, rf_s) = moe_p, moe_s
        n_used_p, x1_s = lax.optimization_barrier((disp_p['n_used'], x1_s))
        disp_p = dict(disp_p, n_used=n_used_p)
        xp = _moe_apply(x1_p, disp_p, rf_p, P['moe'], g_final, last).reshape(xp.shape)
        xs = _moe_apply(x1_s, disp_s, rf_s, P['moe'], g_final, last).reshape(xs.shape)
        outs_p.append(side_p)
        outs_s.append(side_s)
    stack = lambda outs, i: jnp.stack([o[i] for o in outs])
    return (xp, xs, stack(outs_p, 0), stack(outs_p, 1), stack(outs_p, 2), stack(outs_p, 3),
            stack(outs_s, 0), stack(outs_s, 1), stack(outs_s, 2), stack(outs_s, 3))
```

```python
import functools
import math

import jax
import jax.numpy as jnp
from jax import lax
from jax.experimental import pallas as pl
from jax.experimental.pallas import tpu as pltpu

F32 = jnp.float32
BF16 = jnp.bfloat16

NORM_EPS = 1e-6
ROPE_THETA = 10000.0
NEG_INF = -1e30
TOP_K = 4
SWIGLU_LIMIT = 7.0
SWIGLU_ALPHA = 1.702
LANES = 128
MOE_BLOCK_ROWS = 512
MOE_BLOCK_ROWS_SMALL = 128
ROW_TILE = 512
PAIR_COLS = 256
ATTN_KEY_CHUNK = 2048
PAGE_RING = 3
PAGES_PER_STEP = 16
VMEM_LIMIT = 56 * 1024 * 1024


def _cparams(sem):
    return pltpu.CompilerParams(dimension_semantics=sem, vmem_limit_bytes=VMEM_LIMIT)


def _dot(a, b):
    return jnp.dot(a, b, preferred_element_type=F32)


def _dot_nt(a, b):
    return lax.dot_general(a, b, (((1,), (1,)), ((), ())), preferred_element_type=F32)


def _pack_bf16_pairs(x):
    half = x.shape[1] // 2
    bits = lax.bitcast_convert_type(x.astype(BF16).astype(F32), jnp.uint32)
    return bits[:, half:] | (bits[:, :half] >> 16)


def _unpack_bf16_pairs(words):
    return jnp.concatenate(
        [lax.bitcast_convert_type(words << 16, F32),
         lax.bitcast_convert_type(words & jnp.uint32(0xFFFF0000), F32)], axis=1)


def _rms(x, g):
    return x * lax.rsqrt(jnp.mean(x * x, axis=-1, keepdims=True) + NORM_EPS) * g


def _proj_kernel(x_ref, g_ref, win_ref, wgate_ref, bgate_ref, cos_ref, sin_ref,
                 u_ref, q_ref, k_ref, v_ref, kb_ref, vb_ref, gates_ref, *, half, scale):
    hb = _rms(x_ref[...], g_ref[...]).astype(BF16)
    proj = _dot(hb, win_ref[...])
    w = u_ref.shape[-1]
    u_ref[...] = proj[:, :w]
    cos = cos_ref[...]
    sin = sin_ref[...]
    lane = lax.broadcasted_iota(jnp.int32, cos.shape, 1)
    lo = (lane % (2 * half)) < half

    def rot(xh):
        fwd = pltpu.roll(xh, LANES - half, axis=1)
        bwd = pltpu.roll(xh, half, axis=1)
        return xh * cos + jnp.where(lo, fwd, bwd) * sin

    nh = w // LANES
    tm = x_ref.shape[0]
    for hh in range(nh):
        sl = slice(hh * LANES, (hh + 1) * LANES)
        qh = rot(proj[:, w + hh * LANES:w + (hh + 1) * LANES])
        q_ref[:, sl] = (qh * scale).astype(BF16)
        kh = rot(proj[:, 2 * w + hh * LANES:2 * w + (hh + 1) * LANES])
        k_ref[pl.ds(hh, tm, stride=nh), :] = kh
        kb_ref[:, sl] = kh.astype(BF16)
        v_ref[pl.ds(hh, tm, stride=nh), :] = proj[:, 3 * w + hh * LANES:3 * w + (hh + 1) * LANES]
    vb_ref[...] = proj[:, 3 * w:].astype(BF16)
    gl = _dot(hb, wgate_ref[...]) + bgate_ref[...]
    gates_ref[...] = jax.nn.sigmoid(gl).astype(BF16)


def _proj(x2d, n_batch, t_len, tm, g, win_b, wgate_b, bgate, cos_t, sin_t, half, scale):
    n, d = x2d.shape
    w = win_b.shape[1] // 4
    nh = w // LANES
    nt = t_len // tm
    row = lambda b, t: (b * nt + t, 0)
    const = lambda b, t: (0, 0)
    outs = pl.pallas_call(
        functools.partial(_proj_kernel, half=half, scale=scale),
        grid=(n_batch, nt),
        in_specs=[pl.BlockSpec((tm, d), row),
                  pl.BlockSpec((1, d), const),
                  pl.BlockSpec(win_b.shape, const),
                  pl.BlockSpec(wgate_b.shape, const),
                  pl.BlockSpec((1, wgate_b.shape[1]), const),
                  pl.BlockSpec((tm, LANES), lambda b, t: (t, 0)),
                  pl.BlockSpec((tm, LANES), lambda b, t: (t, 0))],
        out_specs=[pl.BlockSpec((tm, w), lambda b, t: (t, b)),
                   pl.BlockSpec((tm, w), row),
                   pl.BlockSpec((tm * nh, LANES), row),
                   pl.BlockSpec((tm * nh, LANES), row),
                   pl.BlockSpec((tm, w), row),
                   pl.BlockSpec((tm, w), row),
                   pl.BlockSpec((tm, 2 * d), row)],
        out_shape=[jax.ShapeDtypeStruct((t_len, n_batch * w), F32),
                   jax.ShapeDtypeStruct((n, w), BF16),
                   jax.ShapeDtypeStruct((n * nh, LANES), F32),
                   jax.ShapeDtypeStruct((n * nh, LANES), F32),
                   jax.ShapeDtypeStruct((n, w), BF16),
                   jax.ShapeDtypeStruct((n, w), BF16),
                   jax.ShapeDtypeStruct((n, 2 * d), BF16)],
        compiler_params=_cparams(("parallel", "parallel")),
        name="proj",
    )(x2d, g, win_b, wgate_b, bgate, cos_t, sin_t)
    return outs


def _s5_kernel(u_ref, x0_ref, lre_ref, lim_ref, ldt_ref, bre_ref, bim_ref, cre_ref, cim_ref,
               d_ref, wglu_ref, bglu_ref, y_ref, st_ref,
               zx_ref, wz_ref, are_ref, aim_ref, state_ref, us_ref, ys_ref, *, nb, tc, ns, sw):
    step = pl.program_id(0)
    w = ns * LANES

    @pl.when(step == 0)
    def _init():
        lre = lre_ref[...]
        lim = lim_ref[...]
        dt = jnp.exp(ldt_ref[...])
        mag = jnp.exp(lre * dt)
        are = mag * jnp.cos(lim * dt)
        aim = mag * jnp.sin(lim * dt)
        den = lre * lre + lim * lim
        fre = ((are - 1.0) * lre + aim * lim) / den
        fim = (aim * lre - (are - 1.0) * lim) / den
        are_ref[...] = are
        aim_ref[...] = aim
        for i in range(ns):
            fr = fre[:, i * sw:(i + 1) * sw]
            fi = fim[:, i * sw:(i + 1) * sw]
            br = bre_ref[i]
            bi = bim_ref[i]
            wz_ref[i, :, :sw] = (fr * br - fi * bi).astype(BF16)
            wz_ref[i, :, sw:] = (fr * bi + fi * br).astype(BF16)
        state_ref[...] = x0_ref[...]

    for b in range(nb):
        for c in range(ns):
            us_ref[c, pl.ds(b, tc, stride=nb), :] = u_ref[:, b * w + c * LANES:b * w + (c + 1) * LANES]
    ub = jnp.concatenate([us_ref[c] for c in range(ns)], axis=1)
    ubb = ub.astype(BF16)
    for i in range(ns):
        zx_ref[:, 2 * sw * i:2 * sw * (i + 1)] = _dot(ubb[:, LANES * i:LANES * (i + 1)], wz_ref[i])

    for i in range(ns):
        c0 = 2 * sw * i
        arb = jnp.broadcast_to(are_ref[:, i * sw:(i + 1) * sw], (nb, sw))
        aib = jnp.broadcast_to(aim_ref[:, i * sw:(i + 1) * sw], (nb, sw))

        def body(t, carry, c0=c0, arb=arb, aib=aib):
            xr, xi = carry
            r0 = pl.multiple_of(t * nb, nb)
            zr = zx_ref[pl.ds(r0, nb), c0:c0 + sw]
            zi = zx_ref[pl.ds(r0, nb), c0 + sw:c0 + 2 * sw]
            nxr = arb * xr - aib * xi + zr
            nxi = arb * xi + aib * xr + zi
            zx_ref[pl.ds(r0, nb), c0:c0 + sw] = nxr
            zx_ref[pl.ds(r0, nb), c0 + sw:c0 + 2 * sw] = nxi
            return nxr, nxi

        xr, xi = lax.fori_loop(0, tc, body,
                               (state_ref[:, c0:c0 + sw], state_ref[:, c0 + sw:c0 + 2 * sw]))
        state_ref[:, c0:c0 + sw] = xr
        state_ref[:, c0 + sw:c0 + 2 * sw] = xi

    ys = []
    for i in range(ns):
        c0 = 2 * sw * i
        xr = zx_ref[:, c0:c0 + sw].astype(BF16)
        xi = zx_ref[:, c0 + sw:c0 + 2 * sw].astype(BF16)
        ys.append(_dot(xr, cre_ref[i]) - _dot(xi, cim_ref[i]))
    y = jnp.concatenate(ys, axis=1) + d_ref[...] * ub
    y = jax.nn.gelu(y)
    gl = _dot(y.astype(BF16), wglu_ref[...]) + bglu_ref[...]
    yv = y * jax.nn.sigmoid(gl)
    for c in range(ns):
        ys_ref[c] = yv[:, c * LANES:(c + 1) * LANES]
    for b in range(nb):
        for c in range(ns):
            y_ref[:, b * w + c * LANES:b * w + (c + 1) * LANES] = (
                ys_ref[c, pl.ds(b, tc, stride=nb), :].astype(y_ref.dtype))

    @pl.when(step == pl.num_programs(0) - 1)
    def _fin():
        st_ref[...] = state_ref[...]


def _s5_layout(lp):
    g, p, h = lp['ssm_b_re'].shape
    gps = LANES // h
    ns = g // gps
    eye = jnp.eye(gps, dtype=jnp.bool_)

    def bd_b(b):
        bb = b.reshape(ns, gps, p, h).transpose(0, 1, 3, 2)
        out = jnp.where(eye[None, :, None, :, None], bb[:, :, :, None, :], 0.0)
        return out.reshape(ns, gps * h, gps * p)

    def bd_c(c):
        cc = c.reshape(ns, gps, h, p).transpose(0, 1, 3, 2)
        out = jnp.where(eye[None, :, None, :, None], cc[:, :, :, None, :], 0.0)
        return out.reshape(ns, gps * p, gps * h)

    return dict(
        lre=lp['ssm_lambda_re'].reshape(1, g * p),
        lim=lp['ssm_lambda_im'].reshape(1, g * p),
        ldt=jnp.repeat(lp['ssm_log_dt'], p).reshape(1, g * p),
        bre=bd_b(lp['ssm_b_re']), bim=bd_b(lp['ssm_b_im']),
        cre=bd_c(lp['ssm_c_re']).astype(BF16), cim=bd_c(lp['ssm_c_im']).astype(BF16),
        d=lp['ssm_d'].reshape(1, g * h),
        wglu=lp['ssm_w_glu'].astype(BF16), bglu=lp['ssm_b_glu'].reshape(1, -1),
        ns=ns, sw=gps * p)


def _state_to_cols(re, im, ns):
    b = re.shape[0]
    return jnp.stack([re.reshape(b, ns, -1), im.reshape(b, ns, -1)], axis=2).reshape(b, -1)


def _cols_to_state(st, ns, g, p):
    b = st.shape[0]
    s = st.reshape(b, ns, 2, -1)
    return s[:, :, 0].reshape(b, g, p), s[:, :, 1].reshape(b, g, p)


def _s5(u, x0, sp, nb, t_len, tc, y_dtype):
    rows = nb * tc
    ns, sw = sp['ns'], sp['sw']
    ncol = ns * 2 * sw
    full = lambda a: pl.BlockSpec(a.shape, lambda i: (0,) * a.ndim)
    args = (u, x0, sp['lre'], sp['lim'], sp['ldt'], sp['bre'], sp['bim'], sp['cre'], sp['cim'],
            sp['d'], sp['wglu'], sp['bglu'])
    return pl.pallas_call(
        functools.partial(_s5_kernel, nb=nb, tc=tc, ns=ns, sw=sw),
        grid=(t_len // tc,),
        in_specs=[pl.BlockSpec((tc, u.shape[1]), lambda i: (i, 0))] + [full(a) for a in args[1:]],
        out_specs=[pl.BlockSpec((tc, u.shape[1]), lambda i: (i, 0)),
                   pl.BlockSpec((nb, ncol), lambda i: (0, 0))],
        out_shape=[jax.ShapeDtypeStruct(u.shape, y_dtype),
                   jax.ShapeDtypeStruct((nb, ncol), F32)],
        scratch_shapes=[pltpu.VMEM((rows, ncol), F32),
                        pltpu.VMEM((ns, LANES, 2 * sw), BF16),
                        pltpu.VMEM((1, ns * sw), F32),
                        pltpu.VMEM((1, ns * sw), F32),
                        pltpu.VMEM((nb, ncol), F32),
                        pltpu.VMEM((ns, rows, LANES), F32),
                        pltpu.VMEM((ns, rows, LANES), F32)],
        compiler_params=_cparams(("arbitrary",)),
        name="s5",
    )(*args)


def _diff_lambda(lq1, lk1, lq2, lk2, lambda_init):
    return (jnp.exp(jnp.sum(lq1[...] * lk1[...], keepdims=True))
            - jnp.exp(jnp.sum(lq2[...] * lk2[...], keepdims=True)) + lambda_init)


def _attn_kernel(q_ref, k_ref, v_ref, lq1, lk1, lq2, lk2, g_ref, o_ref, *, tq, hd, lambda_init):
    t_len = q_ref.shape[0]
    lam = _diff_lambda(lq1, lk1, lq2, lk2, lambda_init)
    lo = lax.broadcasted_iota(jnp.int32, (tq, LANES), 1) < hd
    max_blocks = max(ATTN_KEY_CHUNK // tq, 1)

    for qi in range(t_len // tq):
        q = q_ref[qi * tq:(qi + 1) * tq, :]
        zero = jnp.zeros_like(q)
        qs = jnp.concatenate([jnp.where(lo, q, zero), jnp.where(lo, zero, q)], axis=0)
        m = jnp.full((1, 2 * tq), NEG_INF, F32)
        l = jnp.zeros((1, 2 * tq), F32)
        acc = jnp.zeros((LANES, 2 * tq), F32)
        n_blocks = qi + 1
        n_chunks = -(-n_blocks // max_blocks)
        k0 = 0
        for ci in range(n_chunks):
            ksz = (n_blocks // n_chunks + (1 if ci < n_blocks % n_chunks else 0)) * tq
            s = _dot_nt(k_ref[k0:k0 + ksz, :], qs)
            if ci == n_chunks - 1:
                kpos = k0 + lax.broadcasted_iota(jnp.int32, (ksz, 2 * tq), 0)
                qc = lax.broadcasted_iota(jnp.int32, (ksz, 2 * tq), 1)
                s = jnp.where(kpos <= qi * tq + jnp.where(qc >= tq, qc - tq, qc), s, NEG_INF)
            m_new = jnp.maximum(m, jnp.max(s, axis=0, keepdims=True))
            alpha = jnp.exp(m - m_new)
            p = jnp.exp(s - m_new)
            l = alpha * l + jnp.sum(p, axis=0, keepdims=True)
            pv = lax.dot_general(v_ref[k0:k0 + ksz, :], p.astype(BF16), (((0,), (0,)), ((), ())),
                                 preferred_element_type=F32)
            acc = alpha * acc + pv
            m = m_new
            k0 += ksz
        o_t = (acc[:, :tq] / l[:, :tq]) - lam * (acc[:, tq:] / l[:, tq:])
        ms = jnp.mean(o_t * o_t, axis=0, keepdims=True)
        o_t = o_t * lax.rsqrt(ms + NORM_EPS) * g_ref[...] * (1.0 - lambda_init)
        o_ref[qi * tq:(qi + 1) * tq, :] = o_t.T.astype(BF16)


def _attn(qb, kb, vb, lam_params, g, n_batch, t_len, tq, hd, lambda_init):
    n, w = qb.shape
    nh = w // LANES
    blk = pl.BlockSpec((t_len, LANES), lambda b, h: (b, h))
    small = lambda a: pl.BlockSpec(a.shape, lambda b, h: (0, 0))
    g_cols = jnp.broadcast_to(g.reshape(LANES, 1), (LANES, tq))
    return pl.pallas_call(
        functools.partial(_attn_kernel, tq=tq, hd=hd, lambda_init=lambda_init),
        grid=(n_batch, nh),
        in_specs=[blk, blk, blk] + [small(a) for a in lam_params] + [small(g_cols)],
        out_specs=blk,
        out_shape=jax.ShapeDtypeStruct((n, w), BF16),
        compiler_params=_cparams(("parallel", "parallel")),
        name="attn_prompt",
    )(qb, kb, vb, *lam_params, g_cols)


def _sattn_kernel(pt_ref, q_ref, kown_ref, vown_ref, ck_hbm, cv_hbm, lq1, lk1, lq2, lk2, g_ref,
                  o_ref, kbuf, vbuf, sem, kb_ref, vb_ref, m_ref, l_ref, acc_ref,
                  *, layer, nsteps, total, npg, prow, dq, nh, lambda_init):
    j = pl.program_id(1)
    t = pl.program_id(0) * nsteps + j
    nq = q_ref.shape[1]
    rq = nq // nh
    page = prow // nh
    q = [q_ref[0, h * rq:(h + 1) * rq, :].astype(BF16) for h in range(nh)]

    def page_copy(step, p_, cache, buf, kind):
        page_id = pt_ref[step // nsteps, (step % nsteps) * npg + p_]
        slot = step % PAGE_RING
        return pltpu.make_async_copy(cache.at[layer, page_id], buf.at[slot, p_],
                                     sem.at[kind, slot, p_])

    def start_step(step):
        for p_ in range(npg):
            page_copy(step, p_, ck_hbm, kbuf, 0).start(priority=p_ % 2)
            page_copy(step, p_, cv_hbm, vbuf, 1).start(priority=(p_ + 1) % 2)

    @pl.when(t == 0)
    def _prime():
        for s_ in range(min(PAGE_RING - 1, total)):
            start_step(s_)

    @pl.when(t + (PAGE_RING - 1) < total)
    def _prefetch():
        start_step(t + (PAGE_RING - 1))

    for p_ in range(npg):
        page_copy(t, p_, ck_hbm, kbuf, 0).wait()
        page_copy(t, p_, cv_hbm, vbuf, 1).wait()
    slot = t % PAGE_RING

    @pl.when(j == 0)
    def _init():
        m_ref[...] = jnp.full(m_ref.shape, NEG_INF, F32)
        l_ref[...] = jnp.zeros(l_ref.shape, F32)
        acc_ref[...] = jnp.zeros(acc_ref.shape, F32)

    def update(s, values):
        m_prev = m_ref[...]
        m_new = jnp.maximum(m_prev, jnp.max(s, axis=1, keepdims=True))
        alpha = jnp.exp(m_prev - m_new)
        p = jnp.exp(s - m_new)
        l_ref[...] = alpha * l_ref[...] + jnp.sum(p, axis=1, keepdims=True)
        pv = jnp.concatenate([_dot(p[h * rq:(h + 1) * rq].astype(BF16), values[h])
                              for h in range(nh)], axis=0)
        acc_ref[...] = alpha * acc_ref[...] + pv
        m_ref[...] = m_new

    for p_ in range(npg):
        for h in range(nh):
            rows = pl.ds(h, page, stride=nh)
            kb_ref[h, p_ * page:(p_ + 1) * page, :] = kbuf[slot, p_, rows, :].astype(BF16)
            vb_ref[h, p_ * page:(p_ + 1) * page, :] = vbuf[slot, p_, rows, :].astype(BF16)
    s = jnp.concatenate([_dot_nt(q[h], kb_ref[h]) for h in range(nh)], axis=0)
    update(s, [vb_ref[h] for h in range(nh)])

    @pl.when(j == nsteps - 1)
    def _fin():
        k_own = [kown_ref[0, pl.ds(h, dq, stride=nh), :].astype(BF16) for h in range(nh)]
        v_own = [vown_ref[0, pl.ds(h, dq, stride=nh), :].astype(BF16) for h in range(nh)]
        s_own = jnp.concatenate([_dot_nt(q[h], k_own[h]) for h in range(nh)], axis=0)
        r = lax.broadcasted_iota(jnp.int32, (nq, dq), 0)
        c = lax.broadcasted_iota(jnp.int32, (nq, dq), 1)
        update(jnp.where(c <= r % dq, s_own, NEG_INF), v_own)
        lam = _diff_lambda(lq1, lk1, lq2, lk2, lambda_init)
        o_all = acc_ref[...] / l_ref[...]
        for h in range(nh):
            o = o_all[h * rq:h * rq + dq] - lam * o_all[h * rq + dq:(h + 1) * rq]
            o_ref[0, h * dq:(h + 1) * dq, :] = (_rms(o, g_ref[...])
                                                * (1.0 - lambda_init)).astype(BF16)


def _sattn(page_table, q_rows, k_own, v_own, cache_k, cache_v, layer, lam_params, g, npg, dq, nh,
           lambda_init):
    nb, nq, _ = q_rows.shape
    prow = cache_k.shape[2]
    n_pages = page_table.shape[1]
    nsteps = n_pages // npg
    per_b = lambda a: pl.BlockSpec((1,) + a.shape[1:], lambda b, j, pt: (b, 0, 0))
    small = lambda a: pl.BlockSpec(a.shape, lambda b, j, pt: (0, 0))
    in_hbm = pl.BlockSpec(memory_space=pl.ANY)
    grid_spec = pltpu.PrefetchScalarGridSpec(
        num_scalar_prefetch=1,
        grid=(nb, nsteps),
        in_specs=([per_b(q_rows), per_b(k_own), per_b(v_own), in_hbm, in_hbm]
                  + [small(a) for a in lam_params] + [small(g)]),
        out_specs=pl.BlockSpec((1, nq // 2, LANES), lambda b, j, pt: (b, 0, 0)),
        scratch_shapes=[pltpu.VMEM((PAGE_RING, npg, prow, LANES), cache_k.dtype),
                        pltpu.VMEM((PAGE_RING, npg, prow, LANES), cache_v.dtype),
                        pltpu.SemaphoreType.DMA((2, PAGE_RING, npg)),
                        pltpu.VMEM((nh, npg * prow // nh, LANES), BF16),
                        pltpu.VMEM((nh, npg * prow // nh, LANES), BF16),
                        pltpu.VMEM((nq, 1), F32),
                        pltpu.VMEM((nq, 1), F32),
                        pltpu.VMEM((nq, LANES), F32)])
    return pl.pallas_call(
        functools.partial(_sattn_kernel, layer=layer, nsteps=nsteps, total=nb * nsteps, npg=npg,
                          prow=prow, dq=dq, nh=nh, lambda_init=lambda_init),
        grid_spec=grid_spec,
        out_shape=jax.ShapeDtypeStruct((nb, nq // 2, LANES), BF16),
        compiler_params=_cparams(("arbitrary", "arbitrary")),
        name="attn_sample",
    )(page_table, q_rows, k_own, v_own, cache_k, cache_v, *lam_params, g)


def _merge_kernel(x_ref, gates_ref, y_ref, o_ref, wbs_ref, wba_ref, wout_ref, gffn_ref,
                  wr_ref, br_ref, x1_ref, h2_ref, ri_ref, rf_ref, cnt_ref, run_ref, *, ne):
    first = jnp.logical_and(pl.program_id(0) == 0, pl.program_id(1) == 0)

    @pl.when(first)
    def _init():
        run_ref[...] = jnp.zeros(run_ref.shape, F32)

    tm, d = x_ref.shape
    a = _dot(y_ref[...].astype(BF16), wbs_ref[...])
    b = _dot(o_ref[...], wba_ref[...])
    merged = gates_ref[:, :d].astype(F32) * a + gates_ref[:, d:].astype(F32) * b
    x1 = x_ref[...] + _dot(merged.astype(BF16), wout_ref[...])
    x1_ref[...] = x1
    h2 = _rms(x1, gffn_ref[...])
    h2b = h2.astype(BF16)
    h2_ref[...] = _pack_bf16_pairs(h2)

    logits = _dot(h2b, wr_ref[...]) + br_ref[...]
    lane = lax.broadcasted_iota(jnp.int32, logits.shape, 1)
    lane_f = lane.astype(F32)
    cur = jnp.where(lane < ne, logits, -jnp.inf)
    vals, idxs, hots = [], [], []
    for _ in range(TOP_K):
        mk = jnp.max(cur, axis=1, keepdims=True)
        ik = jnp.min(jnp.where(cur == mk, lane_f, float(LANES)), axis=1, keepdims=True)
        hot = lane_f == ik
        cur = jnp.where(hot, -jnp.inf, cur)
        vals.append(mk)
        idxs.append(ik.astype(jnp.int32))
        hots.append(hot)
    exps = [jnp.exp(v - vals[0]) for v in vals]
    denom = sum(exps[1:], exps[0])
    chosen = functools.reduce(jnp.logical_or, hots)
    cnt = jnp.where(chosen, 1.0, 0.0)
    r = lax.broadcasted_iota(jnp.int32, (tm, tm), 0)
    c = lax.broadcasted_iota(jnp.int32, (tm, tm), 1)
    before = _dot((c < r).astype(BF16), cnt.astype(BF16)) + run_ref[...]
    ri = jnp.zeros(logits.shape, jnp.int32)
    rf = jnp.zeros(logits.shape, F32)
    for k in range(TOP_K):
        rank = jnp.sum(jnp.where(hots[k], before, 0.0), axis=1, keepdims=True).astype(jnp.int32)
        ri = jnp.where(lane == k, idxs[k], ri)
        ri = jnp.where(lane == TOP_K + k, rank, ri)
        rf = jnp.where(lane == k, exps[k] / denom, rf)
    ri_ref[...] = ri.T[:2 * TOP_K]
    rf_ref[...] = rf
    run_ref[...] = run_ref[...] + jnp.sum(cnt, axis=0, keepdims=True)
    cnt_ref[...] = run_ref[...]


def _merge(x2d, gates, y_tb, o, n_batch, t_len, tm, wbs, wba, wout, gffn, wr, br, ne):
    n, d = x2d.shape
    w = o.shape[1]
    nt = t_len // tm
    row = lambda b, t: (b * nt + t, 0)
    const = lambda b, t: (0, 0)
    return pl.pallas_call(
        functools.partial(_merge_kernel, ne=ne),
        grid=(n_batch, nt),
        in_specs=[pl.BlockSpec((tm, d), row),
                  pl.BlockSpec((tm, 2 * d), row),
                  pl.BlockSpec((tm, w), lambda b, t: (t, b)),
                  pl.BlockSpec((tm, w), row),
                  pl.BlockSpec(wbs.shape, const),
                  pl.BlockSpec(wba.shape, const),
                  pl.BlockSpec(wout.shape, const),
                  pl.BlockSpec((1, d), const),
                  pl.BlockSpec(wr.shape, const),
                  pl.BlockSpec((1, LANES), const)],
        out_specs=[pl.BlockSpec((tm, d), row),
                   pl.BlockSpec((tm, d // 2), row),
                   pl.BlockSpec((2 * TOP_K, tm), lambda b, t: (0, b * nt + t)),
                   pl.BlockSpec((tm, LANES), row),
                   pl.BlockSpec((1, LANES), const)],
        out_shape=[jax.ShapeDtypeStruct((n, d), F32),
                   jax.ShapeDtypeStruct((n, d // 2), jnp.uint32),
                   jax.ShapeDtypeStruct((2 * TOP_K, n), jnp.int32),
                   jax.ShapeDtypeStruct((n, LANES), F32),
                   jax.ShapeDtypeStruct((1, LANES), F32)],
        scratch_shapes=[pltpu.VMEM((1, LANES), F32)],
        compiler_params=_cparams(("arbitrary", "arbitrary")),
        name="merge",
    )(x2d, gates, y_tb, o, wbs, wba, wout, gffn, wr, br)


def _moe_kernel(be_ref, nu_ref, x_ref, wgu_ref, bgu_ref, wd_ref, bd_ref, o_ref, wgu_s, wd_s):
    i = pl.program_id(0)
    active = i < nu_ref[0]
    fresh = jnp.logical_or(i == 0, be_ref[i] != be_ref[jnp.maximum(i - 1, 0)])
    pc = PAIR_COLS
    hc = pc // 2
    n_chunk = wgu_ref.shape[2] // pc

    @pl.when(jnp.logical_and(active, fresh))
    def _stage_weights():
        r = lax.broadcasted_iota(jnp.int32, (pc, pc), 0)
        c = lax.broadcasted_iota(jnp.int32, (pc, pc), 1)
        perm = (r == jnp.where(c < hc, 2 * c, 2 * (c - hc) + 1)).astype(BF16)
        for j in range(n_chunk):
            wj = wgu_ref[0, :, j * pc:(j + 1) * pc].astype(BF16)
            wgu_s[:, j * pc:(j + 1) * pc] = _dot(wj, perm).astype(BF16)
        wd_s[...] = wd_ref[0].astype(BF16)

    @pl.when(active)
    def _compute():
        x = _unpack_bf16_pairs(x_ref[...]).astype(BF16)
        gu = _dot(x, wgu_s[...]) + bgu_ref[0]
        acts = []
        for j in range(n_chunk):
            g_lin = jnp.minimum(gu[:, j * pc:j * pc + hc], SWIGLU_LIMIT)
            up = jnp.clip(gu[:, j * pc + hc:(j + 1) * pc], -SWIGLU_LIMIT, SWIGLU_LIMIT)
            acts.append(((up + 1.0) * (g_lin * jax.nn.sigmoid(SWIGLU_ALPHA * g_lin))).astype(BF16))
        o_ref[...] = _pack_bf16_pairs(_dot(jnp.concatenate(acts, axis=1), wd_s[...]) + bd_ref[0])

    @pl.when(jnp.logical_not(active))
    def _skip():
        o_ref[...] = jnp.zeros(o_ref.shape, o_ref.dtype)


def _moe_block_rows(n_tokens, ne):
    return MOE_BLOCK_ROWS if n_tokens * TOP_K >= 8 * ne * MOE_BLOCK_ROWS else MOE_BLOCK_ROWS_SMALL


def _moe_gemm(xs, blk_expert, n_used, wgu, bgu, wd, bd, bm):
    n_rows = xs.shape[0]
    d = wd.shape[2]
    wspec = lambda a: pl.BlockSpec((1,) + a.shape[1:], lambda i, be, nu: (be[i], 0, 0))
    grid_spec = pltpu.PrefetchScalarGridSpec(
        num_scalar_prefetch=2,
        grid=(n_rows // bm,),
        in_specs=[pl.BlockSpec((bm, xs.shape[1]), lambda i, be, nu: (i, 0)),
                  wspec(wgu), wspec(bgu), wspec(wd), wspec(bd)],
        out_specs=pl.BlockSpec((bm, d // 2), lambda i, be, nu: (i, 0)),
        scratch_shapes=[pltpu.VMEM(wgu.shape[1:], BF16),
                        pltpu.VMEM(wd.shape[1:], BF16)])
    return pl.pallas_call(
        _moe_kernel,
        grid_spec=grid_spec,
        out_shape=jax.ShapeDtypeStruct((n_rows, d // 2), jnp.uint32),
        compiler_params=_cparams(("arbitrary",)),
        name="moe_gemm",
    )(blk_expert, n_used, xs, wgu, bgu, wd, bd)


def _combine_kernel(x_ref, *rest, final_norm):
    yg_refs = rest[:TOP_K]
    gate_ref, g_ref, o_ref = rest[TOP_K:]
    acc = x_ref[...]
    gate = gate_ref[...]
    for k in range(TOP_K):
        acc = acc + gate[:, k:k + 1] * _unpack_bf16_pairs(yg_refs[k][...])
    o_ref[...] = _rms(acc, g_ref[...]) if final_norm else acc


def _combine(x1, ygs, gate, g, tm, final_norm):
    n, d = x1.shape
    row = pl.BlockSpec((tm, d), lambda i: (i, 0))
    packed = pl.BlockSpec((tm, d // 2), lambda i: (i, 0))
    return pl.pallas_call(
        functools.partial(_combine_kernel, final_norm=final_norm),
        grid=(n // tm,),
        in_specs=[row] + [packed] * TOP_K + [pl.BlockSpec((tm, LANES), lambda i: (i, 0)),
                                          pl.BlockSpec((1, d), lambda i: (0, 0))],
        out_specs=row,
        out_shape=jax.ShapeDtypeStruct((n, d), F32),
        compiler_params=_cparams(("parallel",)),
        name="combine",
    )(x1, *ygs, gate, g)


def _dest_kernel(pstart_ref, ri_ref, o_ref):
    ri = ri_ref[...]
    base = jnp.zeros(ri.shape, jnp.int32)
    for e in range(pstart_ref.shape[0]):
        base = jnp.where(ri == e, pstart_ref[e], base)
    o_ref[...] = (base + pltpu.roll(ri, TOP_K, axis=0))[:TOP_K]


def _dest_rows(pstart, route_i):
    n = route_i.shape[1]
    grid_spec = pltpu.PrefetchScalarGridSpec(
        num_scalar_prefetch=1,
        grid=(1,),
        in_specs=[pl.BlockSpec(route_i.shape, lambda i, ps: (0, 0))],
        out_specs=pl.BlockSpec((TOP_K, n), lambda i, ps: (0, 0)))
    return pl.pallas_call(
        _dest_kernel,
        grid_spec=grid_spec,
        out_shape=jax.ShapeDtypeStruct((TOP_K, n), jnp.int32),
        compiler_params=_cparams(("arbitrary",)),
        name="dest_rows",
    )(pstart, route_i)


def _moe_dispatch(h2p, route_i, count_row, ne):
    n = h2p.shape[0]
    bm = _moe_block_rows(n, ne)
    m = n * TOP_K
    counts = count_row[0, :ne].astype(jnp.int32)
    start = jnp.cumsum(counts) - counts
    padded = (counts + bm - 1) // bm * bm
    pend = jnp.cumsum(padded)
    pstart = pend - padded
    dest_rows = _dest_rows(pstart.astype(jnp.int32), route_i)
    dest = [dest_rows[k] for k in range(TOP_K)]
    n_rows = (m + bm - 1) // bm * bm + ne * bm
    blk_start = jnp.arange(n_rows // bm, dtype=jnp.int32) * bm
    blk_expert = jnp.minimum(
        jnp.sum((blk_start[:, None] >= pend[None, :]).astype(jnp.int32), axis=1), ne - 1)
    n_used = (pend[-1] // bm).astype(jnp.int32).reshape(1)
    tok = jnp.tile(jnp.arange(n, dtype=jnp.int32), TOP_K)
    _, tok_sorted = lax.sort_key_val(dest_rows.reshape(m), tok)
    blk_shift = (pstart - start)[blk_expert]
    src = jnp.arange(n_rows, dtype=jnp.int32) - jnp.repeat(blk_shift, bm)
    row_tok = tok_sorted[jnp.clip(src, 0, m - 1)]
    return dict(xs=h2p[row_tok], blk_expert=blk_expert, n_used=n_used, dest=dest, bm=bm)


def _moe_apply(x1, disp, route_f, mp, g_final, final_norm):
    ys = _moe_gemm(disp['xs'], disp['blk_expert'], disp['n_used'],
                   mp['wgu'], mp['bgu'], mp['wd'], mp['bd'], disp['bm'])
    ygs = [ys[disp['dest'][k]] for k in range(TOP_K)]
    tm = min(ROW_TILE, x1.shape[0])
    return _combine(x1, ygs, route_f, g_final, tm, final_norm)


def _rope_tables(pos, hd):
    half = hd // 2
    inv_freq = ROPE_THETA ** (-jnp.arange(half, dtype=F32) / half)
    ang = pos.astype(F32)[:, None] * inv_freq[None, :]
    cos = jnp.cos(ang)
    sin = jnp.sin(ang)
    reps = LANES // hd
    cos_t = jnp.tile(jnp.concatenate([cos, cos], axis=1), (1, reps))
    sin_t = jnp.tile(jnp.concatenate([-sin, sin], axis=1), (1, reps))
    return cos_t, sin_t


def _layer_params(lp):
    d = lp['w_in'].shape[0]
    ne = lp['w_router'].shape[1]
    wr = jnp.zeros((d, LANES), F32).at[:, :ne].set(lp['w_router']).astype(BF16)
    br = jnp.zeros((1, LANES), F32).at[0, :ne].set(lp['b_router'])
    return dict(
        g_mix=lp['norm_mix_g'].reshape(1, d),
        win=lp['w_in'].astype(BF16),
        wgate=lp['w_gate'].astype(BF16),
        bgate=lp['b_gate'].reshape(1, -1),
        s5=_s5_layout(lp),
        lam=[lp[k].reshape(1, -1) for k in ('lambda_q1', 'lambda_k1', 'lambda_q2', 'lambda_k2')],
        g_sub=lp['attn_subln_g'].reshape(1, -1),
        wbs=lp['w_branch_ssm'].astype(BF16),
        wba=lp['w_branch_attn'].astype(BF16),
        wout=lp['w_out'].astype(BF16),
        g_ffn=lp['norm_ffn_g'].reshape(1, d),
        wr=wr, br=br,
        moe=dict(ne=ne,
                 wgu=lp['w_gate_up'],
                 bgu=lp['b_gate_up'].reshape(ne, -1, PAIR_COLS // 2, 2).transpose(0, 1, 3, 2)
                 .reshape(ne, 1, -1),
                 wd=lp['w_down'],
                 bd=lp['b_down'][:, None, :]))


def _query_rows(qb, bsz, dq, nh, hd):
    qt = qb.astype(F32).reshape(bsz, dq, nh, 2, hd).transpose(0, 2, 3, 1, 4)
    ec = jnp.eye(2, dtype=jnp.bool_)[None, None, :, None, :, None]
    out = jnp.where(ec, qt[:, :, :, :, None, :], 0.0)
    return out.reshape(bsz, nh * 2 * dq, 2 * hd)


def _layer(x, pos, x0_re, x0_im, past, P, lambda_init):
    bsz, t_len, d = x.shape
    n = bsz * t_len
    x2d = x.reshape(n, d)
    sp = P['s5']
    g, p = x0_re.shape[1], x0_re.shape[2]
    w = P['win'].shape[1] // 4
    nh = w // LANES
    hd = LANES // 2
    cos_t, sin_t = _rope_tables(pos, hd)
    x0 = _state_to_cols(x0_re.astype(F32), x0_im.astype(F32), sp['ns'])
    if past is None:
        n_batch, rows_t, tm = bsz, t_len, min(ROW_TILE, t_len)
    else:
        n_batch, rows_t, tm = 1, n, n
        cos_t = jnp.tile(cos_t, (bsz, 1))
        sin_t = jnp.tile(sin_t, (bsz, 1))
    u, qb, k, v, kb, vb, gates = _proj(x2d, n_batch, rows_t, tm, P['g_mix'], P['win'], P['wgate'],
                                       P['bgate'], cos_t, sin_t, hd // 2, hd ** -0.5)
    if past is None:
        y_in, st = _s5(u, x0, sp, bsz, t_len, min(64, t_len), BF16)
        o = _attn(qb, kb, vb, P['lam'], P['g_sub'], bsz, t_len, min(256, t_len), hd, lambda_init)
    else:
        cache_k, cache_v, page_table, layer = past
        u_tb = u.reshape(bsz, t_len, w).transpose(1, 0, 2).reshape(t_len, bsz * w)
        y_tb, st = _s5(u_tb, x0, sp, bsz, t_len, t_len, F32)
        y_in = y_tb.reshape(t_len, bsz, w).transpose(1, 0, 2).reshape(n, w)
        npg = math.gcd(PAGES_PER_STEP, page_table.shape[1])
        cshape = cache_k.shape[:2] + (cache_k.shape[2] * nh, LANES)
        o = _sattn(page_table, _query_rows(qb, bsz, t_len, nh, hd),
                   k.reshape(bsz, t_len * nh, LANES), v.reshape(bsz, t_len * nh, LANES),
                   cache_k.reshape(cshape), cache_v.reshape(cshape), layer,
                   P['lam'], P['g_sub'], npg, t_len, nh, lambda_init)
        o = o.reshape(bsz, nh, t_len, LANES).transpose(0, 2, 1, 3).reshape(n, w)
    x1, h2p, route_i, route_f, count_row = _merge(
        x2d, gates, y_in, o, n_batch, rows_t, tm, P['wbs'], P['wba'], P['wout'], P['g_ffn'],
        P['wr'], P['br'], P['moe']['ne'])
    disp = _moe_dispatch(h2p, route_i, count_row, P['moe']['ne'])
    st_re, st_im = _cols_to_state(st, sp['ns'], g, p)
    side = (k.reshape(bsz, t_len, nh, LANES), v.reshape(bsz, t_len, nh, LANES),
            st_re.astype(x.dtype), st_im.astype(x.dtype))
    return (x1, disp, route_f), side


def kernel(x_prompt, x_sample, cache_k, cache_v, state_ssm_re, state_ssm_im, page_table, norm_mix_g, w_in, ssm_lambda_re, ssm_lambda_im, ssm_log_dt, ssm_b_re, ssm_b_im, ssm_c_re, ssm_c_im, ssm_d, ssm_w_glu, ssm_b_glu, lambda_q1, lambda_k1, lambda_q2, lambda_k2, attn_subln_g, w_branch_ssm, w_branch_attn, w_gate, b_gate, w_out, norm_ffn_g, w_router, b_router, w_gate_up, b_gate_up, w_down, b_down, norm_final_g):
    depth = w_in.shape[0]
    past_len = page_table.shape[1] * cache_k.shape[2]
    pos_prompt = jnp.arange(x_prompt.shape[1], dtype=jnp.int32)
    pos_sample = past_len + jnp.arange(x_sample.shape[1], dtype=jnp.int32)
    g, p = state_ssm_re.shape[2], state_ssm_re.shape[3]
    zero_state = jnp.zeros((x_prompt.shape[0], g, p), F32)
    g_final = norm_final_g.reshape(1, -1)
    names = ('norm_mix_g', 'w_in', 'ssm_lambda_re', 'ssm_lambda_im', 'ssm_log_dt', 'ssm_b_re',
             'ssm_b_im', 'ssm_c_re', 'ssm_c_im', 'ssm_d', 'ssm_w_glu', 'ssm_b_glu', 'lambda_q1',
             'lambda_k1', 'lambda_q2', 'lambda_k2', 'attn_subln_g', 'w_branch_ssm',
             'w_branch_attn', 'w_gate', 'b_gate', 'w_out', 'norm_ffn_g', 'w_router', 'b_router',
             'w_gate_up', 'b_gate_up', 'w_down', 'b_down')
    stacked = (norm_mix_g, w_in, ssm_lambda_re, ssm_lambda_im, ssm_log_dt, ssm_b_re, ssm_b_im,
               ssm_c_re, ssm_c_im, ssm_d, ssm_w_glu, ssm_b_glu, lambda_q1, lambda_k1, lambda_q2,
               lambda_k2, attn_subln_g, w_branch_ssm, w_branch_attn, w_gate, b_gate, w_out,
               norm_ffn_g, w_router, b_router, w_gate_up, b_gate_up, w_down, b_down)
    xp, xs = x_prompt, x_sample
    outs_p, outs_s = [], []
    for l in range(depth):
        P = _layer_params({nm: a[l] for nm, a in zip(names, stacked)})
        lambda_init = 0.8 - 0.6 * math.exp(-0.3 * l)
        last = l == depth - 1
        moe_p, side_p = _layer(xp, pos_prompt, zero_state, zero_state, None, P, lambda_init)
        moe_s, side_s = _layer(xs, pos_sample, state_ssm_re[l], state_ssm_im[l],
                               (cache_k, cache_v, page_table, l), P, lambda_init)
        (x1_p, disp_p, rf_p), (x1_s, disp_s, rf_s) = moe_p, moe_s
        n_used_p, x1_s = lax.optimization_barrier((disp_p['n_used'], x1_s))
        disp_p = dict(disp_p, n_used=n_used_p)
        xp = _moe_apply(x1_p, disp_p, rf_p, P['moe'], g_final, last).reshape(xp.shape)
        xs = _moe_apply(x1_s, disp_s, rf_s, P['moe'], g_final, last).reshape(xs.shape)
        outs_p.append(side_p)
        outs_s.append(side_s)
    stack = lambda outs, i: jnp.stack([o[i] for o in outs])
    return (xp, xs, stack(outs_p, 0), stack(outs_p, 1), stack(outs_p, 2), stack(outs_p, 3),
            stack(outs_s, 0), stack(outs_s, 1), stack(outs_s, 2), stack(outs_s, 3))
```
